```python
import math
import jax, jax.numpy as jnp
from jax import lax
import numpy as np

D_MODEL = 1024
BATCH = 1
SEQ = 16384
DEPTH = 4
DEC_BATCH = 32
DEC_SEQ = 16
PAST_LEN = 2048

CHUNK = 64
N_A = DEPTH // 2
N_B = DEPTH - N_A
A_HEADS = 8
A_DK = D_MODEL // A_HEADS
A_DV = D_MODEL // A_HEADS
CONV_W = 4
A_QKV = A_HEADS * (2 * A_DK + A_DV)
A_IN = A_QKV + A_HEADS * A_DV + 2 * A_HEADS
B_HEADS = 8
B_DH = D_MODEL // (2 * B_HEADS)
B_DK = 2 * B_DH
B_DV = 2 * B_DH
Q_BLOCK = 128
D_FF = 4 * D_MODEL
EPS = 1e-6

kernel_name = "yoco_gdn_diffattn_stream_step"


def rmsnorm(x, g):
    xf = x.astype(jnp.float32)
    y = xf * lax.rsqrt(jnp.mean(xf * xf, axis=-1, keepdims=True) + EPS)
    return (y * g.astype(jnp.float32)).astype(x.dtype)


def l2norm(x):
    xf = x.astype(jnp.float32)
    return xf * lax.rsqrt(jnp.sum(xf * xf, axis=-1, keepdims=True) + EPS)


def causal_conv(x, buf, w):
    xp = jnp.concatenate([buf.astype(x.dtype), x], axis=1)
    T = x.shape[1]
    y = sum(xp[:, j:j + T] * w[j] for j in range(CONV_W))
    return jax.nn.silu(y), xp[:, -(CONV_W - 1):]


def gdn_chunk(S, q, k, v, beta, g):
    C = q.shape[2]
    gc = jnp.cumsum(g, axis=-1)
    incl = jnp.tril(jnp.ones((C, C), bool))
    strict = jnp.tril(jnp.ones((C, C), bool), -1)
    decay = jnp.exp(jnp.where(incl, gc[..., :, None] - gc[..., None, :], -jnp.inf))
    kk = jnp.einsum('bhtd,bhsd->bhts', k, k)
    m = jnp.where(strict, beta[..., None] * kk * decay, 0.0) + jnp.eye(C, dtype=kk.dtype)
    rhs = jnp.concatenate([beta[..., None] * v, (beta * jnp.exp(gc))[..., None] * k], axis=-1)
    sol = lax.linalg.triangular_solve(m, rhs, left_side=True, lower=True, unit_diagonal=True)
    u = sol[..., :A_DV] - jnp.einsum('bhck,bhkv->bhcv', sol[..., A_DV:], S)
    qk = jnp.einsum('bhtd,bhsd->bhts', q, k) * decay
    o = jnp.exp(gc)[..., None] * jnp.einsum('bhtk,bhkv->bhtv', q, S) + jnp.einsum('bhts,bhsv->bhtv', qk, u)
    gl = gc[..., -1:]
    S_new = jnp.exp(gl)[..., None] * S + jnp.einsum('bhck,bhcv->bhkv', jnp.exp(gl - gc)[..., None] * k, u)
    return S_new, o


def gdn_prompt(S0, q, k, v, beta, g):
    bsz, H, T = q.shape[:3]
    n = T // CHUNK

    def split(a):
        a = a.reshape(a.shape[:2] + (n, CHUNK) + a.shape[3:])
        return jnp.moveaxis(a, 2, 0)

    def step(S, xs):
        return gdn_chunk(S, *xs)

    S, o = lax.scan(step, S0, (split(q), split(k), split(v), split(beta), split(g)))
    o = jnp.moveaxis(o, 0, 2).reshape(bsz, H, T, A_DV)
    return S, o


def gdn_mixer(h, conv_buf, S0, w_in, conv_w, a_log, dt_bias, o_gain, w_out):
    bsz, T, _ = h.shape
    f32 = jnp.float32
    proj = h @ w_in
    qkv_raw, z, b_raw, a_raw = jnp.split(
        proj, [A_QKV, A_QKV + A_HEADS * A_DV, A_QKV + A_HEADS * A_DV + A_HEADS], axis=-1)
    qkv, new_buf = causal_conv(qkv_raw, conv_buf, conv_w)
    q, k, v = jnp.split(qkv, [A_HEADS * A_DK, 2 * A_HEADS * A_DK], axis=-1)

    def heads(a):
        return a.reshape(bsz, T, A_HEADS, -1).transpose(0, 2, 1, 3)

    q = l2norm(heads(q)) * (A_DK ** -0.5)
    k = l2norm(heads(k))
    v = heads(v).astype(f32)
    beta = jax.nn.sigmoid(b_raw.astype(f32)).transpose(0, 2, 1)
    g = (-jnp.exp(a_log.astype(f32)) * jax.nn.softplus(a_raw.astype(f32) + dt_bias.astype(f32))).transpose(0, 2, 1)
    run = gdn_prompt if T > CHUNK else gdn_chunk
    S, o = run(S0.astype(f32), q, k, v, beta, g)
    o = rmsnorm(o, o_gain) * jax.nn.silu(heads(z).astype(f32))
    o = o.transpose(0, 2, 1, 3).reshape(bsz, T, A_HEADS * A_DV).astype(h.dtype)
    return o @ w_out, new_buf, S


def shared_kv(x, kv_gain, w_kv):
    hn = rmsnorm(x, kv_gain)
    kv = hn @ w_kv
    bsz, T, _ = kv.shape
    k = kv[..., :B_HEADS * B_DK].reshape(bsz, T, B_HEADS, B_DK)
    v = kv[..., B_HEADS * B_DK:].reshape(bsz, T, B_HEADS, B_DV)
    return k, v


def diff_attend(q, k, v, q_pos, k_pos, lam):
    bsz, Tq = q.shape[:2]
    Tk = k.shape[1]
    qf = q.astype(jnp.float32).reshape(bsz, Tq, B_HEADS, 2, B_DH)
    kf = k.astype(jnp.float32).reshape(bsz, Tk, B_HEADS, 2, B_DH)
    s = jnp.einsum('bqhmd,bkhmd->bmhqk', qf, kf) * (B_DH ** -0.5)
    slopes = 2.0 ** (-8.0 * jnp.arange(1, B_HEADS + 1, dtype=jnp.float32) / B_HEADS)
    dist = jnp.abs(q_pos[:, None] - k_pos[None, :]).astype(jnp.float32)
    bias = -slopes[:, None, None] * dist
    allowed = (k_pos[None, :] // CHUNK) <= (q_pos[:, None] // CHUNK)
    p = jax.nn.softmax(jnp.where(allowed, s + bias, -jnp.inf), axis=-1)
    a = p[:, 0] - lam * p[:, 1]
    return jnp.einsum('bhqk,bkhd->bqhd', a, v.astype(jnp.float32))


def diff_mixer(h, k, v, pos0, blocked, w_q, lam_p, sub_gain, w_out, lam_init):
    bsz, T, _ = h.shape
    q = (h @ w_q).reshape(bsz, T, B_HEADS, B_DK)
    lp = lam_p.astype(jnp.float32)
    lam = jnp.exp(jnp.sum(lp[0] * lp[1])) - jnp.exp(jnp.sum(lp[2] * lp[3])) + lam_init
    Tk = k.shape[1]
    k_pos = jnp.arange(Tk)
    if blocked:
        n = T // Q_BLOCK
        qb = jnp.moveaxis(q.reshape(bsz, n, Q_BLOCK, B_HEADS, B_DK), 1, 0)

        def blk(args):
            qi, i = args
            return diff_attend(qi, k, v, i * Q_BLOCK + jnp.arange(Q_BLOCK), k_pos, lam)

        o = lax.map(blk, (qb, jnp.arange(n)))
        o = jnp.moveaxis(o, 0, 1).reshape(bsz, T, B_HEADS, B_DV)
    else:
        o = diff_attend(q, k, v, pos0 + jnp.arange(T), k_pos, lam)
    o = rmsnorm(o, sub_gain) * (1.0 - lam_init)
    return o.reshape(bsz, T, B_HEADS * B_DV).astype(h.dtype) @ w_out


def sq_relu_mlp(h, w1, w2):
    return jnp.square(jax.nn.relu(h @ w1)) @ w2


def trunk(x, conv_state, ssm_state, past_k, past_v, norm_gains, a_w_in, a_conv_w, a_log, a_dt_bias,
          a_o_gain, a_w_out, kv_gain, w_kv, b_w_q, b_lam, b_sub_gain, b_w_out, mlp_w1, mlp_w2):
    T = x.shape[1]
    pos0 = 0 if past_k is None else past_k.shape[1]
    blocked = past_k is None
    new_conv, new_ssm = [], []
    k_new = v_new = k_all = v_all = None
    for l in range(DEPTH):
        h = rmsnorm(x, norm_gains[l, 0])
        if l < N_A:
            mix, buf, S = gdn_mixer(h, conv_state[l], ssm_state[l], a_w_in[l], a_conv_w[l], a_log[l],
                                    a_dt_bias[l], a_o_gain[l], a_w_out[l])
            new_conv.append(buf)
            new_ssm.append(S)
        else:
            j = l - N_A
            if j == 0:
                k_new, v_new = shared_kv(x, kv_gain, w_kv)
                if past_k is None:
                    k_all, v_all = k_new, v_new
                else:
                    k_all = jnp.concatenate([past_k.astype(k_new.dtype), k_new], axis=1)
                    v_all = jnp.concatenate([past_v.astype(v_new.dtype), v_new], axis=1)
            lam_init = 0.8 - 0.6 * math.exp(-0.3 * l)
            mix = diff_mixer(h, k_all, v_all, pos0, blocked, b_w_q[j], b_lam[j], b_sub_gain[j],
                             b_w_out[j], lam_init)
        x = x + rmsnorm(mix, norm_gains[l, 1])
        h = rmsnorm(x, norm_gains[l, 2])
        x = x + rmsnorm(sq_relu_mlp(h, mlp_w1[l], mlp_w2[l]), norm_gains[l, 3])
    return x, jnp.stack(new_conv), jnp.stack(new_ssm), k_new, v_new


def setup_inputs(seed: int = 0) -> dict:
    key = jax.random.key(seed)
    ks = jax.random.split(key, 24)
    nrm = jax.random.normal
    f32 = jnp.float32
    dt = jnp.exp(jax.random.uniform(ks[0], (N_A, A_HEADS), f32) * (math.log(0.1) - math.log(0.001)) + math.log(0.001))
    return {
        'x_prompt': nrm(ks[1], (BATCH, SEQ, D_MODEL), f32),
        'x_sample': nrm(ks[2], (DEC_BATCH, DEC_SEQ, D_MODEL), f32),
        'state_conv': nrm(ks[3], (N_A, DEC_BATCH, CONV_W - 1, A_QKV), f32),
        'state_ssm': nrm(ks[4], (N_A, DEC_BATCH, A_HEADS, A_DK, A_DV), f32) * (A_DK ** -0.5),
        'cache_k': nrm(ks[5], (DEC_BATCH, PAST_LEN, B_HEADS, B_DK), f32),
        'cache_v': nrm(ks[6], (DEC_BATCH, PAST_LEN, B_HEADS, B_DV), f32),
        'norm_gains': 1.0 + 0.02 * nrm(ks[7], (DEPTH, 4, D_MODEL), f32),
        'a_w_in': nrm(ks[8], (N_A, D_MODEL, A_IN), f32) * (D_MODEL ** -0.5),
        'a_conv_w': nrm(ks[9], (N_A, CONV_W, A_QKV), f32) * (CONV_W ** -0.5),
        'a_log': jnp.log(jax.random.uniform(ks[10], (N_A, A_HEADS), f32, 1.0, 16.0)),
        'a_dt_bias': dt + jnp.log(-jnp.expm1(-dt)),
        'a_o_gain': 1.0 + 0.02 * nrm(ks[11], (N_A, A_DV), f32),
        'a_w_out': nrm(ks[12], (N_A, A_HEADS * A_DV, D_MODEL), f32) * ((A_HEADS * A_DV) ** -0.5),
        'kv_gain': 1.0 + 0.02 * nrm(ks[13], (D_MODEL,), f32),
        'w_kv': nrm(ks[14], (D_MODEL, B_HEADS * (B_DK + B_DV)), f32) * (D_MODEL ** -0.5),
        'b_w_q': nrm(ks[15], (N_B, D_MODEL, B_HEADS * B_DK), f32) * (D_MODEL ** -0.5),
        'b_lam': 0.1 * nrm(ks[16], (N_B, 4, B_DH), f32),
        'b_sub_gain': 1.0 + 0.02 * nrm(ks[17], (N_B, B_DV), f32),
        'b_w_out': nrm(ks[18], (N_B, B_HEADS * B_DV, D_MODEL), f32) * ((B_HEADS * B_DV) ** -0.5),
        'mlp_w1': nrm(ks[19], (DEPTH, D_MODEL, D_FF), f32) * (D_MODEL ** -0.5),
        'mlp_w2': nrm(ks[20], (DEPTH, D_FF, D_MODEL), f32) * (D_FF ** -0.5),
    }


def reference(x_prompt, x_sample, state_conv, state_ssm, cache_k, cache_v, norm_gains, a_w_in, a_conv_w,
              a_log, a_dt_bias, a_o_gain, a_w_out, kv_gain, w_kv, b_w_q, b_lam, b_sub_gain, b_w_out,
              mlp_w1, mlp_w2):
    weights = (norm_gains, a_w_in, a_conv_w, a_log, a_dt_bias, a_o_gain, a_w_out, kv_gain, w_kv,
               b_w_q, b_lam, b_sub_gain, b_w_out, mlp_w1, mlp_w2)
    bp = x_prompt.shape[0]
    conv0 = jnp.zeros((N_A, bp, CONV_W - 1, A_QKV), x_prompt.dtype)
    ssm0 = jnp.zeros((N_A, bp, A_HEADS, A_DK, A_DV), jnp.float32)
    y_prompt, p_conv, p_ssm, p_k, p_v = trunk(x_prompt, conv0, ssm0, None, None, *weights)
    y_sample, s_conv, s_ssm, s_k, s_v = trunk(x_sample, state_conv, state_ssm, cache_k, cache_v, *weights)
    return (y_prompt, y_sample, p_conv, p_ssm, p_k, p_v, s_conv, s_ssm, s_k, s_v)
```

```python
import functools
import math

import jax
import jax.numpy as jnp
from jax import lax
from jax.experimental import pallas as pl
from jax.experimental.pallas import tpu as pltpu

F32 = jnp.float32
BF16 = jnp.bfloat16

D_MODEL = 1024
DEPTH = 4
N_A = DEPTH // 2
CHUNK = 64
CHUNK_SHIFT = CHUNK.bit_length() - 1
assert 1 << CHUNK_SHIFT == CHUNK
HEADS = 8
HEAD_DIM = D_MODEL // HEADS
CONV_W = 4
A_QKV = 3 * D_MODEL
A_GATE_COL = A_QKV + D_MODEL
LANES = 128
A_IN_PAD = A_GATE_COL + LANES
MAP_DIM = HEAD_DIM // 2
D_FF = 4 * D_MODEL
EPS = 1e-6
NEG = -1e30
VMEM_LIMIT = 48 * 1024 * 1024
HIGHEST = lax.Precision.HIGHEST


def _dot(a, b, precision=None):
    return jnp.dot(a, b, preferred_element_type=F32, precision=precision)


def _dot_nt(a, b):
    return lax.dot_general(a, b, (((1,), (1,)), ((), ())), preferred_element_type=F32)


def _dot_tn(a, b):
    return lax.dot_general(a, b, (((0,), (0,)), ((), ())), preferred_element_type=F32)


def _rms(x, gain):
    return x * lax.rsqrt(jnp.mean(x * x, axis=-1, keepdims=True) + EPS) * gain


def _sigmoid(x):
    return 1.0 / (1.0 + jnp.exp(-x))


def _params(*semantics):
    return pltpu.CompilerParams(dimension_semantics=semantics, vmem_limit_bytes=VMEM_LIMIT)


def _norm_matmul_kernel(x_ref, g_ref, w_ref, o_ref, hn_ref):
    @pl.when(pl.program_id(1) == 0)
    def _():
        hn_ref[...] = _rms(x_ref[...], g_ref[...]).astype(BF16)

    o_ref[...] = _dot(hn_ref[...], w_ref[...]).astype(o_ref.dtype)


def norm_matmul(x, gain, w, *, tm, tn, out_dtype=F32):
    m, k = x.shape
    n = w.shape[1]
    return pl.pallas_call(
        _norm_matmul_kernel,
        grid=(m // tm, n // tn),
        in_specs=[
            pl.BlockSpec((tm, k), lambda i, j: (i, 0)),
            pl.BlockSpec((1, k), lambda i, j: (0, 0)),
            pl.BlockSpec((k, tn), lambda i, j: (0, j)),
        ],
        out_specs=pl.BlockSpec((tm, tn), lambda i, j: (i, j)),
        out_shape=jax.ShapeDtypeStruct((m, n), out_dtype),
        scratch_shapes=[pltpu.VMEM((tm, k), BF16)],
        compiler_params=_params("parallel", "arbitrary"),
        name="norm_matmul",
    )(x, gain.reshape(1, k), w)


def _matmul_norm_res_kernel(a_ref, w_ref, g_ref, x_ref, o_ref):
    y = _dot(a_ref[...], w_ref[...])
    o_ref[...] = x_ref[...] + _rms(y, g_ref[...])


def matmul_norm_res(a, w, gain, x, *, tm):
    m, k = a.shape
    d = w.shape[1]
    return pl.pallas_call(
        _matmul_norm_res_kernel,
        grid=(m // tm,),
        in_specs=[
            pl.BlockSpec((tm, k), lambda i: (i, 0)),
            pl.BlockSpec((k, d), lambda i: (0, 0)),
            pl.BlockSpec((1, d), lambda i: (0, 0)),
            pl.BlockSpec((tm, d), lambda i: (i, 0)),
        ],
        out_specs=pl.BlockSpec((tm, d), lambda i: (i, 0)),
        out_shape=jax.ShapeDtypeStruct((m, d), F32),
        compiler_params=_params("parallel"),
        name="matmul_norm_res",
    )(a, w, gain.reshape(1, d), x)


def _mlp_kernel(x_ref, g_in_ref, w1_ref, w2_ref, g_out_ref, o_ref, hn_ref, acc_ref):
    f = pl.program_id(1)

    @pl.when(f == 0)
    def _():
        hn_ref[...] = _rms(x_ref[...], g_in_ref[...]).astype(BF16)
        acc_ref[...] = jnp.zeros_like(acc_ref)

    h = _dot(hn_ref[...], w1_ref[...])
    h = jnp.square(jnp.maximum(h, 0.0)).astype(BF16)
    acc_ref[...] += _dot(h, w2_ref[...])

    @pl.when(f == pl.num_programs(1) - 1)
    def _():
        o_ref[...] = x_ref[...] + _rms(acc_ref[...], g_out_ref[...])


def mlp_block(x, g_in, w1, w2, g_out, *, tm, tf):
    m, d = x.shape
    ff = w1.shape[1]
    return pl.pallas_call(
        _mlp_kernel,
        grid=(m // tm, ff // tf),
        in_specs=[
            pl.BlockSpec((tm, d), lambda i, f: (i, 0)),
            pl.BlockSpec((1, d), lambda i, f: (0, 0)),
            pl.BlockSpec((d, tf), lambda i, f: (0, f)),
            pl.BlockSpec((tf, d), lambda i, f: (f, 0)),
            pl.BlockSpec((1, d), lambda i, f: (0, 0)),
        ],
        out_specs=pl.BlockSpec((tm, d), lambda i, f: (i, 0)),
        out_shape=jax.ShapeDtypeStruct((m, d), F32),
        scratch_shapes=[pltpu.VMEM((tm, d), BF16), pltpu.VMEM((tm, d), F32)],
        compiler_params=_params("parallel", "arbitrary"),
        name="mlp_block",
    )(x, g_in.reshape(1, d), w1, w2, g_out.reshape(1, d))


CONV_ROW0 = 8


def _gdn_kernel(proj_ref, conv0_ref, s0_ref, cw_ref, gp_ref, og_ref, o_ref, conv_ref, s_ref, xbuf_ref, *, C):
    prev0 = CONV_ROW0 - (CONV_W - 1)

    @pl.when(pl.program_id(1) == 0)
    def _():
        xbuf_ref[prev0:CONV_ROW0, :] = conv0_ref[0]
        s_ref[0] = s0_ref[0]

    xbuf_ref[CONV_ROW0:CONV_ROW0 + C, :] = proj_ref[0, :, 0:A_QKV]
    cw = cw_ref[...]
    y = xbuf_ref[prev0:prev0 + C, :] * cw[0:1, :]
    for j in range(1, CONV_W):
        y = y + xbuf_ref[prev0 + j:prev0 + j + C, :] * cw[j:j + 1, :]
    y = y * _sigmoid(y)
    last = xbuf_ref[prev0 + C:CONV_ROW0 + C, :]
    xbuf_ref[prev0:CONV_ROW0, :] = last
    conv_ref[0] = last

    tail = proj_ref[0, :, A_GATE_COL:A_IN_PAD]
    gp = gp_ref[...]
    beta = _sigmoid(tail)
    t = tail + gp[1:2, :]
    softplus = jnp.maximum(t, 0.0) + jnp.log(1.0 + jnp.exp(-jnp.abs(t)))
    g = -jnp.exp(gp[0:1, :]) * softplus

    row = lax.broadcasted_iota(jnp.int32, (C, C), 0)
    col = lax.broadcasted_iota(jnp.int32, (C, C), 1)
    incl = row >= col
    strict = row > col
    eye = (row == col).astype(F32)
    gc = _dot(incl.astype(F32), g, HIGHEST)
    gc_t = jnp.concatenate([gc, jnp.zeros((LANES - C, LANES), F32)], axis=0).T
    eg = jnp.exp(gc)
    gl = gc[C - 1:C, :]
    egl = jnp.exp(gl)
    ekd = jnp.exp(gl - gc)

    n_double = C.bit_length() - 2
    for h in range(HEADS):
        hs = slice(h * HEAD_DIM, (h + 1) * HEAD_DIM)
        q = y[:, hs]
        k = y[:, D_MODEL + h * HEAD_DIM:D_MODEL + (h + 1) * HEAD_DIM]
        v = y[:, 2 * D_MODEL + h * HEAD_DIM:2 * D_MODEL + (h + 1) * HEAD_DIM]
        q = q * lax.rsqrt(jnp.sum(q * q, axis=-1, keepdims=True) + EPS) * (HEAD_DIM ** -0.5)
        k = k * lax.rsqrt(jnp.sum(k * k, axis=-1, keepdims=True) + EPS)
        gl_h = HEADS + h
        b_col = beta[:, h:h + 1]
        g_col = gc[:, gl_h:gl_h + 1]
        g_row = gc_t[gl_h:gl_h + 1, 0:C]
        eg_col = eg[:, gl_h:gl_h + 1]
        decay = jnp.where(incl, jnp.exp(jnp.minimum(g_col - g_row, 0.0)), 0.0)

        k_bf = k.astype(BF16)
        qk_kk = _dot_nt(jnp.concatenate([q.astype(BF16), k_bf], axis=0), k_bf)
        qk = qk_kk[0:C]
        kk = qk_kk[C:2 * C]
        a = jnp.where(strict, b_col * kk * decay, 0.0)
        p = a
        t_inv = eye - a
        for _ in range(n_double):
            p = _dot(p, p, HIGHEST)
            t_inv = t_inv + _dot(t_inv, p, HIGHEST)
        rhs = jnp.concatenate([b_col * v, (b_col * eg_col) * k], axis=-1)
        sol = _dot(t_inv, rhs, HIGHEST)

        s_h = s_ref[0, h]
        s_bf = s_h.astype(BF16)
        u = sol[:, 0:HEAD_DIM] - _dot(sol[:, HEAD_DIM:].astype(BF16), s_bf)
        u_bf = u.astype(BF16)
        o = eg_col * _dot(q.astype(BF16), s_bf) + _dot((qk * decay).astype(BF16), u_bf)
        kd = (ekd[:, gl_h:gl_h + 1] * k).astype(BF16)
        s_ref[0, h] = egl[:, gl_h:gl_h + 1] * s_h + _dot_tn(kd, u_bf)

        z = proj_ref[0, :, A_QKV + h * HEAD_DIM:A_QKV + (h + 1) * HEAD_DIM]
        o_ref[0, :, hs] = (_rms(o, og_ref[...]) * (z * _sigmoid(z))).astype(o_ref.dtype)


def gdn_mixer(proj, conv0, s0, conv_w, gate_par, o_gain, *, C):
    b, t, _ = proj.shape
    return pl.pallas_call(
        functools.partial(_gdn_kernel, C=C),
        grid=(b, t // C),
        in_specs=[
            pl.BlockSpec((1, C, A_IN_PAD), lambda i, c: (i, c, 0)),
            pl.BlockSpec((1, CONV_W - 1, A_QKV), lambda i, c: (i, 0, 0)),
            pl.BlockSpec((1, HEADS, HEAD_DIM, HEAD_DIM), lambda i, c: (i, 0, 0, 0)),
            pl.BlockSpec((CONV_W, A_QKV), lambda i, c: (0, 0)),
            pl.BlockSpec((2, LANES), lambda i, c: (0, 0)),
            pl.BlockSpec((1, HEAD_DIM), lambda i, c: (0, 0)),
        ],
        out_specs=[
            pl.BlockSpec((1, C, D_MODEL), lambda i, c: (i, c, 0)),
            pl.BlockSpec((1, CONV_W - 1, A_QKV), lambda i, c: (i, 0, 0)),
            pl.BlockSpec((1, HEADS, HEAD_DIM, HEAD_DIM), lambda i, c: (i, 0, 0, 0)),
        ],
        out_shape=[
            jax.ShapeDtypeStruct((b, t, D_MODEL), BF16),
            jax.ShapeDtypeStruct((b, CONV_W - 1, A_QKV), F32),
            jax.ShapeDtypeStruct((b, HEADS, HEAD_DIM, HEAD_DIM), F32),
        ],
        scratch_shapes=[pltpu.VMEM((CONV_ROW0 + C, A_QKV), F32)],
        compiler_params=_params("parallel", "arbitrary"),
        name="gdn_mixer",
    )(proj, conv0, s0, conv_w, gate_par, o_gain.reshape(1, HEAD_DIM))


def _attn_kernel(*refs, tq, tk, pos0, n_new, lam_init):
    if n_new:
        slope_ref, lam_ref, sg_ref, q_ref, k_ref, v_ref, kn_ref, vn_ref, o_ref, m_ref, l_ref, acc_ref = refs
    else:
        slope_ref, lam_ref, sg_ref, q_ref, k_ref, v_ref, o_ref, m_ref, l_ref, acc_ref = refs
    i = pl.program_id(2)
    j = pl.program_id(3)
    nk = pl.num_programs(3)

    @pl.when(j == 0)
    def _():
        m_ref[...] = jnp.full_like(m_ref, NEG)
        l_ref[...] = jnp.zeros_like(l_ref)
        acc_ref[...] = jnp.zeros_like(acc_ref)

    slope = slope_ref[0, :, 0:1]
    q_pos = pos0 + i * tq + lax.broadcasted_iota(jnp.int32, (tq, 1), 0)
    lane = lax.broadcasted_iota(jnp.int32, (1, HEAD_DIM), 1)
    q = (q_ref[0] * (MAP_DIM ** -0.5)).astype(BF16)
    q_maps = [jnp.where(lane < MAP_DIM, q, 0), jnp.where(lane >= MAP_DIM, q, 0)]

    def update(kb, vb, k0, n):
        k_pos = k0 + lax.broadcasted_iota(jnp.int32, (1, n), 1)
        allowed = (k_pos >> CHUNK_SHIFT) <= (q_pos >> CHUNK_SHIFT)
        bias = -slope * jnp.abs(q_pos - k_pos).astype(F32)
        kb = kb.astype(BF16)
        vb = vb.astype(BF16)
        for mp in range(2):
            s = jnp.where(allowed, _dot_nt(q_maps[mp], kb) + bias, NEG)
            m_old = m_ref[mp]
            m_new = jnp.maximum(m_old, jnp.max(s, axis=-1, keepdims=True))
            alpha = jnp.exp(m_old - m_new)
            p = jnp.exp(s - m_new)
            l_ref[mp] = alpha * l_ref[mp] + jnp.sum(p, axis=-1, keepdims=True)
            acc_ref[mp] = alpha * acc_ref[mp] + _dot(p.astype(BF16), vb)
            m_ref[mp] = m_new

    q_last_chunk_end = (((pos0 + (i + 1) * tq - 1) >> CHUNK_SHIFT) << CHUNK_SHIFT) + CHUNK - 1

    @pl.when(j * tk <= q_last_chunk_end)
    def _():
        update(k_ref[0], v_ref[0], j * tk, tk)

    @pl.when(j == nk - 1)
    def _():
        if n_new:
            update(kn_ref[0], vn_ref[0], nk * tk, n_new)
        lp = lam_ref[...]
        lam = (jnp.exp(jnp.sum(lp[0:1] * lp[1:2], axis=-1, keepdims=True))
               - jnp.exp(jnp.sum(lp[2:3] * lp[3:4], axis=-1, keepdims=True)) + lam_init)
        o = acc_ref[0] / l_ref[0] - lam * (acc_ref[1] / l_ref[1])
        o_ref[0] = (_rms(o, sg_ref[...]) * (1.0 - lam_init)).astype(o_ref.dtype)


def diff_attention(q, k, v, k_new, v_new, slopes, lam_p, sub_gain, *, tq, tk, pos0, lam_init):
    b, t_q, _ = q.shape
    t_k = k.shape[1]
    n_new = 0 if k_new is None else k_new.shape[1]
    nk = t_k // tk

    def kv_index(bi, h, i, j):
        q_last_chunk_end = (((pos0 + (i + 1) * tq - 1) >> CHUNK_SHIFT) << CHUNK_SHIFT) + CHUNK - 1
        return (bi, jnp.minimum(j, q_last_chunk_end // tk), h)

    in_specs = [
        pl.BlockSpec((1, 1, LANES), lambda bi, h, i, j: (h, 0, 0)),
        pl.BlockSpec((4, MAP_DIM), lambda bi, h, i, j: (0, 0)),
        pl.BlockSpec((1, HEAD_DIM), lambda bi, h, i, j: (0, 0)),
        pl.BlockSpec((1, tq, HEAD_DIM), lambda bi, h, i, j: (bi, i, h)),
        pl.BlockSpec((1, tk, HEAD_DIM), kv_index),
        pl.BlockSpec((1, tk, HEAD_DIM), kv_index),
    ]
    args = [slopes, lam_p, sub_gain.reshape(1, HEAD_DIM), q, k, v]
    if n_new:
        in_specs += [pl.BlockSpec((1, n_new, HEAD_DIM), lambda bi, h, i, j: (bi, 0, h))] * 2
        args += [k_new, v_new]
    return pl.pallas_call(
        functools.partial(_attn_kernel, tq=tq, tk=tk, pos0=pos0, n_new=n_new, lam_init=lam_init),
        grid=(b, HEADS, t_q // tq, nk),
        in_specs=in_specs,
        out_specs=pl.BlockSpec((1, tq, HEAD_DIM), lambda bi, h, i, j: (bi, i, h)),
        out_shape=jax.ShapeDtypeStruct((b, t_q, D_MODEL), BF16),
        scratch_shapes=[
            pltpu.VMEM((2, tq, 1), F32),
            pltpu.VMEM((2, tq, 1), F32),
            pltpu.VMEM((2, tq, HEAD_DIM), F32),
        ],
        compiler_params=_params("parallel", "parallel", "parallel", "arbitrary"),
        name="diff_attention",
    )(*args)


def _trunk(x, conv_state, ssm_state, past_k, past_v, wts, *, tm, gdn_chunk, tq, tk):
    b, t, _ = x.shape
    m = b * t
    xf = x.reshape(m, D_MODEL)
    pos0 = 0 if past_k is None else past_k.shape[1]
    new_conv, new_ssm = [], []
    k_new = v_new = None
    for l in range(DEPTH):
        gains = wts["norm_gains"][l]
        if l < N_A:
            proj = norm_matmul(xf, gains[0], wts["a_w_in"][l], tm=tm, tn=A_IN_PAD // 3)
            og, conv_l, s_l = gdn_mixer(proj.reshape(b, t, A_IN_PAD), conv_state[l], ssm_state[l],
                                        wts["a_conv_w"][l], wts["gate_par"][l], wts["a_o_gain"][l], C=gdn_chunk)
            new_conv.append(conv_l)
            new_ssm.append(s_l)
            xf = matmul_norm_res(og.reshape(m, D_MODEL), wts["a_w_out"][l], gains[1], xf, tm=tm)
        else:
            j = l - N_A
            if j == 0:
                k_new = norm_matmul(xf, wts["kv_gain"], wts["w_k"], tm=tm, tn=D_MODEL).reshape(b, t, D_MODEL)
                v_new = norm_matmul(xf, wts["kv_gain"], wts["w_v"], tm=tm, tn=D_MODEL).reshape(b, t, D_MODEL)
            q = norm_matmul(xf, gains[0], wts["b_w_q"][j], tm=tm, tn=D_MODEL, out_dtype=BF16)
            lam_init = 0.8 - 0.6 * math.exp(-0.3 * l)
            common = dict(tq=tq, tk=tk, pos0=pos0, lam_init=lam_init)
            if past_k is None:
                o = diff_attention(q.reshape(b, t, D_MODEL), k_new, v_new, None, None, wts["slopes"],
                                   wts["b_lam"][j], wts["b_sub_gain"][j], **common)
            else:
                o = diff_attention(q.reshape(b, t, D_MODEL), past_k, past_v, k_new, v_new, wts["slopes"],
                                   wts["b_lam"][j], wts["b_sub_gain"][j], **common)
            xf = matmul_norm_res(o.reshape(m, D_MODEL), wts["b_w_out"][j], gains[1], xf, tm=tm)
        xf = mlp_block(xf, gains[2], wts["mlp_w1"][l], wts["mlp_w2"][l], gains[3], tm=tm, tf=512)
    kv_shape = (b, t, HEADS, HEAD_DIM)
    return (xf.reshape(b, t, D_MODEL), jnp.stack(new_conv), jnp.stack(new_ssm),
            k_new.reshape(kv_shape), v_new.reshape(kv_shape))


def kernel(x_prompt, x_sample, state_conv, state_ssm, cache_k, cache_v, norm_gains, a_w_in, a_conv_w, a_log,
           a_dt_bias, a_o_gain, a_w_out, kv_gain, w_kv, b_w_q, b_lam, b_sub_gain, b_w_out, mlp_w1, mlp_w2):
    a_in = a_w_in.shape[-1]
    gate_par = jnp.zeros((N_A, 2, LANES), F32)
    gate_par = gate_par.at[:, 0, HEADS:2 * HEADS].set(a_log.astype(F32))
    gate_par = gate_par.at[:, 1, HEADS:2 * HEADS].set(a_dt_bias.astype(F32))
    slopes = 2.0 ** (-8.0 * jnp.arange(1, HEADS + 1, dtype=F32) / HEADS)
    wts = dict(
        norm_gains=norm_gains.astype(F32),
        a_w_in=jnp.pad(a_w_in, ((0, 0), (0, 0), (0, A_IN_PAD - a_in))).astype(BF16),
        a_conv_w=a_conv_w.astype(F32),
        gate_par=gate_par,
        a_o_gain=a_o_gain.astype(F32),
        a_w_out=a_w_out.astype(BF16),
        kv_gain=kv_gain.astype(F32),
        w_k=w_kv[:, :D_MODEL].astype(BF16),
        w_v=w_kv[:, D_MODEL:].astype(BF16),
        b_w_q=b_w_q.astype(BF16),
        b_lam=b_lam.astype(F32),
        b_sub_gain=b_sub_gain.astype(F32),
        b_w_out=b_w_out.astype(BF16),
        mlp_w1=mlp_w1.astype(BF16),
        mlp_w2=mlp_w2.astype(BF16),
        slopes=jnp.broadcast_to(slopes[:, None, None], (HEADS, 1, LANES)),
    )
    bp, tp, _ = x_prompt.shape
    conv0 = jnp.zeros((N_A, bp, CONV_W - 1, A_QKV), F32)
    ssm0 = jnp.zeros((N_A, bp, HEADS, HEAD_DIM, HEAD_DIM), F32)
    y_p, p_conv, p_ssm, p_k, p_v = _trunk(x_prompt, conv0, ssm0, None, None, wts,
                                          tm=512, gdn_chunk=CHUNK, tq=512, tk=512)
    bs, ts, _ = x_sample.shape
    past = cache_k.shape[1]
    y_s, s_conv, s_ssm, s_k, s_v = _trunk(x_sample, state_conv, state_ssm,
                                          cache_k.reshape(bs, past, D_MODEL), cache_v.reshape(bs, past, D_MODEL),
                                          wts, tm=bs * ts, gdn_chunk=ts, tq=ts, tk=512)
    return (y_p, y_s, p_conv, p_ssm, p_k, p_v, s_conv, s_ssm, s_k, s_v)
```

```python
import functools
import math

import jax
import jax.numpy as jnp
import numpy as np
from jax import lax
from jax.experimental import pallas as pl
from jax.experimental.pallas import tpu as pltpu

F32 = jnp.float32
BF16 = jnp.bfloat16

D_MODEL = 1024
DEPTH = 4
N_A = DEPTH // 2
CHUNK = 64
CHUNK_SHIFT = CHUNK.bit_length() - 1
assert 1 << CHUNK_SHIFT == CHUNK
HEADS = 8
HEAD_DIM = D_MODEL // HEADS
CONV_W = 4
A_QKV = 3 * D_MODEL
A_GATE_COL = A_QKV + D_MODEL
LANES = 128
A_IN_PAD = A_GATE_COL + LANES
MAP_DIM = HEAD_DIM // 2
D_FF = 4 * D_MODEL
EPS = 1e-6
NEG = -1e30
VMEM_LIMIT = 48 * 1024 * 1024
HIGHEST = lax.Precision.HIGHEST
BF16_EXACT_INT = 256


def _dot(a, b, precision=None):
    return jnp.dot(a, b, preferred_element_type=F32, precision=precision)


def _dot_nt(a, b):
    return lax.dot_general(a, b, (((1,), (1,)), ((), ())), preferred_element_type=F32)


def _dot_tn(a, b):
    return lax.dot_general(a, b, (((0,), (0,)), ((), ())), preferred_element_type=F32)


def _split(a):
    hi = a.astype(BF16)
    return hi, (a - hi.astype(F32)).astype(BF16)


def _dot_split(a, b):
    return _dot(a[0], b[0]) + (_dot(a[0], b[1]) + _dot(a[1], b[0]))


def _rms(x, gain):
    return x * lax.rsqrt(jnp.mean(x * x, axis=-1, keepdims=True) + EPS) * gain


def _sigmoid(x):
    return 1.0 / (1.0 + jnp.exp(-x))


def _params(*semantics):
    return pltpu.CompilerParams(dimension_semantics=semantics, vmem_limit_bytes=VMEM_LIMIT)


def _norm_matmul_kernel(x_ref, g_ref, w_ref, *out_and_scratch):
    *o_refs, hn_ref = out_and_scratch

    @pl.when(pl.program_id(1) == 0)
    def _():
        hn_ref[...] = _rms(x_ref[...], g_ref[...]).astype(BF16)

    y = _dot(hn_ref[...], w_ref[...])
    for o_ref in o_refs:
        o_ref[...] = y.astype(o_ref.dtype)


def norm_matmul(x, gain, w, *, tm, tn, out_dtypes=(F32,)):
    m, k = x.shape
    n = w.shape[1]
    outs = pl.pallas_call(
        _norm_matmul_kernel,
        grid=(m // tm, n // tn),
        in_specs=[
            pl.BlockSpec((tm, k), lambda i, j: (i, 0)),
            pl.BlockSpec((1, k), lambda i, j: (0, 0)),
            pl.BlockSpec((k, tn), lambda i, j: (0, j)),
        ],
        out_specs=[pl.BlockSpec((tm, tn), lambda i, j: (i, j)) for _ in out_dtypes],
        out_shape=[jax.ShapeDtypeStruct((m, n), dt) for dt in out_dtypes],
        scratch_shapes=[pltpu.VMEM((tm, k), BF16)],
        compiler_params=_params("parallel", "arbitrary"),
        name="norm_matmul",
    )(x, gain.reshape(1, k), w)
    return outs[0] if len(out_dtypes) == 1 else outs


def _matmul_norm_res_kernel(a_ref, w_ref, g_ref, x_ref, o_ref):
    y = _dot(a_ref[...], w_ref[...])
    o_ref[...] = x_ref[...] + _rms(y, g_ref[...])


def matmul_norm_res(a, w, gain, x, *, tm):
    m, k = a.shape
    d = w.shape[1]
    return pl.pallas_call(
        _matmul_norm_res_kernel,
        grid=(m // tm,),
        in_specs=[
            pl.BlockSpec((tm, k), lambda i: (i, 0)),
            pl.BlockSpec((k, d), lambda i: (0, 0)),
            pl.BlockSpec((1, d), lambda i: (0, 0)),
            pl.BlockSpec((tm, d), lambda i: (i, 0)),
        ],
        out_specs=pl.BlockSpec((tm, d), lambda i: (i, 0)),
        out_shape=jax.ShapeDtypeStruct((m, d), F32),
        compiler_params=_params("parallel"),
        name="matmul_norm_res",
    )(a, w, gain.reshape(1, d), x)


def _mlp_kernel(x_ref, g_in_ref, w1_ref, w2_ref, g_out_ref, o_ref, hn_ref, acc_ref):
    f = pl.program_id(1)

    @pl.when(f == 0)
    def _():
        hn_ref[...] = _rms(x_ref[...], g_in_ref[...]).astype(BF16)
        acc_ref[...] = jnp.zeros_like(acc_ref)

    h = _dot(hn_ref[...], w1_ref[...])
    h = jnp.square(jnp.maximum(h, 0.0)).astype(BF16)
    acc_ref[...] += _dot(h, w2_ref[...])

    @pl.when(f == pl.num_programs(1) - 1)
    def _():
        o_ref[...] = x_ref[...] + _rms(acc_ref[...], g_out_ref[...])


def mlp_block(x, g_in, w1, w2, g_out, *, tm, tf):
    m, d = x.shape
    ff = w1.shape[1]
    return pl.pallas_call(
        _mlp_kernel,
        grid=(m // tm, ff // tf),
        in_specs=[
            pl.BlockSpec((tm, d), lambda i, f: (i, 0)),
            pl.BlockSpec((1, d), lambda i, f: (0, 0)),
            pl.BlockSpec((d, tf), lambda i, f: (0, f)),
            pl.BlockSpec((tf, d), lambda i, f: (f, 0)),
            pl.BlockSpec((1, d), lambda i, f: (0, 0)),
        ],
        out_specs=pl.BlockSpec((tm, d), lambda i, f: (i, 0)),
        out_shape=jax.ShapeDtypeStruct((m, d), F32),
        scratch_shapes=[pltpu.VMEM((tm, d), BF16), pltpu.VMEM((tm, d), F32)],
        compiler_params=_params("parallel", "arbitrary"),
        name="mlp_block",
    )(x, g_in.reshape(1, d), w1, w2, g_out.reshape(1, d))


CONV_ROW0 = 8


def _gdn_prep_kernel(proj_ref, conv0_ref, cw_ref, gp_ref,
                     w_ref, uu_ref, qg_ref, kd_ref, qkd_ref, egl_ref, conv_ref, xbuf_ref, *, C, G):
    R = C * G
    prev0 = CONV_ROW0 - (CONV_W - 1)

    @pl.when(pl.program_id(1) == 0)
    def _():
        xbuf_ref[prev0:CONV_ROW0, :] = conv0_ref[0]

    xbuf_ref[CONV_ROW0:CONV_ROW0 + R, :] = proj_ref[0, :, 0:A_QKV]
    cw = cw_ref[...]
    y = xbuf_ref[prev0:prev0 + R, :] * cw[0:1, :]
    for j in range(1, CONV_W):
        y = y + xbuf_ref[prev0 + j:prev0 + j + R, :] * cw[j:j + 1, :]
    y = y * _sigmoid(y)
    last = xbuf_ref[prev0 + R:CONV_ROW0 + R, :]
    xbuf_ref[prev0:CONV_ROW0, :] = last
    conv_ref[0] = last

    tail = proj_ref[0, :, A_GATE_COL:A_IN_PAD]
    gp = gp_ref[...]
    beta = _sigmoid(tail)
    t = tail + gp[1:2, :]
    softplus = jnp.maximum(t, 0.0) + jnp.log(1.0 + jnp.exp(-jnp.abs(t)))
    g = -jnp.exp(gp[0:1, :]) * softplus

    rr = lax.broadcasted_iota(jnp.int32, (R, R), 0)
    cc = lax.broadcasted_iota(jnp.int32, (R, R), 1)
    shift = C.bit_length() - 1
    same = (rr >> shift) == (cc >> shift)
    gc = _dot(jnp.where(same & (rr >= cc), 1.0, 0.0), g, HIGHEST)
    gl = _dot(jnp.where(same, 1.0, 0.0), g, HIGHEST)
    pad_rows = -R % LANES
    gc_t = (jnp.concatenate([gc, jnp.zeros((pad_rows, LANES), F32)], axis=0) if pad_rows else gc).T
    eg = jnp.exp(gc)
    ekd = jnp.exp(gl - gc)
    egl = jnp.exp(gl)
    for gi in range(G):
        egl_ref[0, gi] = egl[gi * C:gi * C + 1, :]

    row = lax.broadcasted_iota(jnp.int32, (C, C), 0)
    col = lax.broadcasted_iota(jnp.int32, (C, C), 1)
    incl = row >= col
    strict = row > col
    eye = (row == col).astype(F32)

    units = [(gi, h) for gi in range(G) for h in range(HEADS)]
    qn, kn = [], []
    for h in range(HEADS):
        q = y[:, h * HEAD_DIM:(h + 1) * HEAD_DIM]
        k = y[:, D_MODEL + h * HEAD_DIM:D_MODEL + (h + 1) * HEAD_DIM]
        qn.append(q * lax.rsqrt(jnp.sum(q * q, axis=-1, keepdims=True) + EPS) * (HEAD_DIM ** -0.5))
        kn.append(k * lax.rsqrt(jnp.sum(k * k, axis=-1, keepdims=True) + EPS))

    def rows(gi):
        return slice(gi * C, (gi + 1) * C)

    decay, qk, a_mat = {}, {}, {}
    for gi, h in units:
        rs, gl_h = rows(gi), HEADS + h
        g_col = gc[rs, gl_h:gl_h + 1]
        g_row = gc_t[gl_h:gl_h + 1, gi * C:(gi + 1) * C]
        decay[gi, h] = jnp.where(incl, jnp.exp(jnp.minimum(g_col - g_row, 0.0)), 0.0)
        k_bf = kn[h][rs].astype(BF16)
        qk_kk = _dot_nt(jnp.concatenate([qn[h][rs].astype(BF16), k_bf], axis=0), k_bf)
        qk[gi, h] = qk_kk[0:C]
        a_mat[gi, h] = jnp.where(strict, beta[rs, h:h + 1] * qk_kk[C:2 * C] * decay[gi, h], 0.0)
    t_inv = {u: eye - a_mat[u] for u in units}
    pw = {}
    for u in units:
        a_s = _split(a_mat[u])
        pw[u] = _dot_split(a_s, a_s)
    n_double = C.bit_length() - 2
    for step in range(n_double):
        for u in units:
            p_s = _split(pw[u])
            if step + 1 < n_double:
                tp = _dot_split(_split(jnp.concatenate([t_inv[u], pw[u]], axis=0)), p_s)
                t_inv[u] = t_inv[u] + tp[0:C]
                pw[u] = tp[C:2 * C]
            else:
                t_inv[u] = t_inv[u] + _dot_split(_split(t_inv[u]), p_s)
    for gi, h in units:
        rs, gl_h = rows(gi), HEADS + h
        hs = slice(h * HEAD_DIM, (h + 1) * HEAD_DIM)
        b_col = beta[rs, h:h + 1]
        eg_col = eg[rs, gl_h:gl_h + 1]
        k = kn[h][rs]
        v = y[rs, 2 * D_MODEL + h * HEAD_DIM:2 * D_MODEL + (h + 1) * HEAD_DIM]
        rhs = jnp.concatenate([b_col * v, (b_col * eg_col) * k], axis=-1).astype(BF16)
        sol = _dot(t_inv[gi, h].astype(BF16), rhs)
        uu_ref[0, rs, hs] = sol[:, 0:HEAD_DIM].astype(uu_ref.dtype)
        w_ref[0, rs, hs] = sol[:, HEAD_DIM:].astype(w_ref.dtype)
        qg_ref[0, rs, hs] = (eg_col * qn[h][rs]).astype(qg_ref.dtype)
        kd_ref[0, rs, hs] = (ekd[rs, gl_h:gl_h + 1] * k).astype(kd_ref.dtype)
        qkd_ref[0, rs, h * C:(h + 1) * C] = (qk[gi, h] * decay[gi, h]).astype(qkd_ref.dtype)


def gdn_prep(proj, conv0, conv_w, gate_par, *, C, G):
    b, t, _ = proj.shape
    r = C * G
    nc = t // C
    row_spec = pl.BlockSpec((1, r, D_MODEL), lambda i, c: (i, c, 0))
    conv_spec = pl.BlockSpec((1, CONV_W - 1, A_QKV), lambda i, c: (i, 0, 0))
    return pl.pallas_call(
        functools.partial(_gdn_prep_kernel, C=C, G=G),
        grid=(b, t // r),
        in_specs=[
            pl.BlockSpec((1, r, A_IN_PAD), lambda i, c: (i, c, 0)),
            conv_spec,
            pl.BlockSpec((CONV_W, A_QKV), lambda i, c: (0, 0)),
            pl.BlockSpec((2, LANES), lambda i, c: (0, 0)),
        ],
        out_specs=[
            row_spec, row_spec, row_spec, row_spec,
            pl.BlockSpec((1, r, HEADS * C), lambda i, c: (i, c, 0)),
            pl.BlockSpec((1, G, 1, LANES), lambda i, c: (i, c, 0, 0)),
            conv_spec,
        ],
        out_shape=[jax.ShapeDtypeStruct((b, t, D_MODEL), BF16)] * 4 + [
            jax.ShapeDtypeStruct((b, t, HEADS * C), BF16),
            jax.ShapeDtypeStruct((b, nc, 1, LANES), F32),
            jax.ShapeDtypeStruct((b, CONV_W - 1, A_QKV), F32),
        ],
        scratch_shapes=[pltpu.VMEM((CONV_ROW0 + r, A_QKV), F32)],
        compiler_params=_params("arbitrary", "arbitrary"),
        name="gdn_prep",
    )(proj, conv0, conv_w, gate_par)


def _gdn_scan_kernel(w_ref, uu_ref, qg_ref, kd_ref, qkd_ref, egl_ref, z_ref, s0_ref, og_ref, o_ref, s_ref, *, C, G):
    @pl.when(pl.program_id(1) == 0)
    def _():
        s_ref[0] = s0_ref[0]

    heads = range(HEADS)
    for gi in range(G):
        rs = slice(gi * C, (gi + 1) * C)
        egl = egl_ref[0, gi]

        def hs(h):
            return slice(h * HEAD_DIM, (h + 1) * HEAD_DIM)

        s_old = [s_ref[0, h] for h in heads]
        s_bf = [s.astype(BF16) for s in s_old]
        wq = [_dot(jnp.concatenate([w_ref[0, rs, hs(h)], qg_ref[0, rs, hs(h)]], axis=0), s_bf[h]) for h in heads]
        u_bf = [(uu_ref[0, rs, hs(h)].astype(F32) - wq[h][0:C]).astype(BF16) for h in heads]
        o = [wq[h][C:2 * C] + _dot(qkd_ref[0, rs, h * C:(h + 1) * C], u_bf[h]) for h in heads]
        for h in heads:
            s_ref[0, h] = egl[:, HEADS + h:HEADS + h + 1] * s_old[h] + _dot_tn(kd_ref[0, rs, hs(h)], u_bf[h])
        for h in heads:
            z = z_ref[0, rs, hs(h)]
            o_ref[0, rs, hs(h)] = (_rms(o[h], og_ref[...]) * (z * _sigmoid(z))).astype(o_ref.dtype)


def gdn_scan(w, uu, qg, kd, qkd, egl, proj, s0, o_gain, *, C, G):
    b, t, _ = w.shape
    r = C * G
    row_spec = pl.BlockSpec((1, r, D_MODEL), lambda i, c: (i, c, 0))
    state_spec = pl.BlockSpec((1, HEADS, HEAD_DIM, HEAD_DIM), lambda i, c: (i, 0, 0, 0))
    return pl.pallas_call(
        functools.partial(_gdn_scan_kernel, C=C, G=G),
        grid=(b, t // r),
        in_specs=[
            row_spec, row_spec, row_spec, row_spec,
            pl.BlockSpec((1, r, HEADS * C), lambda i, c: (i, c, 0)),
            pl.BlockSpec((1, G, 1, LANES), lambda i, c: (i, c, 0, 0)),
            pl.BlockSpec((1, r, D_MODEL), lambda i, c: (i, c, A_QKV // D_MODEL)),
            state_spec,
            pl.BlockSpec((1, HEAD_DIM), lambda i, c: (0, 0)),
        ],
        out_specs=[row_spec, state_spec],
        out_shape=[
            jax.ShapeDtypeStruct((b, t, D_MODEL), BF16),
            jax.ShapeDtypeStruct((b, HEADS, HEAD_DIM, HEAD_DIM), F32),
        ],
        compiler_params=_params("arbitrary", "arbitrary"),
        name="gdn_scan",
    )(w, uu, qg, kd, qkd, egl, proj, s0, o_gain.reshape(1, HEAD_DIM))


def _lambda(lam_ref, lam_init):
    lp = lam_ref[...]
    return (jnp.exp(jnp.sum(lp[0:1] * lp[1:2], axis=-1, keepdims=True))
            - jnp.exp(jnp.sum(lp[2:3] * lp[3:4], axis=-1, keepdims=True)) + lam_init)


def _attn_prompt_kernel(qi_ref, kj_ref, slope_ref, lam_ref, sg_ref, q_ref, k_ref, v_ref, o_ref, m_ref, acc_ref,
                        *, t, lam_init):
    pair = pl.program_id(1)
    qi = qi_ref[pair]
    kj = kj_ref[pair]

    @pl.when(kj == 0)
    def _():
        m_ref[...] = jnp.full_like(m_ref, NEG)
        acc_ref[...] = jnp.zeros_like(acc_ref)

    slope = slope_ref[0, :, 0:1]
    lane = lax.broadcasted_iota(jnp.int32, (1, HEAD_DIM), 1)
    map_lanes = [lane < MAP_DIM, lane >= MAP_DIM]
    q = q_ref[...] * (MAP_DIM ** -0.5)
    k = k_ref[...]
    v_aug = jnp.concatenate([v_ref[...], jnp.broadcast_to(jnp.where(lane == 0, 1.0, 0.0).astype(BF16),
                                                          (t, HEAD_DIM))], axis=1)

    def accumulate(mp, s, shift_const):
        m_old = m_ref[mp]
        m_new = jnp.maximum(m_old, jnp.max(s, axis=-1, keepdims=True) + shift_const)
        p = jnp.exp(s - (m_new - shift_const))
        acc_ref[mp] = jnp.exp(m_old - m_new) * acc_ref[mp] + _dot(p.astype(BF16), v_aug)
        m_ref[mp] = m_new

    @pl.when(kj < qi)
    def _():
        j = lax.broadcasted_iota(jnp.int32, (t, 1), 0)
        j_hi = slope * (2 * (j >> 1)).astype(F32)
        j_lo = slope * (j & 1).astype(F32)
        tile_shift = slope * ((kj - qi) * t).astype(F32)
        for mp in range(2):
            a0 = MAP_DIM if mp == 0 else 0
            extra = (lane == a0) | (lane == a0 + 1)
            qa = jnp.where(map_lanes[mp], q, jnp.where(extra, 1.0, 0.0).astype(BF16))
            k_extra = jnp.where(lane == a0, j_hi, jnp.where(lane == a0 + 1, j_lo, 0.0)).astype(BF16)
            ka = jnp.where(map_lanes[mp], k, k_extra)
            accumulate(mp, _dot_nt(qa, ka), tile_shift)

    @pl.when(kj == qi)
    def _():
        i = lax.broadcasted_iota(jnp.int32, (t, 1), 0)
        j = lax.broadcasted_iota(jnp.int32, (1, t), 1)
        allowed = (j >> CHUNK_SHIFT) <= (i >> CHUNK_SHIFT)
        bias = slope * (i - jnp.abs(i - j)).astype(F32)
        for mp in range(2):
            qm = jnp.where(map_lanes[mp], q, jnp.zeros_like(q))
            accumulate(mp, jnp.where(allowed, _dot_nt(qm, k) + bias, NEG), 0.0)
        lam = _lambda(lam_ref, lam_init)
        a0 = acc_ref[0]
        a1 = acc_ref[1]
        o = (a0[:, 0:HEAD_DIM] / a0[:, HEAD_DIM:HEAD_DIM + 1]
             - lam * (a1[:, 0:HEAD_DIM] / a1[:, HEAD_DIM:HEAD_DIM + 1]))
        o_ref[...] = (_rms(o, sg_ref[...]) * (1.0 - lam_init)).astype(o_ref.dtype)


def diff_attention_prompt(q, k, v, slopes, lam_p, sub_gain, *, t, lam_init):
    t_all = q.shape[0]
    n = t_all // t
    assert t % CHUNK == 0 and t <= 2 * BF16_EXACT_INT
    qi = np.concatenate([np.full(i + 1, i, np.int32) for i in range(n)])
    kj = np.concatenate([np.arange(i + 1, dtype=np.int32) for i in range(n)])
    grid_spec = pltpu.PrefetchScalarGridSpec(
        num_scalar_prefetch=2,
        grid=(HEADS, len(qi)),
        in_specs=[
            pl.BlockSpec((1, 1, LANES), lambda h, p, qi_r, kj_r: (h, 0, 0)),
            pl.BlockSpec((4, MAP_DIM), lambda h, p, qi_r, kj_r: (0, 0)),
            pl.BlockSpec((1, HEAD_DIM), lambda h, p, qi_r, kj_r: (0, 0)),
            pl.BlockSpec((t, HEAD_DIM), lambda h, p, qi_r, kj_r: (qi_r[p], h)),
            pl.BlockSpec((t, HEAD_DIM), lambda h, p, qi_r, kj_r: (kj_r[p], h)),
            pl.BlockSpec((t, HEAD_DIM), lambda h, p, qi_r, kj_r: (kj_r[p], h)),
        ],
        out_specs=pl.BlockSpec((t, HEAD_DIM), lambda h, p, qi_r, kj_r: (qi_r[p], h)),
        scratch_shapes=[pltpu.VMEM((2, t, 1), F32), pltpu.VMEM((2, t, 2 * HEAD_DIM), F32)],
    )
    return pl.pallas_call(
        functools.partial(_attn_prompt_kernel, t=t, lam_init=lam_init),
        grid_spec=grid_spec,
        out_shape=jax.ShapeDtypeStruct((t_all, D_MODEL), BF16),
        compiler_params=_params("arbitrary", "arbitrary"),
        name="diff_attention_prompt",
    )(jnp.asarray(qi), jnp.asarray(kj), slopes, lam_p, sub_gain.reshape(1, HEAD_DIM), q, k, v)


def _attn_decode_kernel(slope_ref, lam_ref, sg_ref, q_ref, k_ref, v_ref, kn_ref, vn_ref, o_ref, m_ref, l_ref, acc_ref,
                        *, tk, pos0, lam_init):
    j = pl.program_id(2)
    nk = pl.num_programs(2)
    tq = q_ref.shape[1]

    @pl.when(j == 0)
    def _():
        m_ref[...] = jnp.full_like(m_ref, NEG)
        l_ref[...] = jnp.zeros_like(l_ref)
        acc_ref[...] = jnp.zeros_like(acc_ref)

    slope = slope_ref[0, :, 0:1]
    q_pos = pos0 + lax.broadcasted_iota(jnp.int32, (tq, 1), 0)
    lane = lax.broadcasted_iota(jnp.int32, (1, HEAD_DIM), 1)
    q = q_ref[0] * (MAP_DIM ** -0.5)
    q_maps = [jnp.where(lane < MAP_DIM, q, jnp.zeros_like(q)), jnp.where(lane >= MAP_DIM, q, jnp.zeros_like(q))]

    def update(kb, vb, k0, n):
        k_pos = k0 + lax.broadcasted_iota(jnp.int32, (1, n), 1)
        allowed = (k_pos >> CHUNK_SHIFT) <= (q_pos >> CHUNK_SHIFT)
        bias = -slope * jnp.abs(q_pos - k_pos).astype(F32)
        kb = kb.astype(BF16)
        vb = vb.astype(BF16)
        for mp in range(2):
            s = jnp.where(allowed, _dot_nt(q_maps[mp], kb) + bias, NEG)
            m_old = m_ref[mp]
            m_new = jnp.maximum(m_old, jnp.max(s, axis=-1, keepdims=True))
            alpha = jnp.exp(m_old - m_new)
            p = jnp.exp(s - m_new)
            l_ref[mp] = alpha * l_ref[mp] + jnp.sum(p, axis=-1, keepdims=True)
            acc_ref[mp] = alpha * acc_ref[mp] + _dot(p.astype(BF16), vb)
            m_ref[mp] = m_new

    update(k_ref[0], v_ref[0], j * tk, tk)

    @pl.when(j == nk - 1)
    def _():
        update(kn_ref[0], vn_ref[0], pos0, tq)
        lam = _lambda(lam_ref, lam_init)
        o = acc_ref[0] / l_ref[0] - lam * (acc_ref[1] / l_ref[1])
        o_ref[0] = (_rms(o, sg_ref[...]) * (1.0 - lam_init)).astype(o_ref.dtype)


def diff_attention_decode(q, k, v, k_new, v_new, slopes, lam_p, sub_gain, *, tk, lam_init):
    b, n, _ = q.shape
    t_k = k.shape[1]
    assert t_k % tk == 0 and t_k % CHUNK == 0
    head_block = lambda rows: pl.BlockSpec((1, rows, HEAD_DIM), lambda bi, h, j: (bi, 0, h))
    kv_spec = pl.BlockSpec((1, tk, HEAD_DIM), lambda bi, h, j: (bi, j, h))
    return pl.pallas_call(
        functools.partial(_attn_decode_kernel, tk=tk, pos0=t_k, lam_init=lam_init),
        grid=(b, HEADS, t_k // tk),
        in_specs=[
            pl.BlockSpec((1, 1, LANES), lambda bi, h, j: (h, 0, 0)),
            pl.BlockSpec((4, MAP_DIM), lambda bi, h, j: (0, 0)),
            pl.BlockSpec((1, HEAD_DIM), lambda bi, h, j: (0, 0)),
            head_block(n), kv_spec, kv_spec, head_block(n), head_block(n),
        ],
        out_specs=head_block(n),
        out_shape=jax.ShapeDtypeStruct((b, n, D_MODEL), BF16),
        scratch_shapes=[
            pltpu.VMEM((2, n, 1), F32),
            pltpu.VMEM((2, n, 1), F32),
            pltpu.VMEM((2, n, HEAD_DIM), F32),
        ],
        compiler_params=_params("arbitrary", "arbitrary", "arbitrary"),
        name="diff_attention_decode",
    )(slopes, lam_p, sub_gain.reshape(1, HEAD_DIM), q, k, v, k_new, v_new)


def _trunk(x, conv_state, ssm_state, past_k, past_v, wts, *, tm, gdn_chunk, gdn_group, attn_tile):
    b, t, _ = x.shape
    m = b * t
    xf = x.reshape(m, D_MODEL)
    new_conv, new_ssm = [], []
    k_new = v_new = k_bf = v_bf = None
    for l in range(DEPTH):
        gains = wts["norm_gains"][l]
        if l < N_A:
            proj = norm_matmul(xf, gains[0], wts["a_w_in"][l], tm=tm, tn=A_IN_PAD // 3).reshape(b, t, A_IN_PAD)
            w, uu, qg, kd, qkd, egl, conv_l = gdn_prep(proj, conv_state[l], wts["a_conv_w"][l], wts["gate_par"][l],
                                                       C=gdn_chunk, G=gdn_group)
            og, s_l = gdn_scan(w, uu, qg, kd, qkd, egl, proj, ssm_state[l], wts["a_o_gain"][l],
                               C=gdn_chunk, G=gdn_group)
            new_conv.append(conv_l)
            new_ssm.append(s_l)
            xf = matmul_norm_res(og.reshape(m, D_MODEL), wts["a_w_out"][l], gains[1], xf, tm=tm)
        else:
            j = l - N_A
            if j == 0:
                k_new, k_bf = norm_matmul(xf, wts["kv_gain"], wts["w_k"], tm=tm, tn=D_MODEL, out_dtypes=(F32, BF16))
                v_new, v_bf = norm_matmul(xf, wts["kv_gain"], wts["w_v"], tm=tm, tn=D_MODEL, out_dtypes=(F32, BF16))
            q = norm_matmul(xf, gains[0], wts["b_w_q"][j], tm=tm, tn=D_MODEL, out_dtypes=(BF16,))
            lam_init = 0.8 - 0.6 * math.exp(-0.3 * l)
            tail = (wts["slopes"], wts["b_lam"][j], wts["b_sub_gain"][j])
            if past_k is None:
                assert b == 1
                o = diff_attention_prompt(q, k_bf, v_bf, *tail, t=attn_tile, lam_init=lam_init)
            else:
                o = diff_attention_decode(q.reshape(b, t, D_MODEL), past_k, past_v, k_new.reshape(b, t, D_MODEL),
                                          v_new.reshape(b, t, D_MODEL), *tail, tk=attn_tile, lam_init=lam_init)
            xf = matmul_norm_res(o.reshape(m, D_MODEL), wts["b_w_out"][j], gains[1], xf, tm=tm)
        xf = mlp_block(xf, gains[2], wts["mlp_w1"][l], wts["mlp_w2"][l], gains[3], tm=tm, tf=512)
    kv_shape = (b, t, HEADS, HEAD_DIM)
    return (xf.reshape(b, t, D_MODEL), jnp.stack(new_conv), jnp.stack(new_ssm),
            k_new.reshape(kv_shape), v_new.reshape(kv_shape))


def kernel(x_prompt, x_sample, state_conv, state_ssm, cache_k, cache_v, norm_gains, a_w_in, a_conv_w, a_log,
           a_dt_bias, a_o_gain, a_w_out, kv_gain, w_kv, b_w_q, b_lam, b_sub_gain, b_w_out, mlp_w1, mlp_w2):
    a_in = a_w_in.shape[-1]
    gate_par = jnp.zeros((N_A, 2, LANES), F32)
    gate_par = gate_par.at[:, 0, HEADS:2 * HEADS].set(a_log.astype(F32))
    gate_par = gate_par.at[:, 1, HEADS:2 * HEADS].set(a_dt_bias.astype(F32))
    slopes = 2.0 ** (-8.0 * jnp.arange(1, HEADS + 1, dtype=F32) / HEADS)
    wts = dict(
        norm_gains=norm_gains.astype(F32),
        a_w_in=jnp.pad(a_w_in, ((0, 0), (0, 0), (0, A_IN_PAD - a_in))).astype(BF16),
        a_conv_w=a_conv_w.astype(F32),
        gate_par=gate_par,
        a_o_gain=a_o_gain.astype(F32),
        a_w_out=a_w_out.astype(BF16),
        kv_gain=kv_gain.astype(F32),
        w_k=w_kv[:, :D_MODEL].astype(BF16),
        w_v=w_kv[:, D_MODEL:].astype(BF16),
        b_w_q=b_w_q.astype(BF16),
        b_lam=b_lam.astype(F32),
        b_sub_gain=b_sub_gain.astype(F32),
        b_w_out=b_w_out.astype(BF16),
        mlp_w1=mlp_w1.astype(BF16),
        mlp_w2=mlp_w2.astype(BF16),
        slopes=jnp.broadcast_to(slopes[:, None, None], (HEADS, 1, LANES)),
    )
    bp, tp, _ = x_prompt.shape
    conv0 = jnp.zeros((N_A, bp, CONV_W - 1, A_QKV), F32)
    ssm0 = jnp.zeros((N_A, bp, HEADS, HEAD_DIM, HEAD_DIM), F32)
    y_p, p_conv, p_ssm, p_k, p_v = _trunk(x_prompt, conv0, ssm0, None, None, wts,
                                          tm=512, gdn_chunk=CHUNK, gdn_group=4, attn_tile=512)
    bs, ts, _ = x_sample.shape
    past = cache_k.shape[1]
    y_s, s_conv, s_ssm, s_k, s_v = _trunk(x_sample, state_conv, state_ssm,
                                          cache_k.reshape(bs, past, D_MODEL), cache_v.reshape(bs, past, D_MODEL),
                                          wts, tm=bs * ts, gdn_chunk=ts, gdn_group=1, attn_tile=512)
    return (y_p, y_s, p_conv, p_ssm, p_k, p_v, s_conv, s_ssm, s_k, s_v)
```

```python
import functools
import math

import jax
import jax.numpy as jnp
import numpy as np
from jax import lax
from jax.experimental import pallas as pl
from jax.experimental.pallas import tpu as pltpu

F32 = jnp.float32
BF16 = jnp.bfloat16

D_MODEL = 1024
DEPTH = 4
N_A = DEPTH // 2
CHUNK = 64
CHUNK_SHIFT = CHUNK.bit_length() - 1
assert 1 << CHUNK_SHIFT == CHUNK
HEADS = 8
HEAD_DIM = D_MODEL // HEADS
CONV_W = 4
A_QKV = 3 * D_MODEL
A_GATE_COL = A_QKV + D_MODEL
LANES = 128
A_IN_PAD = A_GATE_COL + LANES
MAP_DIM = HEAD_DIM // 2
D_FF = 4 * D_MODEL
EPS = 1e-6
NEG = -1e30
VMEM_LIMIT = 48 * 1024 * 1024
HIGHEST = lax.Precision.HIGHEST
ONES_ROWS = 16
HEAD_ROWS_T = HEAD_DIM + ONES_ROWS
LOG2E = math.log2(math.e)
Q_SCALE = MAP_DIM ** -0.5 * LOG2E


def _dot(a, b, precision=None):
    return jnp.dot(a, b, preferred_element_type=F32, precision=precision)


def _dot_nt(a, b):
    return lax.dot_general(a, b, (((1,), (1,)), ((), ())), preferred_element_type=F32)


def _dot_tn(a, b):
    return lax.dot_general(a, b, (((0,), (0,)), ((), ())), preferred_element_type=F32)


def _split(a):
    hi = a.astype(BF16)
    return hi, (a - hi.astype(F32)).astype(BF16)


def _dot_split(a, b):
    return _dot(a[0], b[0]) + (_dot(a[0], b[1]) + _dot(a[1], b[0]))


def _rms(x, gain):
    return x * lax.rsqrt(jnp.mean(x * x, axis=-1, keepdims=True) + EPS) * gain


def _sigmoid(x):
    return 1.0 / (1.0 + jnp.exp(-x))


def _params(*semantics):
    return pltpu.CompilerParams(dimension_semantics=semantics, vmem_limit_bytes=VMEM_LIMIT)


def _norm_matmul_kernel(x_ref, g_ref, w_ref, *out_and_scratch, out_scale, head_transposed_copy):
    *o_refs, hn_ref = out_and_scratch

    @pl.when(pl.program_id(1) == 0)
    def _():
        hn_ref[...] = _rms(x_ref[...], g_ref[...]).astype(BF16)

    y = _dot(hn_ref[...], w_ref[...])
    if out_scale != 1.0:
        y = y * out_scale
    if head_transposed_copy:
        *o_refs, ot_ref = o_refs
        y_t = y.T.astype(ot_ref.dtype)
        for h in range(y_t.shape[0] // HEAD_DIM):
            r0 = h * HEAD_ROWS_T
            ot_ref[r0:r0 + HEAD_DIM, :] = y_t[h * HEAD_DIM:(h + 1) * HEAD_DIM]
            ot_ref[r0 + HEAD_DIM:r0 + HEAD_ROWS_T, :] = jnp.ones((ONES_ROWS, y_t.shape[1]), ot_ref.dtype)
    for o_ref in o_refs:
        o_ref[...] = y.astype(o_ref.dtype)


def norm_matmul(x, gain, w, *, tm, tn, out_dtypes=(F32,), out_scale=1.0, head_transposed_copy=False):
    m, k = x.shape
    n = w.shape[1]
    out_specs = [pl.BlockSpec((tm, tn), lambda i, j: (i, j)) for _ in out_dtypes]
    out_shape = [jax.ShapeDtypeStruct((m, n), dt) for dt in out_dtypes]
    if head_transposed_copy:
        out_specs.append(pl.BlockSpec((tn // HEAD_DIM * HEAD_ROWS_T, tm), lambda i, j: (j, i)))
        out_shape.append(jax.ShapeDtypeStruct((n // HEAD_DIM * HEAD_ROWS_T, m), BF16))
    outs = pl.pallas_call(
        functools.partial(_norm_matmul_kernel, out_scale=out_scale, head_transposed_copy=head_transposed_copy),
        grid=(m // tm, n // tn),
        in_specs=[
            pl.BlockSpec((tm, k), lambda i, j: (i, 0)),
            pl.BlockSpec((1, k), lambda i, j: (0, 0)),
            pl.BlockSpec((k, tn), lambda i, j: (0, j)),
        ],
        out_specs=out_specs,
        out_shape=out_shape,
        scratch_shapes=[pltpu.VMEM((tm, k), BF16)],
        compiler_params=_params("parallel", "arbitrary"),
        name="norm_matmul",
    )(x, gain.reshape(1, k), w)
    return outs[0] if len(outs) == 1 else outs


def _matmul_norm_res_kernel(a_ref, w_ref, g_ref, x_ref, o_ref):
    y = _dot(a_ref[...], w_ref[...])
    o_ref[...] = x_ref[...] + _rms(y, g_ref[...])


def matmul_norm_res(a, w, gain, x, *, tm):
    m, k = a.shape
    d = w.shape[1]
    return pl.pallas_call(
        _matmul_norm_res_kernel,
        grid=(m // tm,),
        in_specs=[
            pl.BlockSpec((tm, k), lambda i: (i, 0)),
            pl.BlockSpec((k, d), lambda i: (0, 0)),
            pl.BlockSpec((1, d), lambda i: (0, 0)),
            pl.BlockSpec((tm, d), lambda i: (i, 0)),
        ],
        out_specs=pl.BlockSpec((tm, d), lambda i: (i, 0)),
        out_shape=jax.ShapeDtypeStruct((m, d), F32),
        compiler_params=_params("parallel"),
        name="matmul_norm_res",
    )(a, w, gain.reshape(1, d), x)


def _mlp_kernel(x_ref, g_in_ref, w1_ref, w2_ref, g_out_ref, o_ref, hn_ref, acc_ref):
    f = pl.program_id(1)

    @pl.when(f == 0)
    def _():
        hn_ref[...] = _rms(x_ref[...], g_in_ref[...]).astype(BF16)
        acc_ref[...] = jnp.zeros_like(acc_ref)

    h = _dot(hn_ref[...], w1_ref[...])
    h = jnp.square(jnp.maximum(h, 0.0)).astype(BF16)
    acc_ref[...] += _dot(h, w2_ref[...])

    @pl.when(f == pl.num_programs(1) - 1)
    def _():
        o_ref[...] = x_ref[...] + _rms(acc_ref[...], g_out_ref[...])


def mlp_block(x, g_in, w1, w2, g_out, *, tm, tf):
    m, d = x.shape
    ff = w1.shape[1]
    return pl.pallas_call(
        _mlp_kernel,
        grid=(m // tm, ff // tf),
        in_specs=[
            pl.BlockSpec((tm, d), lambda i, f: (i, 0)),
            pl.BlockSpec((1, d), lambda i, f: (0, 0)),
            pl.BlockSpec((d, tf), lambda i, f: (0, f)),
            pl.BlockSpec((tf, d), lambda i, f: (f, 0)),
            pl.BlockSpec((1, d), lambda i, f: (0, 0)),
        ],
        out_specs=pl.BlockSpec((tm, d), lambda i, f: (i, 0)),
        out_shape=jax.ShapeDtypeStruct((m, d), F32),
        scratch_shapes=[pltpu.VMEM((tm, d), BF16), pltpu.VMEM((tm, d), F32)],
        compiler_params=_params("parallel", "arbitrary"),
        name="mlp_block",
    )(x, g_in.reshape(1, d), w1, w2, g_out.reshape(1, d))


CONV_ROW0 = 8


def _gdn_prep_kernel(proj_ref, conv0_ref, cw_ref, gp_ref,
                     w_ref, uu_ref, qg_ref, kd_ref, qkd_ref, egl_ref, conv_ref, xbuf_ref, *, C, G):
    R = C * G
    prev0 = CONV_ROW0 - (CONV_W - 1)

    @pl.when(pl.program_id(1) == 0)
    def _():
        xbuf_ref[prev0:CONV_ROW0, :] = conv0_ref[0]

    xbuf_ref[CONV_ROW0:CONV_ROW0 + R, :] = proj_ref[0, :, 0:A_QKV]
    cw = cw_ref[...]
    y = xbuf_ref[prev0:prev0 + R, :] * cw[0:1, :]
    for j in range(1, CONV_W):
        y = y + xbuf_ref[prev0 + j:prev0 + j + R, :] * cw[j:j + 1, :]
    y = y * _sigmoid(y)
    last = xbuf_ref[prev0 + R:CONV_ROW0 + R, :]
    xbuf_ref[prev0:CONV_ROW0, :] = last
    conv_ref[0] = last

    tail = proj_ref[0, :, A_GATE_COL:A_IN_PAD]
    gp = gp_ref[...]
    beta = _sigmoid(tail)
    t = tail + gp[1:2, :]
    softplus = jnp.maximum(t, 0.0) + jnp.log(1.0 + jnp.exp(-jnp.abs(t)))
    g = -jnp.exp(gp[0:1, :]) * softplus

    rr = lax.broadcasted_iota(jnp.int32, (R, R), 0)
    cc = lax.broadcasted_iota(jnp.int32, (R, R), 1)
    shift = C.bit_length() - 1
    same = (rr >> shift) == (cc >> shift)
    gc = _dot(jnp.where(same & (rr >= cc), 1.0, 0.0), g, HIGHEST)
    gl = _dot(jnp.where(same, 1.0, 0.0), g, HIGHEST)
    pad_rows = -R % LANES
    gc_t = (jnp.concatenate([gc, jnp.zeros((pad_rows, LANES), F32)], axis=0) if pad_rows else gc).T
    eg = jnp.exp(gc)
    ekd = jnp.exp(gl - gc)
    egl = jnp.exp(gl)
    for gi in range(G):
        egl_ref[0, gi] = egl[gi * C:gi * C + 1, :]

    row = lax.broadcasted_iota(jnp.int32, (C, C), 0)
    col = lax.broadcasted_iota(jnp.int32, (C, C), 1)
    incl = row >= col
    strict = row > col
    eye = (row == col).astype(F32)

    units = [(gi, h) for gi in range(G) for h in range(HEADS)]
    qn, kn = [], []
    for h in range(HEADS):
        q = y[:, h * HEAD_DIM:(h + 1) * HEAD_DIM]
        k = y[:, D_MODEL + h * HEAD_DIM:D_MODEL + (h + 1) * HEAD_DIM]
        qn.append(q * lax.rsqrt(jnp.sum(q * q, axis=-1, keepdims=True) + EPS) * (HEAD_DIM ** -0.5))
        kn.append(k * lax.rsqrt(jnp.sum(k * k, axis=-1, keepdims=True) + EPS))

    def rows(gi):
        return slice(gi * C, (gi + 1) * C)

    decay, qk, a_mat = {}, {}, {}
    for gi, h in units:
        rs, gl_h = rows(gi), HEADS + h
        g_col = gc[rs, gl_h:gl_h + 1]
        g_row = gc_t[gl_h:gl_h + 1, gi * C:(gi + 1) * C]
        decay[gi, h] = jnp.where(incl, jnp.exp(jnp.minimum(g_col - g_row, 0.0)), 0.0)
        k_bf = kn[h][rs].astype(BF16)
        qk_kk = _dot_nt(jnp.concatenate([qn[h][rs].astype(BF16), k_bf], axis=0), k_bf)
        qk[gi, h] = qk_kk[0:C]
        a_mat[gi, h] = jnp.where(strict, beta[rs, h:h + 1] * qk_kk[C:2 * C] * decay[gi, h], 0.0)
    t_inv = {u: eye - a_mat[u] for u in units}
    pw = {}
    for u in units:
        a_s = _split(a_mat[u])
        pw[u] = _dot_split(a_s, a_s)
    n_double = C.bit_length() - 2
    for step in range(n_double):
        for u in units:
            p_s = _split(pw[u])
            if step + 1 < n_double:
                tp = _dot_split(_split(jnp.concatenate([t_inv[u], pw[u]], axis=0)), p_s)
                t_inv[u] = t_inv[u] + tp[0:C]
                pw[u] = tp[C:2 * C]
            else:
                t_inv[u] = t_inv[u] + _dot_split(_split(t_inv[u]), p_s)
    for gi, h in units:
        rs, gl_h = rows(gi), HEADS + h
        hs = slice(h * HEAD_DIM, (h + 1) * HEAD_DIM)
        b_col = beta[rs, h:h + 1]
        eg_col = eg[rs, gl_h:gl_h + 1]
        k = kn[h][rs]
        v = y[rs, 2 * D_MODEL + h * HEAD_DIM:2 * D_MODEL + (h + 1) * HEAD_DIM]
        rhs = jnp.concatenate([b_col * v, (b_col * eg_col) * k], axis=-1).astype(BF16)
        sol = _dot(t_inv[gi, h].astype(BF16), rhs)
        uu_ref[0, rs, hs] = sol[:, 0:HEAD_DIM].astype(uu_ref.dtype)
        w_ref[0, rs, hs] = sol[:, HEAD_DIM:].astype(w_ref.dtype)
        qg_ref[0, rs, hs] = (eg_col * qn[h][rs]).astype(qg_ref.dtype)
        kd_ref[0, rs, hs] = (ekd[rs, gl_h:gl_h + 1] * k).astype(kd_ref.dtype)
        qkd_ref[0, rs, h * C:(h + 1) * C] = (qk[gi, h] * decay[gi, h]).astype(qkd_ref.dtype)


def gdn_prep(proj, conv0, conv_w, gate_par, *, C, G):
    b, t, _ = proj.shape
    r = C * G
    nc = t // C
    row_spec = pl.BlockSpec((1, r, D_MODEL), lambda i, c: (i, c, 0))
    conv_spec = pl.BlockSpec((1, CONV_W - 1, A_QKV), lambda i, c: (i, 0, 0))
    return pl.pallas_call(
        functools.partial(_gdn_prep_kernel, C=C, G=G),
        grid=(b, t // r),
        in_specs=[
            pl.BlockSpec((1, r, A_IN_PAD), lambda i, c: (i, c, 0)),
            conv_spec,
            pl.BlockSpec((CONV_W, A_QKV), lambda i, c: (0, 0)),
            pl.BlockSpec((2, LANES), lambda i, c: (0, 0)),
        ],
        out_specs=[
            row_spec, row_spec, row_spec, row_spec,
            pl.BlockSpec((1, r, HEADS * C), lambda i, c: (i, c, 0)),
            pl.BlockSpec((1, G, 1, LANES), lambda i, c: (i, c, 0, 0)),
            conv_spec,
        ],
        out_shape=[jax.ShapeDtypeStruct((b, t, D_MODEL), BF16)] * 4 + [
            jax.ShapeDtypeStruct((b, t, HEADS * C), BF16),
            jax.ShapeDtypeStruct((b, nc, 1, LANES), F32),
            jax.ShapeDtypeStruct((b, CONV_W - 1, A_QKV), F32),
        ],
        scratch_shapes=[pltpu.VMEM((CONV_ROW0 + r, A_QKV), F32)],
        compiler_params=_params("arbitrary", "arbitrary"),
        name="gdn_prep",
    )(proj, conv0, conv_w, gate_par)


def _gdn_scan_kernel(w_ref, uu_ref, qg_ref, kd_ref, qkd_ref, egl_ref, z_ref, s0_ref, og_ref, o_ref, s_ref, *, C, G):
    @pl.when(pl.program_id(1) == 0)
    def _():
        s_ref[0] = s0_ref[0]

    heads = range(HEADS)
    for gi in range(G):
        rs = slice(gi * C, (gi + 1) * C)
        egl = egl_ref[0, gi]

        def hs(h):
            return slice(h * HEAD_DIM, (h + 1) * HEAD_DIM)

        s_old = [s_ref[0, h] for h in heads]
        s_bf = [s.astype(BF16) for s in s_old]
        wq = [_dot(jnp.concatenate([w_ref[0, rs, hs(h)], qg_ref[0, rs, hs(h)]], axis=0), s_bf[h]) for h in heads]
        u_bf = [(uu_ref[0, rs, hs(h)].astype(F32) - wq[h][0:C]).astype(BF16) for h in heads]
        o = [wq[h][C:2 * C] + _dot(qkd_ref[0, rs, h * C:(h + 1) * C], u_bf[h]) for h in heads]
        for h in heads:
            s_ref[0, h] = egl[:, HEADS + h:HEADS + h + 1] * s_old[h] + _dot_tn(kd_ref[0, rs, hs(h)], u_bf[h])
        for h in heads:
            z = z_ref[0, rs, hs(h)]
            o_ref[0, rs, hs(h)] = (_rms(o[h], og_ref[...]) * (z * _sigmoid(z))).astype(o_ref.dtype)


def gdn_scan(w, uu, qg, kd, qkd, egl, proj, s0, o_gain, *, C, G):
    b, t, _ = w.shape
    r = C * G
    row_spec = pl.BlockSpec((1, r, D_MODEL), lambda i, c: (i, c, 0))
    state_spec = pl.BlockSpec((1, HEADS, HEAD_DIM, HEAD_DIM), lambda i, c: (i, 0, 0, 0))
    return pl.pallas_call(
        functools.partial(_gdn_scan_kernel, C=C, G=G),
        grid=(b, t // r),
        in_specs=[
            row_spec, row_spec, row_spec, row_spec,
            pl.BlockSpec((1, r, HEADS * C), lambda i, c: (i, c, 0)),
            pl.BlockSpec((1, G, 1, LANES), lambda i, c: (i, c, 0, 0)),
            pl.BlockSpec((1, r, D_MODEL), lambda i, c: (i, c, A_QKV // D_MODEL)),
            state_spec,
            pl.BlockSpec((1, HEAD_DIM), lambda i, c: (0, 0)),
        ],
        out_specs=[row_spec, state_spec],
        out_shape=[
            jax.ShapeDtypeStruct((b, t, D_MODEL), BF16),
            jax.ShapeDtypeStruct((b, HEADS, HEAD_DIM, HEAD_DIM), F32),
        ],
        compiler_params=_params("arbitrary", "arbitrary"),
        name="gdn_scan",
    )(w, uu, qg, kd, qkd, egl, proj, s0, o_gain.reshape(1, HEAD_DIM))


def _lambda(lam_ref, lam_init):
    lp = lam_ref[...]
    return (jnp.exp(jnp.sum(lp[0:1] * lp[1:2], axis=-1, keepdims=True))
            - jnp.exp(jnp.sum(lp[2:3] * lp[3:4], axis=-1, keepdims=True)) + lam_init)


KEY_COLS = 3


def _attn_prompt_kernel(qi_ref, kj_ref, slope_ref, lam_ref, sg_ref, kx_ref, q_ref, k_ref, vt_ref, o_ref, m_ref, acc_ref,
                        *, t, lam_init):
    pair = pl.program_id(1)
    qi = qi_ref[pair]
    kj = kj_ref[pair]

    @pl.when(kj == 0)
    def _():
        m_ref[...] = jnp.full_like(m_ref, NEG)
        acc_ref[...] = jnp.zeros_like(acc_ref)

    slope = slope_ref[0, :, 0:1] * LOG2E
    lane = lax.broadcasted_iota(jnp.int32, (1, HEAD_DIM), 1)
    map_lanes = [lane < MAP_DIM, lane >= MAP_DIM]
    q = q_ref[...]
    k = k_ref[...]
    vt_aug = vt_ref[...]

    def accumulate(scores, shift_const):
        probs, alphas = [], []
        for mp in range(2):
            m_old = m_ref[mp]
            m_new = jnp.maximum(m_old, jnp.max(scores[mp], axis=0, keepdims=True) + shift_const)
            probs.append(jnp.exp2(scores[mp] - (m_new - shift_const)).astype(BF16))
            alphas.append(jnp.exp2(m_old - m_new))
            m_ref[mp] = m_new
        pv = [_dot(vt_aug, probs[mp]) for mp in range(2)]
        for mp in range(2):
            acc_ref[mp] = alphas[mp] * acc_ref[mp] + pv[mp]

    @pl.when(kj < qi)
    def _():
        tile_shift = slope * ((kj - qi) * t).astype(F32)
        scores = []
        for mp in range(2):
            a0 = MAP_DIM if mp == 0 else 0
            extra = (lane >= a0) & (lane < a0 + KEY_COLS)
            qa = jnp.where(map_lanes[mp], q, jnp.where(extra, 1.0, 0.0).astype(BF16))
            ka = jnp.where(map_lanes[mp], k, kx_ref[0, mp])
            scores.append(_dot_nt(ka, qa))
        accumulate(scores, tile_shift)

    @pl.when(kj == qi)
    def _():
        j = lax.broadcasted_iota(jnp.int32, (t, 1), 0)
        i = lax.broadcasted_iota(jnp.int32, (1, t), 1)
        allowed = (j >> CHUNK_SHIFT) <= (i >> CHUNK_SHIFT)
        bias = slope * (i - jnp.abs(i - j)).astype(F32)
        scores = []
        for mp in range(2):
            km = jnp.where(map_lanes[mp], k, jnp.zeros_like(k))
            scores.append(jnp.where(allowed, _dot_nt(km, q) + bias, NEG))
        accumulate(scores, 0.0)
        lam = _lambda(lam_ref, lam_init)
        a0 = acc_ref[0]
        a1 = acc_ref[1]
        o_t = (a0[0:HEAD_DIM] / a0[HEAD_DIM:HEAD_DIM + 1]
               - lam * (a1[0:HEAD_DIM] / a1[HEAD_DIM:HEAD_DIM + 1]))
        o_t = o_t * (lax.rsqrt(jnp.mean(o_t * o_t, axis=0, keepdims=True) + EPS) * (1.0 - lam_init))
        o_ref[...] = (o_t.T * sg_ref[...]).astype(o_ref.dtype)


def alibi_key_columns(slopes, t):
    term = (LOG2E * slopes)[:, None] * jnp.arange(t, dtype=F32)[None, :]
    cols = jnp.zeros((HEADS, 2, t, HEAD_DIM), BF16)
    for c in range(KEY_COLS):
        piece = term.astype(BF16)
        term = term - piece.astype(F32)
        for mp in range(2):
            a0 = MAP_DIM if mp == 0 else 0
            cols = cols.at[:, mp, :, a0 + c].set(piece)
    return cols


def diff_attention_prompt(q, k, v_t, slopes, key_cols, lam_p, sub_gain, *, t, lam_init):
    t_all = q.shape[0]
    n = t_all // t
    assert t % CHUNK == 0
    qi = np.concatenate([np.full(i + 1, i, np.int32) for i in range(n)])
    kj = np.concatenate([np.arange(i + 1, dtype=np.int32) for i in range(n)])
    grid_spec = pltpu.PrefetchScalarGridSpec(
        num_scalar_prefetch=2,
        grid=(HEADS, len(qi)),
        in_specs=[
            pl.BlockSpec((1, 1, LANES), lambda h, p, qi_r, kj_r: (h, 0, 0)),
            pl.BlockSpec((4, MAP_DIM), lambda h, p, qi_r, kj_r: (0, 0)),
            pl.BlockSpec((1, HEAD_DIM), lambda h, p, qi_r, kj_r: (0, 0)),
            pl.BlockSpec((1, 2, t, HEAD_DIM), lambda h, p, qi_r, kj_r: (h, 0, 0, 0)),
            pl.BlockSpec((t, HEAD_DIM), lambda h, p, qi_r, kj_r: (qi_r[p], h)),
            pl.BlockSpec((t, HEAD_DIM), lambda h, p, qi_r, kj_r: (kj_r[p], h)),
            pl.BlockSpec((HEAD_ROWS_T, t), lambda h, p, qi_r, kj_r: (h, kj_r[p])),
        ],
        out_specs=pl.BlockSpec((t, HEAD_DIM), lambda h, p, qi_r, kj_r: (qi_r[p], h)),
        scratch_shapes=[pltpu.VMEM((2, 1, t), F32), pltpu.VMEM((2, HEAD_DIM + ONES_ROWS, t), F32)],
    )
    return pl.pallas_call(
        functools.partial(_attn_prompt_kernel, t=t, lam_init=lam_init),
        grid_spec=grid_spec,
        out_shape=jax.ShapeDtypeStruct((t_all, D_MODEL), BF16),
        compiler_params=_params("arbitrary", "arbitrary"),
        name="diff_attention_prompt",
    )(jnp.asarray(qi), jnp.asarray(kj), slopes, lam_p, sub_gain.reshape(1, HEAD_DIM), key_cols, q, k, v_t)


def _attn_decode_kernel(slope_ref, lam_ref, sg_ref, q_ref, k_ref, v_ref, kn_ref, vn_ref, o_ref, m_ref, l_ref, acc_ref,
                        *, tk, pos0, lam_init):
    j = pl.program_id(1)
    nk = pl.num_programs(1)
    n = q_ref.shape[1]

    @pl.when(j == 0)
    def _():
        m_ref[...] = jnp.full_like(m_ref, NEG)
        l_ref[...] = jnp.zeros_like(l_ref)
        acc_ref[...] = jnp.zeros_like(acc_ref)

    row = lax.broadcasted_iota(jnp.int32, (2 * n, 1), 0)
    q_pos = pos0 + jnp.where(row < n, row, row - n)
    lane = lax.broadcasted_iota(jnp.int32, (1, HEAD_DIM), 1)
    map_of_row_has_lane = (row < n) == (lane < MAP_DIM)

    def update(h, kb, vb, k0, n_keys):
        hs = slice(h * HEAD_DIM, (h + 1) * HEAD_DIM)
        slope = slope_ref[h, :, 0:1] * LOG2E
        q = q_ref[0, :, hs]
        q2 = jnp.concatenate([q, q], axis=0)
        q_stack = jnp.where(map_of_row_has_lane, q2, jnp.zeros_like(q2))
        k_pos = k0 + lax.broadcasted_iota(jnp.int32, (1, n_keys), 1)
        allowed = (k_pos >> CHUNK_SHIFT) <= (q_pos >> CHUNK_SHIFT)
        bias = -slope * jnp.abs(q_pos - k_pos).astype(F32)
        s = jnp.where(allowed, _dot_nt(q_stack, kb.astype(BF16)) + bias, NEG)
        m_old = m_ref[h]
        m_new = jnp.maximum(m_old, jnp.max(s, axis=-1, keepdims=True))
        alpha = jnp.exp2(m_old - m_new)
        p = jnp.exp2(s - m_new)
        l_ref[h] = alpha * l_ref[h] + jnp.sum(p, axis=-1, keepdims=True)
        acc_ref[h] = alpha * acc_ref[h] + _dot(p.astype(BF16), vb.astype(BF16))
        m_ref[h] = m_new

    for h in range(HEADS):
        head_rows = pl.ds(h, tk, stride=HEADS)
        update(h, k_ref[0, head_rows, :], v_ref[0, head_rows, :], j * tk, tk)

    @pl.when(j == nk - 1)
    def _():
        lam = _lambda(lam_ref, lam_init)
        for h in range(HEADS):
            hs = slice(h * HEAD_DIM, (h + 1) * HEAD_DIM)
            update(h, kn_ref[0, :, hs], vn_ref[0, :, hs], pos0, n)
            o = acc_ref[h] / l_ref[h]
            o = o[0:n] - lam * o[n:2 * n]
            o_ref[0, :, hs] = (_rms(o, sg_ref[...]) * (1.0 - lam_init)).astype(o_ref.dtype)


def diff_attention_decode(q, k, v, k_new, v_new, slopes, lam_p, sub_gain, *, tk, lam_init):
    b, n, _ = q.shape
    t_k = k.shape[1]
    assert t_k % tk == 0 and t_k % CHUNK == 0
    k = k.reshape(b, t_k * HEADS, HEAD_DIM)
    v = v.reshape(b, t_k * HEADS, HEAD_DIM)
    row_block = pl.BlockSpec((1, n, D_MODEL), lambda bi, j: (bi, 0, 0))
    kv_spec = pl.BlockSpec((1, tk * HEADS, HEAD_DIM), lambda bi, j: (bi, j, 0))
    return pl.pallas_call(
        functools.partial(_attn_decode_kernel, tk=tk, pos0=t_k, lam_init=lam_init),
        grid=(b, t_k // tk),
        in_specs=[
            pl.BlockSpec((HEADS, 1, LANES), lambda bi, j: (0, 0, 0)),
            pl.BlockSpec((4, MAP_DIM), lambda bi, j: (0, 0)),
            pl.BlockSpec((1, HEAD_DIM), lambda bi, j: (0, 0)),
            row_block, kv_spec, kv_spec, row_block, row_block,
        ],
        out_specs=row_block,
        out_shape=jax.ShapeDtypeStruct((b, n, D_MODEL), BF16),
        scratch_shapes=[
            pltpu.VMEM((HEADS, 2 * n, 1), F32),
            pltpu.VMEM((HEADS, 2 * n, 1), F32),
            pltpu.VMEM((HEADS, 2 * n, HEAD_DIM), F32),
        ],
        compiler_params=_params("arbitrary", "arbitrary"),
        name="diff_attention_decode",
    )(slopes, lam_p, sub_gain.reshape(1, HEAD_DIM), q, k, v, k_new, v_new)


def _trunk(x, conv_state, ssm_state, past_k, past_v, wts, *, tm, gdn_chunk, gdn_group, attn_tile):
    b, t, _ = x.shape
    m = b * t
    xf = x.reshape(m, D_MODEL)
    new_conv, new_ssm = [], []
    k_new = v_new = k_bf = v_t = None
    for l in range(DEPTH):
        gains = wts["norm_gains"][l]
        if l < N_A:
            proj = norm_matmul(xf, gains[0], wts["a_w_in"][l], tm=tm, tn=A_IN_PAD // 3).reshape(b, t, A_IN_PAD)
            w, uu, qg, kd, qkd, egl, conv_l = gdn_prep(proj, conv_state[l], wts["a_conv_w"][l], wts["gate_par"][l],
                                                       C=gdn_chunk, G=gdn_group)
            og, s_l = gdn_scan(w, uu, qg, kd, qkd, egl, proj, ssm_state[l], wts["a_o_gain"][l],
                               C=gdn_chunk, G=gdn_group)
            new_conv.append(conv_l)
            new_ssm.append(s_l)
            xf = matmul_norm_res(og.reshape(m, D_MODEL), wts["a_w_out"][l], gains[1], xf, tm=tm)
        else:
            j = l - N_A
            prompt = past_k is None
            if j == 0:
                k_new, k_bf = norm_matmul(xf, wts["kv_gain"], wts["w_k"], tm=tm, tn=D_MODEL, out_dtypes=(F32, BF16))
                v_new = norm_matmul(xf, wts["kv_gain"], wts["w_v"], tm=tm, tn=D_MODEL, head_transposed_copy=prompt)
                if prompt:
                    v_new, v_t = v_new
            q = norm_matmul(xf, gains[0], wts["b_w_q"][j], tm=tm, tn=D_MODEL, out_dtypes=(BF16,), out_scale=Q_SCALE)
            lam_init = 0.8 - 0.6 * math.exp(-0.3 * l)
            tail = (wts["b_lam"][j], wts["b_sub_gain"][j])
            if prompt:
                assert b == 1
                o = diff_attention_prompt(q, k_bf, v_t, wts["slopes"], alibi_key_columns(wts["slope_values"], attn_tile),
                                          *tail, t=attn_tile, lam_init=lam_init)
            else:
                o = diff_attention_decode(q.reshape(b, t, D_MODEL), past_k, past_v, k_new.reshape(b, t, D_MODEL),
                                          v_new.reshape(b, t, D_MODEL), wts["slopes"], *tail, tk=attn_tile,
                                          lam_init=lam_init)
            xf = matmul_norm_res(o.reshape(m, D_MODEL), wts["b_w_out"][j], gains[1], xf, tm=tm)
        xf = mlp_block(xf, gains[2], wts["mlp_w1"][l], wts["mlp_w2"][l], gains[3], tm=tm, tf=512)
    kv_shape = (b, t, HEADS, HEAD_DIM)
    return (xf.reshape(b, t, D_MODEL), jnp.stack(new_conv), jnp.stack(new_ssm),
            k_new.reshape(kv_shape), v_new.reshape(kv_shape))


def kernel(x_prompt, x_sample, state_conv, state_ssm, cache_k, cache_v, norm_gains, a_w_in, a_conv_w, a_log,
           a_dt_bias, a_o_gain, a_w_out, kv_gain, w_kv, b_w_q, b_lam, b_sub_gain, b_w_out, mlp_w1, mlp_w2):
    a_in = a_w_in.shape[-1]
    gate_par = jnp.zeros((N_A, 2, LANES), F32)
    gate_par = gate_par.at[:, 0, HEADS:2 * HEADS].set(a_log.astype(F32))
    gate_par = gate_par.at[:, 1, HEADS:2 * HEADS].set(a_dt_bias.astype(F32))
    slopes = 2.0 ** (-8.0 * jnp.arange(1, HEADS + 1, dtype=F32) / HEADS)
    wts = dict(
        norm_gains=norm_gains.astype(F32),
        a_w_in=jnp.pad(a_w_in, ((0, 0), (0, 0), (0, A_IN_PAD - a_in))).astype(BF16),
        a_conv_w=a_conv_w.astype(F32),
        gate_par=gate_par,
        a_o_gain=a_o_gain.astype(F32),
        a_w_out=a_w_out.astype(BF16),
        kv_gain=kv_gain.astype(F32),
        w_k=w_kv[:, :D_MODEL].astype(BF16),
        w_v=w_kv[:, D_MODEL:].astype(BF16),
        b_w_q=b_w_q.astype(BF16),
        b_lam=b_lam.astype(F32),
        b_sub_gain=b_sub_gain.astype(F32),
        b_w_out=b_w_out.astype(BF16),
        mlp_w1=mlp_w1.astype(BF16),
        mlp_w2=mlp_w2.astype(BF16),
        slopes=jnp.broadcast_to(slopes[:, None, None], (HEADS, 1, LANES)),
        slope_values=slopes,
    )
    bp, tp, _ = x_prompt.shape
    conv0 = jnp.zeros((N_A, bp, CONV_W - 1, A_QKV), F32)
    ssm0 = jnp.zeros((N_A, bp, HEADS, HEAD_DIM, HEAD_DIM), F32)
    y_p, p_conv, p_ssm, p_k, p_v = _trunk(x_prompt, conv0, ssm0, None, None, wts,
                                          tm=512, gdn_chunk=CHUNK, gdn_group=4, attn_tile=512)
    bs, ts, _ = x_sample.shape
    y_s, s_conv, s_ssm, s_k, s_v = _trunk(x_sample, state_conv, state_ssm, cache_k, cache_v,
                                          wts, tm=bs * ts, gdn_chunk=ts, gdn_group=1, attn_tile=512)
    return (y_p, y_s, p_conv, p_ssm, p_k, p_v, s_conv, s_ssm, s_k, s_v)
```

```python
import functools
import math

import jax
import jax.numpy as jnp
import numpy as np
from jax import lax
from jax.experimental import pallas as pl
from jax.experimental.pallas import tpu as pltpu

F32 = jnp.float32
BF16 = jnp.bfloat16

D_MODEL = 1024
DEPTH = 4
N_A = DEPTH // 2
CHUNK = 64
CHUNK_SHIFT = CHUNK.bit_length() - 1
assert 1 << CHUNK_SHIFT == CHUNK
HEADS = 8
HEAD_DIM = D_MODEL // HEADS
CONV_W = 4
A_QKV = 3 * D_MODEL
A_GATE_COL = A_QKV + D_MODEL
LANES = 128
A_IN_PAD = A_GATE_COL + LANES
MAP_DIM = HEAD_DIM // 2
D_FF = 4 * D_MODEL
EPS = 1e-6
NEG = -1e30
VMEM_LIMIT = 48 * 1024 * 1024
HIGHEST = lax.Precision.HIGHEST
ONES_ROWS = 16
HEAD_ROWS_T = HEAD_DIM + ONES_ROWS
LOG2E = math.log2(math.e)
Q_SCALE = MAP_DIM ** -0.5 * LOG2E


def _dot(a, b, precision=None):
    return jnp.dot(a, b, preferred_element_type=F32, precision=precision)


def _dot_nt(a, b):
    return lax.dot_general(a, b, (((1,), (1,)), ((), ())), preferred_element_type=F32)


def _dot_tn(a, b):
    return lax.dot_general(a, b, (((0,), (0,)), ((), ())), preferred_element_type=F32)


def _split(a):
    hi = a.astype(BF16)
    return hi, (a - hi.astype(F32)).astype(BF16)


def _dot_split(a, b):
    return _dot(a[0], b[0]) + (_dot(a[0], b[1]) + _dot(a[1], b[0]))


def _rms(x, gain):
    return x * lax.rsqrt(jnp.mean(x * x, axis=-1, keepdims=True) + EPS) * gain


def _sigmoid(x):
    return 1.0 / (1.0 + jnp.exp(-x))


def _params(*semantics):
    return pltpu.CompilerParams(dimension_semantics=semantics, vmem_limit_bytes=VMEM_LIMIT)


def _norm_matmul_kernel(x_ref, g_ref, w_ref, *out_and_scratch, out_scale, head_transposed_copy):
    *o_refs, hn_ref = out_and_scratch

    @pl.when(pl.program_id(1) == 0)
    def _():
        hn_ref[...] = _rms(x_ref[...], g_ref[...]).astype(BF16)

    y = _dot(hn_ref[...], w_ref[...])
    if out_scale != 1.0:
        y = y * out_scale
    if head_transposed_copy:
        *o_refs, ot_ref = o_refs
        y_t = y.T.astype(ot_ref.dtype)
        for h in range(y_t.shape[0] // HEAD_DIM):
            r0 = h * HEAD_ROWS_T
            ot_ref[r0:r0 + HEAD_DIM, :] = y_t[h * HEAD_DIM:(h + 1) * HEAD_DIM]
            ot_ref[r0 + HEAD_DIM:r0 + HEAD_ROWS_T, :] = jnp.ones((ONES_ROWS, y_t.shape[1]), ot_ref.dtype)
    for o_ref in o_refs:
        o_ref[...] = y.astype(o_ref.dtype)


def norm_matmul(x, gain, w, *, tm, tn, out_dtypes=(F32,), out_scale=1.0, head_transposed_copy=False):
    m, k = x.shape
    n = w.shape[1]
    out_specs = [pl.BlockSpec((tm, tn), lambda i, j: (i, j)) for _ in out_dtypes]
    out_shape = [jax.ShapeDtypeStruct((m, n), dt) for dt in out_dtypes]
    if head_transposed_copy:
        out_specs.append(pl.BlockSpec((tn // HEAD_DIM * HEAD_ROWS_T, tm), lambda i, j: (j, i)))
        out_shape.append(jax.ShapeDtypeStruct((n // HEAD_DIM * HEAD_ROWS_T, m), BF16))
    outs = pl.pallas_call(
        functools.partial(_norm_matmul_kernel, out_scale=out_scale, head_transposed_copy=head_transposed_copy),
        grid=(m // tm, n // tn),
        in_specs=[
            pl.BlockSpec((tm, k), lambda i, j: (i, 0)),
            pl.BlockSpec((1, k), lambda i, j: (0, 0)),
            pl.BlockSpec((k, tn), lambda i, j: (0, j)),
        ],
        out_specs=out_specs,
        out_shape=out_shape,
        scratch_shapes=[pltpu.VMEM((tm, k), BF16)],
        compiler_params=_params("parallel", "arbitrary"),
        name="norm_matmul",
    )(x, gain.reshape(1, k), w)
    return outs[0] if len(outs) == 1 else outs


def _matmul_norm_res_kernel(a_ref, w_ref, g_ref, x_ref, o_ref):
    y = _dot(a_ref[...], w_ref[...])
    o_ref[...] = x_ref[...] + _rms(y, g_ref[...])


def matmul_norm_res(a, w, gain, x, *, tm):
    m, k = a.shape
    d = w.shape[1]
    return pl.pallas_call(
        _matmul_norm_res_kernel,
        grid=(m // tm,),
        in_specs=[
            pl.BlockSpec((tm, k), lambda i: (i, 0)),
            pl.BlockSpec((k, d), lambda i: (0, 0)),
            pl.BlockSpec((1, d), lambda i: (0, 0)),
            pl.BlockSpec((tm, d), lambda i: (i, 0)),
        ],
        out_specs=pl.BlockSpec((tm, d), lambda i: (i, 0)),
        out_shape=jax.ShapeDtypeStruct((m, d), F32),
        compiler_params=_params("parallel"),
        name="matmul_norm_res",
    )(a, w, gain.reshape(1, d), x)


def _mlp_kernel(x_ref, g_in_ref, w1_ref, w2_ref, g_out_ref, o_ref, hn_ref, acc_ref):
    f = pl.program_id(1)

    @pl.when(f == 0)
    def _():
        hn_ref[...] = _rms(x_ref[...], g_in_ref[...]).astype(BF16)
        acc_ref[...] = jnp.zeros_like(acc_ref)

    h = _dot(hn_ref[...], w1_ref[...])
    h = jnp.square(jnp.maximum(h, 0.0)).astype(BF16)
    acc_ref[...] += _dot(h, w2_ref[...])

    @pl.when(f == pl.num_programs(1) - 1)
    def _():
        o_ref[...] = x_ref[...] + _rms(acc_ref[...], g_out_ref[...])


def mlp_block(x, g_in, w1, w2, g_out, *, tm, tf):
    m, d = x.shape
    ff = w1.shape[1]
    return pl.pallas_call(
        _mlp_kernel,
        grid=(m // tm, ff // tf),
        in_specs=[
            pl.BlockSpec((tm, d), lambda i, f: (i, 0)),
            pl.BlockSpec((1, d), lambda i, f: (0, 0)),
            pl.BlockSpec((d, tf), lambda i, f: (0, f)),
            pl.BlockSpec((tf, d), lambda i, f: (f, 0)),
            pl.BlockSpec((1, d), lambda i, f: (0, 0)),
        ],
        out_specs=pl.BlockSpec((tm, d), lambda i, f: (i, 0)),
        out_shape=jax.ShapeDtypeStruct((m, d), F32),
        scratch_shapes=[pltpu.VMEM((tm, d), BF16), pltpu.VMEM((tm, d), F32)],
        compiler_params=_params("parallel", "arbitrary"),
        name="mlp_block",
    )(x, g_in.reshape(1, d), w1, w2, g_out.reshape(1, d))


CONV_ROW0 = 8


def _gdn_prep_kernel(proj_ref, conv0_ref, cw_ref, gp_ref,
                     w_ref, uu_ref, qg_ref, kd_ref, qkd_ref, egl_ref, conv_ref, xbuf_ref, *, C, G):
    R = C * G
    prev0 = CONV_ROW0 - (CONV_W - 1)

    @pl.when(pl.program_id(1) == 0)
    def _():
        xbuf_ref[prev0:CONV_ROW0, :] = conv0_ref[0]

    xbuf_ref[CONV_ROW0:CONV_ROW0 + R, :] = proj_ref[0, :, 0:A_QKV]
    cw = cw_ref[...]
    y = xbuf_ref[prev0:prev0 + R, :] * cw[0:1, :]
    for j in range(1, CONV_W):
        y = y + xbuf_ref[prev0 + j:prev0 + j + R, :] * cw[j:j + 1, :]
    y = y * _sigmoid(y)
    last = xbuf_ref[prev0 + R:CONV_ROW0 + R, :]
    xbuf_ref[prev0:CONV_ROW0, :] = last
    conv_ref[0] = last

    tail = proj_ref[0, :, A_GATE_COL:A_IN_PAD]
    gp = gp_ref[...]
    beta = _sigmoid(tail)
    t = tail + gp[1:2, :]
    softplus = jnp.maximum(t, 0.0) + jnp.log(1.0 + jnp.exp(-jnp.abs(t)))
    g = -jnp.exp(gp[0:1, :]) * softplus

    rr = lax.broadcasted_iota(jnp.int32, (R, R), 0)
    cc = lax.broadcasted_iota(jnp.int32, (R, R), 1)
    shift = C.bit_length() - 1
    same = (rr >> shift) == (cc >> shift)
    gc = _dot(jnp.where(same & (rr >= cc), 1.0, 0.0), g, HIGHEST)
    gl = _dot(jnp.where(same, 1.0, 0.0), g, HIGHEST)
    pad_rows = -R % LANES
    gc_t = (jnp.concatenate([gc, jnp.zeros((pad_rows, LANES), F32)], axis=0) if pad_rows else gc).T
    eg = jnp.exp(gc)
    ekd = jnp.exp(gl - gc)
    egl = jnp.exp(gl)
    for gi in range(G):
        egl_ref[0, gi] = egl[gi * C:gi * C + 1, :]

    row = lax.broadcasted_iota(jnp.int32, (C, C), 0)
    col = lax.broadcasted_iota(jnp.int32, (C, C), 1)
    incl = row >= col
    strict = row > col
    eye = (row == col).astype(F32)

    units = [(gi, h) for gi in range(G) for h in range(HEADS)]
    qn, kn = [], []
    for h in range(HEADS):
        q = y[:, h * HEAD_DIM:(h + 1) * HEAD_DIM]
        k = y[:, D_MODEL + h * HEAD_DIM:D_MODEL + (h + 1) * HEAD_DIM]
        qn.append(q * lax.rsqrt(jnp.sum(q * q, axis=-1, keepdims=True) + EPS) * (HEAD_DIM ** -0.5))
        kn.append(k * lax.rsqrt(jnp.sum(k * k, axis=-1, keepdims=True) + EPS))

    def rows(gi):
        return slice(gi * C, (gi + 1) * C)

    decay, qk, a_mat = {}, {}, {}
    for gi, h in units:
        rs, gl_h = rows(gi), HEADS + h
        g_col = gc[rs, gl_h:gl_h + 1]
        g_row = gc_t[gl_h:gl_h + 1, gi * C:(gi + 1) * C]
        decay[gi, h] = jnp.where(incl, jnp.exp(jnp.minimum(g_col - g_row, 0.0)), 0.0)
        k_bf = kn[h][rs].astype(BF16)
        qk_kk = _dot_nt(jnp.concatenate([qn[h][rs].astype(BF16), k_bf], axis=0), k_bf)
        qk[gi, h] = qk_kk[0:C]
        a_mat[gi, h] = jnp.where(strict, beta[rs, h:h + 1] * qk_kk[C:2 * C] * decay[gi, h], 0.0)
    t_inv = {u: eye - a_mat[u] for u in units}
    pw = {}
    for u in units:
        a_s = _split(a_mat[u])
        pw[u] = _dot_split(a_s, a_s)
    n_double = C.bit_length() - 2
    for step in range(n_double):
        for u in units:
            p_s = _split(pw[u])
            if step + 1 < n_double:
                tp = _dot_split(_split(jnp.concatenate([t_inv[u], pw[u]], axis=0)), p_s)
                t_inv[u] = t_inv[u] + tp[0:C]
                pw[u] = tp[C:2 * C]
            else:
                t_inv[u] = t_inv[u] + _dot_split(_split(t_inv[u]), p_s)
    for gi, h in units:
        rs, gl_h = rows(gi), HEADS + h
        hs = slice(h * HEAD_DIM, (h + 1) * HEAD_DIM)
        b_col = beta[rs, h:h + 1]
        eg_col = eg[rs, gl_h:gl_h + 1]
        k = kn[h][rs]
        v = y[rs, 2 * D_MODEL + h * HEAD_DIM:2 * D_MODEL + (h + 1) * HEAD_DIM]
        rhs = jnp.concatenate([b_col * v, (b_col * eg_col) * k], axis=-1).astype(BF16)
        sol = _dot(t_inv[gi, h].astype(BF16), rhs)
        uu_ref[0, rs, hs] = sol[:, 0:HEAD_DIM].astype(uu_ref.dtype)
        w_ref[0, rs, hs] = sol[:, HEAD_DIM:].astype(w_ref.dtype)
        qg_ref[0, rs, hs] = (eg_col * qn[h][rs]).astype(qg_ref.dtype)
        kd_ref[0, rs, hs] = (ekd[rs, gl_h:gl_h + 1] * k).astype(kd_ref.dtype)
        qkd_ref[0, rs, h * C:(h + 1) * C] = (qk[gi, h] * decay[gi, h]).astype(qkd_ref.dtype)


def gdn_prep(proj, conv0, conv_w, gate_par, *, C, G):
    b, t, _ = proj.shape
    r = C * G
    nc = t // C
    row_spec = pl.BlockSpec((1, r, D_MODEL), lambda i, c: (i, c, 0))
    conv_spec = pl.BlockSpec((1, CONV_W - 1, A_QKV), lambda i, c: (i, 0, 0))
    return pl.pallas_call(
        functools.partial(_gdn_prep_kernel, C=C, G=G),
        grid=(b, t // r),
        in_specs=[
            pl.BlockSpec((1, r, A_IN_PAD), lambda i, c: (i, c, 0)),
            conv_spec,
            pl.BlockSpec((CONV_W, A_QKV), lambda i, c: (0, 0)),
            pl.BlockSpec((2, LANES), lambda i, c: (0, 0)),
        ],
        out_specs=[
            row_spec, row_spec, row_spec, row_spec,
            pl.BlockSpec((1, r, HEADS * C), lambda i, c: (i, c, 0)),
            pl.BlockSpec((1, G, 1, LANES), lambda i, c: (i, c, 0, 0)),
            conv_spec,
        ],
        out_shape=[jax.ShapeDtypeStruct((b, t, D_MODEL), BF16)] * 4 + [
            jax.ShapeDtypeStruct((b, t, HEADS * C), BF16),
            jax.ShapeDtypeStruct((b, nc, 1, LANES), F32),
            jax.ShapeDtypeStruct((b, CONV_W - 1, A_QKV), F32),
        ],
        scratch_shapes=[pltpu.VMEM((CONV_ROW0 + r, A_QKV), F32)],
        compiler_params=_params("arbitrary", "arbitrary"),
        name="gdn_prep",
    )(proj, conv0, conv_w, gate_par)


def _gdn_scan_kernel(w_ref, uu_ref, qg_ref, kd_ref, qkd_ref, egl_ref, z_ref, s0_ref, og_ref, o_ref, s_ref, *, C, G):
    @pl.when(pl.program_id(1) == 0)
    def _():
        s_ref[0] = s0_ref[0]

    heads = range(HEADS)
    for gi in range(G):
        rs = slice(gi * C, (gi + 1) * C)
        egl = egl_ref[0, gi]

        def hs(h):
            return slice(h * HEAD_DIM, (h + 1) * HEAD_DIM)

        s_old = [s_ref[0, h] for h in heads]
        s_bf = [s.astype(BF16) for s in s_old]
        wq = [_dot(jnp.concatenate([w_ref[0, rs, hs(h)], qg_ref[0, rs, hs(h)]], axis=0), s_bf[h]) for h in heads]
        u_bf = [(uu_ref[0, rs, hs(h)].astype(F32) - wq[h][0:C]).astype(BF16) for h in heads]
        o = [wq[h][C:2 * C] + _dot(qkd_ref[0, rs, h * C:(h + 1) * C], u_bf[h]) for h in heads]
        for h in heads:
            s_ref[0, h] = egl[:, HEADS + h:HEADS + h + 1] * s_old[h] + _dot_tn(kd_ref[0, rs, hs(h)], u_bf[h])
        for h in heads:
            z = z_ref[0, rs, hs(h)]
            o_ref[0, rs, hs(h)] = (_rms(o[h], og_ref[...]) * (z * _sigmoid(z))).astype(o_ref.dtype)


def gdn_scan(w, uu, qg, kd, qkd, egl, proj, s0, o_gain, *, C, G):
    b, t, _ = w.shape
    r = C * G
    row_spec = pl.BlockSpec((1, r, D_MODEL), lambda i, c: (i, c, 0))
    state_spec = pl.BlockSpec((1, HEADS, HEAD_DIM, HEAD_DIM), lambda i, c: (i, 0, 0, 0))
    return pl.pallas_call(
        functools.partial(_gdn_scan_kernel, C=C, G=G),
        grid=(b, t // r),
        in_specs=[
            row_spec, row_spec, row_spec, row_spec,
            pl.BlockSpec((1, r, HEADS * C), lambda i, c: (i, c, 0)),
            pl.BlockSpec((1, G, 1, LANES), lambda i, c: (i, c, 0, 0)),
            pl.BlockSpec((1, r, D_MODEL), lambda i, c: (i, c, A_QKV // D_MODEL)),
            state_spec,
            pl.BlockSpec((1, HEAD_DIM), lambda i, c: (0, 0)),
        ],
        out_specs=[row_spec, state_spec],
        out_shape=[
            jax.ShapeDtypeStruct((b, t, D_MODEL), BF16),
            jax.ShapeDtypeStruct((b, HEADS, HEAD_DIM, HEAD_DIM), F32),
        ],
        compiler_params=_params("arbitrary", "arbitrary"),
        name="gdn_scan",
    )(w, uu, qg, kd, qkd, egl, proj, s0, o_gain.reshape(1, HEAD_DIM))


def _lambda(lam_ref, lam_init):
    lp = lam_ref[...]
    return (jnp.exp(jnp.sum(lp[0:1] * lp[1:2], axis=-1, keepdims=True))
            - jnp.exp(jnp.sum(lp[2:3] * lp[3:4], axis=-1, keepdims=True)) + lam_init)


KEY_COLS = 3


def _attn_prompt_kernel(qi_ref, kj_ref, slope_ref, lam_ref, sg_ref, kx_ref, q_ref, k_ref, vt_ref, o_ref,
                        s_even_ref, s_odd_ref, m_ref, acc_ref, *, t, n_pairs, lam_init):
    step = pl.program_id(1)
    slope = slope_ref[0, :, 0:1] * LOG2E
    lane = lax.broadcasted_iota(jnp.int32, (1, HEAD_DIM), 1)
    map_lanes = [lane < MAP_DIM, lane >= MAP_DIM]

    def score_stage(s_ref):
        q = q_ref[...]
        k = k_ref[...]
        for mp in range(2):
            a0 = MAP_DIM if mp == 0 else 0
            extra = (lane >= a0) & (lane < a0 + KEY_COLS)
            qa = jnp.where(map_lanes[mp], q, jnp.where(extra, 1.0, 0.0).astype(BF16))
            ka = jnp.where(map_lanes[mp], k, kx_ref[0, mp])
            s_ref[mp] = _dot_nt(ka, qa)

    def mask_diagonal(s_ref):
        j = lax.broadcasted_iota(jnp.int32, (t, 1), 0)
        i = lax.broadcasted_iota(jnp.int32, (1, t), 1)
        allowed = (j >> CHUNK_SHIFT) <= (i >> CHUNK_SHIFT)
        fix = jnp.where(j > i, (2.0 * slope) * (i - j).astype(F32), 0.0)
        for mp in range(2):
            s_ref[mp] = jnp.where(allowed, s_ref[mp] + fix, NEG)

    def softmax_stage(s_ref, shift_const):
        vt_aug = vt_ref[...]
        probs, alphas = [], []
        for mp in range(2):
            scores = s_ref[mp]
            m_old = m_ref[mp]
            m_new = jnp.maximum(m_old, jnp.max(scores, axis=0, keepdims=True) + shift_const)
            probs.append(jnp.exp2(scores - (m_new - shift_const)).astype(BF16))
            alphas.append(jnp.exp2(m_old - m_new))
            m_ref[mp] = m_new
        pv = [_dot(vt_aug, probs[mp]) for mp in range(2)]
        for mp in range(2):
            acc_ref[mp] = alphas[mp] * acc_ref[mp] + pv[mp]

    def finalize():
        lam = _lambda(lam_ref, lam_init)
        a0 = acc_ref[0]
        a1 = acc_ref[1]
        o_t = (a0[0:HEAD_DIM] / a0[HEAD_DIM:HEAD_DIM + 1]
               - lam * (a1[0:HEAD_DIM] / a1[HEAD_DIM:HEAD_DIM + 1]))
        o_t = o_t * (lax.rsqrt(jnp.mean(o_t * o_t, axis=0, keepdims=True) + EPS) * (1.0 - lam_init))
        o_ref[...] = (o_t.T * sg_ref[...]).astype(o_ref.dtype)

    scored = jnp.minimum(step, n_pairs - 1)
    scored_is_diagonal = kj_ref[scored] == qi_ref[scored]
    consumed = jnp.maximum(step - 1, 0)
    qi_c = qi_ref[consumed]
    kj_c = kj_ref[consumed]

    @pl.when(step == 0)
    def _():
        score_stage(s_even_ref)

    @pl.when((step > 0) & (kj_c == 0))
    def _():
        m_ref[...] = jnp.full_like(m_ref, NEG)
        acc_ref[...] = jnp.zeros_like(acc_ref)

    tile_shift = slope * ((kj_c - qi_c) * t).astype(F32)
    for parity, (s_write, s_read) in enumerate([(s_even_ref, s_odd_ref), (s_odd_ref, s_even_ref)]):
        @pl.when((step > 0) & (step % 2 == parity))
        def _():
            score_stage(s_write)
            softmax_stage(s_read, tile_shift)

    for parity, s_write in enumerate([s_even_ref, s_odd_ref]):
        @pl.when(scored_is_diagonal & (step % 2 == parity))
        def _():
            mask_diagonal(s_write)

    @pl.when((step > 0) & (kj_c == qi_c))
    def _():
        finalize()


def alibi_slopes():
    return 2.0 ** (-8.0 * np.arange(1, HEADS + 1, dtype=np.float32) / HEADS)


def alibi_key_columns(t):
    term = (np.float32(LOG2E) * alibi_slopes())[:, None] * np.arange(t, dtype=np.float32)[None, :]
    cols = np.zeros((HEADS, 2, t, HEAD_DIM), np.float32)
    for c in range(KEY_COLS):
        piece = term.astype(BF16).astype(np.float32)
        term = term - piece
        for mp in range(2):
            a0 = MAP_DIM if mp == 0 else 0
            cols[:, mp, :, a0 + c] = piece
    return jnp.asarray(cols.astype(BF16))


def diff_attention_prompt(q, k, v_t, slopes, key_cols, lam_p, sub_gain, *, t, lam_init):
    t_all = q.shape[0]
    n = t_all // t
    assert t % CHUNK == 0
    qi = np.concatenate([np.full(i + 1, i, np.int32) for i in range(n)])
    kj = np.concatenate([np.arange(i + 1, dtype=np.int32) for i in range(n)])
    n_pairs = len(qi)

    def scored(p):
        return jnp.minimum(p, n_pairs - 1)

    def consumed(p):
        return jnp.maximum(p - 1, 0)

    grid_spec = pltpu.PrefetchScalarGridSpec(
        num_scalar_prefetch=2,
        grid=(HEADS, n_pairs + 1),
        in_specs=[
            pl.BlockSpec((1, 1, LANES), lambda h, p, qi_r, kj_r: (h, 0, 0)),
            pl.BlockSpec((4, MAP_DIM), lambda h, p, qi_r, kj_r: (0, 0)),
            pl.BlockSpec((1, HEAD_DIM), lambda h, p, qi_r, kj_r: (0, 0)),
            pl.BlockSpec((1, 2, t, HEAD_DIM), lambda h, p, qi_r, kj_r: (h, 0, 0, 0)),
            pl.BlockSpec((t, HEAD_DIM), lambda h, p, qi_r, kj_r: (qi_r[scored(p)], h)),
            pl.BlockSpec((t, HEAD_DIM), lambda h, p, qi_r, kj_r: (kj_r[scored(p)], h)),
            pl.BlockSpec((HEAD_ROWS_T, t), lambda h, p, qi_r, kj_r: (h, kj_r[consumed(p)])),
        ],
        out_specs=pl.BlockSpec((t, HEAD_DIM), lambda h, p, qi_r, kj_r: (qi_r[consumed(p)], h)),
        scratch_shapes=[
            pltpu.VMEM((2, t, t), F32), pltpu.VMEM((2, t, t), F32),
            pltpu.VMEM((2, 1, t), F32), pltpu.VMEM((2, HEAD_ROWS_T, t), F32),
        ],
    )
    return pl.pallas_call(
        functools.partial(_attn_prompt_kernel, t=t, n_pairs=n_pairs, lam_init=lam_init),
        grid_spec=grid_spec,
        out_shape=jax.ShapeDtypeStruct((t_all, D_MODEL), BF16),
        compiler_params=_params("arbitrary", "arbitrary"),
        name="diff_attention_prompt",
    )(jnp.asarray(qi), jnp.asarray(kj), slopes, lam_p, sub_gain.reshape(1, HEAD_DIM), key_cols, q, k, v_t)


def _attn_decode_kernel(slope_ref, lam_ref, sg_ref, q_ref, k_ref, v_ref, kn_ref, vn_ref, o_ref, m_ref, l_ref, acc_ref,
                        *, tk, pos0, lam_init):
    j = pl.program_id(1)
    nk = pl.num_programs(1)
    n = q_ref.shape[1]

    @pl.when(j == 0)
    def _():
        m_ref[...] = jnp.full_like(m_ref, NEG)
        l_ref[...] = jnp.zeros_like(l_ref)
        acc_ref[...] = jnp.zeros_like(acc_ref)

    row = lax.broadcasted_iota(jnp.int32, (2 * n, 1), 0)
    q_pos = pos0 + jnp.where(row < n, row, row - n)
    lane = lax.broadcasted_iota(jnp.int32, (1, HEAD_DIM), 1)
    map_of_row_has_lane = (row < n) == (lane < MAP_DIM)

    def update(h, kb, vb, k0, n_keys):
        hs = slice(h * HEAD_DIM, (h + 1) * HEAD_DIM)
        slope = slope_ref[h, :, 0:1] * LOG2E
        q = q_ref[0, :, hs]
        q2 = jnp.concatenate([q, q], axis=0)
        q_stack = jnp.where(map_of_row_has_lane, q2, jnp.zeros_like(q2))
        k_pos = k0 + lax.broadcasted_iota(jnp.int32, (1, n_keys), 1)
        allowed = (k_pos >> CHUNK_SHIFT) <= (q_pos >> CHUNK_SHIFT)
        bias = -slope * jnp.abs(q_pos - k_pos).astype(F32)
        s = jnp.where(allowed, _dot_nt(q_stack, kb.astype(BF16)) + bias, NEG)
        m_old = m_ref[h]
        m_new = jnp.maximum(m_old, jnp.max(s, axis=-1, keepdims=True))
        alpha = jnp.exp2(m_old - m_new)
        p = jnp.exp2(s - m_new)
        l_ref[h] = alpha * l_ref[h] + jnp.sum(p, axis=-1, keepdims=True)
        acc_ref[h] = alpha * acc_ref[h] + _dot(p.astype(BF16), vb.astype(BF16))
        m_ref[h] = m_new

    for h in range(HEADS):
        head_rows = pl.ds(h, tk, stride=HEADS)
        update(h, k_ref[0, head_rows, :], v_ref[0, head_rows, :], j * tk, tk)

    @pl.when(j == nk - 1)
    def _():
        lam = _lambda(lam_ref, lam_init)
        for h in range(HEADS):
            hs = slice(h * HEAD_DIM, (h + 1) * HEAD_DIM)
            update(h, kn_ref[0, :, hs], vn_ref[0, :, hs], pos0, n)
            o = acc_ref[h] / l_ref[h]
            o = o[0:n] - lam * o[n:2 * n]
            o_ref[0, :, hs] = (_rms(o, sg_ref[...]) * (1.0 - lam_init)).astype(o_ref.dtype)


def diff_attention_decode(q, k, v, k_new, v_new, slopes, lam_p, sub_gain, *, tk, lam_init):
    b, n, _ = q.shape
    t_k = k.shape[1]
    assert t_k % tk == 0 and t_k % CHUNK == 0
    k = k.reshape(b, t_k * HEADS, HEAD_DIM)
    v = v.reshape(b, t_k * HEADS, HEAD_DIM)
    row_block = pl.BlockSpec((1, n, D_MODEL), lambda bi, j: (bi, 0, 0))
    kv_spec = pl.BlockSpec((1, tk * HEADS, HEAD_DIM), lambda bi, j: (bi, j, 0))
    return pl.pallas_call(
        functools.partial(_attn_decode_kernel, tk=tk, pos0=t_k, lam_init=lam_init),
        grid=(b, t_k // tk),
        in_specs=[
            pl.BlockSpec((HEADS, 1, LANES), lambda bi, j: (0, 0, 0)),
            pl.BlockSpec((4, MAP_DIM), lambda bi, j: (0, 0)),
            pl.BlockSpec((1, HEAD_DIM), lambda bi, j: (0, 0)),
            row_block, kv_spec, kv_spec, row_block, row_block,
        ],
        out_specs=row_block,
        out_shape=jax.ShapeDtypeStruct((b, n, D_MODEL), BF16),
        scratch_shapes=[
            pltpu.VMEM((HEADS, 2 * n, 1), F32),
            pltpu.VMEM((HEADS, 2 * n, 1), F32),
            pltpu.VMEM((HEADS, 2 * n, HEAD_DIM), F32),
        ],
        compiler_params=_params("arbitrary", "arbitrary"),
        name="diff_attention_decode",
    )(slopes, lam_p, sub_gain.reshape(1, HEAD_DIM), q, k, v, k_new, v_new)


def _trunk(x, conv_state, ssm_state, past_k, past_v, wts, *, tm, gdn_chunk, gdn_group, attn_tile):
    b, t, _ = x.shape
    m = b * t
    xf = x.reshape(m, D_MODEL)
    new_conv, new_ssm = [], []
    k_new = v_new = k_bf = v_t = None
    for l in range(DEPTH):
        gains = wts["norm_gains"][l]
        if l < N_A:
            proj = norm_matmul(xf, gains[0], wts["a_w_in"][l], tm=tm, tn=A_IN_PAD // 3).reshape(b, t, A_IN_PAD)
            w, uu, qg, kd, qkd, egl, conv_l = gdn_prep(proj, conv_state[l], wts["a_conv_w"][l], wts["gate_par"][l],
                                                       C=gdn_chunk, G=gdn_group)
            og, s_l = gdn_scan(w, uu, qg, kd, qkd, egl, proj, ssm_state[l], wts["a_o_gain"][l],
                               C=gdn_chunk, G=gdn_group)
            new_conv.append(conv_l)
            new_ssm.append(s_l)
            xf = matmul_norm_res(og.reshape(m, D_MODEL), wts["a_w_out"][l], gains[1], xf, tm=tm)
        else:
            j = l - N_A
            prompt = past_k is None
            if j == 0:
                k_new, k_bf = norm_matmul(xf, wts["kv_gain"], wts["w_k"], tm=tm, tn=D_MODEL, out_dtypes=(F32, BF16))
                v_new = norm_matmul(xf, wts["kv_gain"], wts["w_v"], tm=tm, tn=D_MODEL, head_transposed_copy=prompt)
                if prompt:
                    v_new, v_t = v_new
            q = norm_matmul(xf, gains[0], wts["b_w_q"][j], tm=tm, tn=D_MODEL, out_dtypes=(BF16,), out_scale=Q_SCALE)
            lam_init = 0.8 - 0.6 * math.exp(-0.3 * l)
            tail = (wts["b_lam"][j], wts["b_sub_gain"][j])
            if prompt:
                assert b == 1
                o = diff_attention_prompt(q, k_bf, v_t, wts["slopes"], alibi_key_columns(attn_tile),
                                          *tail, t=attn_tile, lam_init=lam_init)
            else:
                o = diff_attention_decode(q.reshape(b, t, D_MODEL), past_k, past_v, k_new.reshape(b, t, D_MODEL),
                                          v_new.reshape(b, t, D_MODEL), wts["slopes"], *tail, tk=attn_tile,
                                          lam_init=lam_init)
            xf = matmul_norm_res(o.reshape(m, D_MODEL), wts["b_w_out"][j], gains[1], xf, tm=tm)
        xf = mlp_block(xf, gains[2], wts["mlp_w1"][l], wts["mlp_w2"][l], gains[3], tm=tm, tf=512)
    kv_shape = (b, t, HEADS, HEAD_DIM)
    return (xf.reshape(b, t, D_MODEL), jnp.stack(new_conv), jnp.stack(new_ssm),
            k_new.reshape(kv_shape), v_new.reshape(kv_shape))


def kernel(x_prompt, x_sample, state_conv, state_ssm, cache_k, cache_v, norm_gains, a_w_in, a_conv_w, a_log,
           a_dt_bias, a_o_gain, a_w_out, kv_gain, w_kv, b_w_q, b_lam, b_sub_gain, b_w_out, mlp_w1, mlp_w2):
    a_in = a_w_in.shape[-1]
    gate_par = jnp.zeros((N_A, 2, LANES), F32)
    gate_par = gate_par.at[:, 0, HEADS:2 * HEADS].set(a_log.astype(F32))
    gate_par = gate_par.at[:, 1, HEADS:2 * HEADS].set(a_dt_bias.astype(F32))
    slopes = jnp.asarray(alibi_slopes())
    wts = dict(
        norm_gains=norm_gains.astype(F32),
        a_w_in=jnp.pad(a_w_in, ((0, 0), (0, 0), (0, A_IN_PAD - a_in))).astype(BF16),
        a_conv_w=a_conv_w.astype(F32),
        gate_par=gate_par,
        a_o_gain=a_o_gain.astype(F32),
        a_w_out=a_w_out.astype(BF16),
        kv_gain=kv_gain.astype(F32),
        w_k=w_kv[:, :D_MODEL].astype(BF16),
        w_v=w_kv[:, D_MODEL:].astype(BF16),
        b_w_q=b_w_q.astype(BF16),
        b_lam=b_lam.astype(F32),
        b_sub_gain=b_sub_gain.astype(F32),
        b_w_out=b_w_out.astype(BF16),
        mlp_w1=mlp_w1.astype(BF16),
        mlp_w2=mlp_w2.astype(BF16),
        slopes=jnp.broadcast_to(slopes[:, None, None], (HEADS, 1, LANES)),
    )
    bp, tp, _ = x_prompt.shape
    conv0 = jnp.zeros((N_A, bp, CONV_W - 1, A_QKV), F32)
    ssm0 = jnp.zeros((N_A, bp, HEADS, HEAD_DIM, HEAD_DIM), F32)
    y_p, p_conv, p_ssm, p_k, p_v = _trunk(x_prompt, conv0, ssm0, None, None, wts,
                                          tm=512, gdn_chunk=CHUNK, gdn_group=4, attn_tile=512)
    bs, ts, _ = x_sample.shape
    y_s, s_conv, s_ssm, s_k, s_v = _trunk(x_sample, state_conv, state_ssm, cache_k, cache_v,
                                          wts, tm=bs * ts, gdn_chunk=ts, gdn_group=1, attn_tile=512)
    return (y_p, y_s, p_conv, p_ssm, p_k, p_v, s_conv, s_ssm, s_k, s_v)
```

```python
import functools
import math

import jax
import jax.numpy as jnp
import numpy as np
from jax import lax
from jax.experimental import pallas as pl
from jax.experimental.pallas import tpu as pltpu

F32 = jnp.float32
BF16 = jnp.bfloat16

D_MODEL = 1024
DEPTH = 4
N_A = DEPTH // 2
CHUNK = 64
CHUNK_SHIFT = CHUNK.bit_length() - 1
assert 1 << CHUNK_SHIFT == CHUNK
HEADS = 8
HEAD_DIM = D_MODEL // HEADS
CONV_W = 4
A_QKV = 3 * D_MODEL
A_GATE_COL = A_QKV + D_MODEL
LANES = 128
A_IN_PAD = A_GATE_COL + LANES
MAP_DIM = HEAD_DIM // 2
D_FF = 4 * D_MODEL
EPS = 1e-6
NEG = -1e30
VMEM_LIMIT = 48 * 1024 * 1024
HIGHEST = lax.Precision.HIGHEST
ONES_ROWS = 16
HEAD_ROWS_T = HEAD_DIM + ONES_ROWS
LOG2E = math.log2(math.e)
Q_SCALE = MAP_DIM ** -0.5 * LOG2E


def _dot(a, b, precision=None):
    return jnp.dot(a, b, preferred_element_type=F32, precision=precision)


def _dot_nt(a, b):
    return lax.dot_general(a, b, (((1,), (1,)), ((), ())), preferred_element_type=F32)


def _dot_tn(a, b):
    return lax.dot_general(a, b, (((0,), (0,)), ((), ())), preferred_element_type=F32)


def _split(a):
    hi = a.astype(BF16)
    return hi, (a - hi.astype(F32)).astype(BF16)


def _dot_split(a, b):
    return _dot(a[0], b[0]) + (_dot(a[0], b[1]) + _dot(a[1], b[0]))


def _rms(x, gain):
    return x * lax.rsqrt(jnp.mean(x * x, axis=-1, keepdims=True) + EPS) * gain


def _sigmoid(x):
    return 1.0 / (1.0 + jnp.exp(-x))


def _params(*semantics):
    return pltpu.CompilerParams(dimension_semantics=semantics, vmem_limit_bytes=VMEM_LIMIT)


def _norm_matmul_kernel(x_ref, g_ref, w_ref, *out_and_scratch, out_scale, head_transposed_copy):
    *o_refs, hn_ref = out_and_scratch

    @pl.when(pl.program_id(1) == 0)
    def _():
        hn_ref[...] = _rms(x_ref[...], g_ref[...]).astype(BF16)

    y = _dot(hn_ref[...], w_ref[...])
    if out_scale != 1.0:
        y = y * out_scale
    if head_transposed_copy:
        *o_refs, ot_ref = o_refs
        y_t = y.T.astype(ot_ref.dtype)
        for h in range(y_t.shape[0] // HEAD_DIM):
            r0 = h * HEAD_ROWS_T
            ot_ref[r0:r0 + HEAD_DIM, :] = y_t[h * HEAD_DIM:(h + 1) * HEAD_DIM]
            ot_ref[r0 + HEAD_DIM:r0 + HEAD_ROWS_T, :] = jnp.ones((ONES_ROWS, y_t.shape[1]), ot_ref.dtype)
    for o_ref in o_refs:
        o_ref[...] = y.astype(o_ref.dtype)


def norm_matmul(x, gain, w, *, tm, tn, out_dtypes=(F32,), out_scale=1.0, head_transposed_copy=False):
    m, k = x.shape
    n = w.shape[1]
    out_specs = [pl.BlockSpec((tm, tn), lambda i, j: (i, j)) for _ in out_dtypes]
    out_shape = [jax.ShapeDtypeStruct((m, n), dt) for dt in out_dtypes]
    if head_transposed_copy:
        out_specs.append(pl.BlockSpec((tn // HEAD_DIM * HEAD_ROWS_T, tm), lambda i, j: (j, i)))
        out_shape.append(jax.ShapeDtypeStruct((n // HEAD_DIM * HEAD_ROWS_T, m), BF16))
    outs = pl.pallas_call(
        functools.partial(_norm_matmul_kernel, out_scale=out_scale, head_transposed_copy=head_transposed_copy),
        grid=(m // tm, n // tn),
        in_specs=[
            pl.BlockSpec((tm, k), lambda i, j: (i, 0)),
            pl.BlockSpec((1, k), lambda i, j: (0, 0)),
            pl.BlockSpec((k, tn), lambda i, j: (0, j)),
        ],
        out_specs=out_specs,
        out_shape=out_shape,
        scratch_shapes=[pltpu.VMEM((tm, k), BF16)],
        compiler_params=_params("parallel", "arbitrary"),
        name="norm_matmul",
    )(x, gain.reshape(1, k), w)
    return outs[0] if len(outs) == 1 else outs


def _matmul_norm_res_kernel(a_ref, w_ref, g_ref, x_ref, o_ref):
    y = _dot(a_ref[...], w_ref[...])
    o_ref[...] = x_ref[...] + _rms(y, g_ref[...])


def matmul_norm_res(a, w, gain, x, *, tm):
    m, k = a.shape
    d = w.shape[1]
    return pl.pallas_call(
        _matmul_norm_res_kernel,
        grid=(m // tm,),
        in_specs=[
            pl.BlockSpec((tm, k), lambda i: (i, 0)),
            pl.BlockSpec((k, d), lambda i: (0, 0)),
            pl.BlockSpec((1, d), lambda i: (0, 0)),
            pl.BlockSpec((tm, d), lambda i: (i, 0)),
        ],
        out_specs=pl.BlockSpec((tm, d), lambda i: (i, 0)),
        out_shape=jax.ShapeDtypeStruct((m, d), F32),
        compiler_params=_params("parallel"),
        name="matmul_norm_res",
    )(a, w, gain.reshape(1, d), x)


def _mlp_kernel(x_ref, g_in_ref, w1_ref, w2_ref, g_out_ref, o_ref, hn_ref, acc_ref):
    f = pl.program_id(1)

    @pl.when(f == 0)
    def _():
        hn_ref[...] = _rms(x_ref[...], g_in_ref[...]).astype(BF16)
        acc_ref[...] = jnp.zeros_like(acc_ref)

    h = _dot(hn_ref[...], w1_ref[...])
    h = jnp.square(jnp.maximum(h, 0.0)).astype(BF16)
    acc_ref[...] += _dot(h, w2_ref[...])

    @pl.when(f == pl.num_programs(1) - 1)
    def _():
        o_ref[...] = x_ref[...] + _rms(acc_ref[...], g_out_ref[...])


def mlp_block(x, g_in, w1, w2, g_out, *, tm, tf):
    m, d = x.shape
    ff = w1.shape[1]
    return pl.pallas_call(
        _mlp_kernel,
        grid=(m // tm, ff // tf),
        in_specs=[
            pl.BlockSpec((tm, d), lambda i, f: (i, 0)),
            pl.BlockSpec((1, d), lambda i, f: (0, 0)),
            pl.BlockSpec((d, tf), lambda i, f: (0, f)),
            pl.BlockSpec((tf, d), lambda i, f: (f, 0)),
            pl.BlockSpec((1, d), lambda i, f: (0, 0)),
        ],
        out_specs=pl.BlockSpec((tm, d), lambda i, f: (i, 0)),
        out_shape=jax.ShapeDtypeStruct((m, d), F32),
        scratch_shapes=[pltpu.VMEM((tm, d), BF16), pltpu.VMEM((tm, d), F32)],
        compiler_params=_params("parallel", "arbitrary"),
        name="mlp_block",
    )(x, g_in.reshape(1, d), w1, w2, g_out.reshape(1, d))


CONV_ROW0 = 8


def _gdn_prep_kernel(proj_ref, conv0_ref, cw_ref, gp_ref,
                     w_ref, uu_ref, qg_ref, kd_ref, qkd_ref, egl_ref, conv_ref, xbuf_ref, *, C, G):
    R = C * G
    prev0 = CONV_ROW0 - (CONV_W - 1)

    @pl.when(pl.program_id(1) == 0)
    def _():
        xbuf_ref[prev0:CONV_ROW0, :] = conv0_ref[0]

    xbuf_ref[CONV_ROW0:CONV_ROW0 + R, :] = proj_ref[0, :, 0:A_QKV]
    cw = cw_ref[...]
    y = xbuf_ref[prev0:prev0 + R, :] * cw[0:1, :]
    for j in range(1, CONV_W):
        y = y + xbuf_ref[prev0 + j:prev0 + j + R, :] * cw[j:j + 1, :]
    y = y * _sigmoid(y)
    last = xbuf_ref[prev0 + R:CONV_ROW0 + R, :]
    xbuf_ref[prev0:CONV_ROW0, :] = last
    conv_ref[0] = last

    tail = proj_ref[0, :, A_GATE_COL:A_IN_PAD]
    gp = gp_ref[...]
    beta = _sigmoid(tail)
    t = tail + gp[1:2, :]
    softplus = jnp.maximum(t, 0.0) + jnp.log(1.0 + jnp.exp(-jnp.abs(t)))
    g = -jnp.exp(gp[0:1, :]) * softplus

    rr = lax.broadcasted_iota(jnp.int32, (R, R), 0)
    cc = lax.broadcasted_iota(jnp.int32, (R, R), 1)
    shift = C.bit_length() - 1
    same = (rr >> shift) == (cc >> shift)
    gc = _dot(jnp.where(same & (rr >= cc), 1.0, 0.0), g, HIGHEST)
    gl = _dot(jnp.where(same, 1.0, 0.0), g, HIGHEST)
    pad_rows = -R % LANES
    gc_t = (jnp.concatenate([gc, jnp.zeros((pad_rows, LANES), F32)], axis=0) if pad_rows else gc).T
    eg = jnp.exp(gc)
    ekd = jnp.exp(gl - gc)
    egl = jnp.exp(gl)
    for gi in range(G):
        egl_ref[0, gi] = egl[gi * C:gi * C + 1, :]

    row = lax.broadcasted_iota(jnp.int32, (C, C), 0)
    col = lax.broadcasted_iota(jnp.int32, (C, C), 1)
    incl = row >= col
    strict = row > col
    eye = (row == col).astype(F32)

    units = [(gi, h) for gi in range(G) for h in range(HEADS)]
    qn, kn = [], []
    for h in range(HEADS):
        q = y[:, h * HEAD_DIM:(h + 1) * HEAD_DIM]
        k = y[:, D_MODEL + h * HEAD_DIM:D_MODEL + (h + 1) * HEAD_DIM]
        qn.append(q * lax.rsqrt(jnp.sum(q * q, axis=-1, keepdims=True) + EPS) * (HEAD_DIM ** -0.5))
        kn.append(k * lax.rsqrt(jnp.sum(k * k, axis=-1, keepdims=True) + EPS))

    def rows(gi):
        return slice(gi * C, (gi + 1) * C)

    decay, qk, a_mat = {}, {}, {}
    for gi, h in units:
        rs, gl_h = rows(gi), HEADS + h
        g_col = gc[rs, gl_h:gl_h + 1]
        g_row = gc_t[gl_h:gl_h + 1, gi * C:(gi + 1) * C]
        decay[gi, h] = jnp.where(incl, jnp.exp(jnp.minimum(g_col - g_row, 0.0)), 0.0)
        k_bf = kn[h][rs].astype(BF16)
        qk_kk = _dot_nt(jnp.concatenate([qn[h][rs].astype(BF16), k_bf], axis=0), k_bf)
        qk[gi, h] = qk_kk[0:C]
        a_mat[gi, h] = jnp.where(strict, beta[rs, h:h + 1] * qk_kk[C:2 * C] * decay[gi, h], 0.0)
    t_inv = {u: eye - a_mat[u] for u in units}
    pw = {}
    for u in units:
        a_s = _split(a_mat[u])
        pw[u] = _dot_split(a_s, a_s)
    n_double = C.bit_length() - 2
    for step in range(n_double):
        for u in units:
            p_s = _split(pw[u])
            if step + 1 < n_double:
                tp = _dot_split(_split(jnp.concatenate([t_inv[u], pw[u]], axis=0)), p_s)
                t_inv[u] = t_inv[u] + tp[0:C]
                pw[u] = tp[C:2 * C]
            else:
                t_inv[u] = t_inv[u] + _dot_split(_split(t_inv[u]), p_s)
    for gi, h in units:
        rs, gl_h = rows(gi), HEADS + h
        hs = slice(h * HEAD_DIM, (h + 1) * HEAD_DIM)
        b_col = beta[rs, h:h + 1]
        eg_col = eg[rs, gl_h:gl_h + 1]
        k = kn[h][rs]
        v = y[rs, 2 * D_MODEL + h * HEAD_DIM:2 * D_MODEL + (h + 1) * HEAD_DIM]
        rhs = jnp.concatenate([b_col * v, (b_col * eg_col) * k], axis=-1).astype(BF16)
        sol = _dot(t_inv[gi, h].astype(BF16), rhs)
        uu_ref[0, rs, hs] = sol[:, 0:HEAD_DIM].astype(uu_ref.dtype)
        w_ref[0, rs, hs] = sol[:, HEAD_DIM:].astype(w_ref.dtype)
        qg_ref[0, rs, hs] = (eg_col * qn[h][rs]).astype(qg_ref.dtype)
        kd_ref[0, rs, hs] = (ekd[rs, gl_h:gl_h + 1] * k).astype(kd_ref.dtype)
        qkd_ref[0, rs, h * C:(h + 1) * C] = (qk[gi, h] * decay[gi, h]).astype(qkd_ref.dtype)


def gdn_prep(proj, conv0, conv_w, gate_par, *, C, G):
    b, t, _ = proj.shape
    r = C * G
    nc = t // C
    row_spec = pl.BlockSpec((1, r, D_MODEL), lambda i, c: (i, c, 0))
    conv_spec = pl.BlockSpec((1, CONV_W - 1, A_QKV), lambda i, c: (i, 0, 0))
    return pl.pallas_call(
        functools.partial(_gdn_prep_kernel, C=C, G=G),
        grid=(b, t // r),
        in_specs=[
            pl.BlockSpec((1, r, A_IN_PAD), lambda i, c: (i, c, 0)),
            conv_spec,
            pl.BlockSpec((CONV_W, A_QKV), lambda i, c: (0, 0)),
            pl.BlockSpec((2, LANES), lambda i, c: (0, 0)),
        ],
        out_specs=[
            row_spec, row_spec, row_spec, row_spec,
            pl.BlockSpec((1, r, HEADS * C), lambda i, c: (i, c, 0)),
            pl.BlockSpec((1, G, 1, LANES), lambda i, c: (i, c, 0, 0)),
            conv_spec,
        ],
        out_shape=[jax.ShapeDtypeStruct((b, t, D_MODEL), BF16)] * 4 + [
            jax.ShapeDtypeStruct((b, t, HEADS * C), BF16),
            jax.ShapeDtypeStruct((b, nc, 1, LANES), F32),
            jax.ShapeDtypeStruct((b, CONV_W - 1, A_QKV), F32),
        ],
        scratch_shapes=[pltpu.VMEM((CONV_ROW0 + r, A_QKV), F32)],
        compiler_params=_params("arbitrary", "arbitrary"),
        name="gdn_prep",
    )(proj, conv0, conv_w, gate_par)


def _gdn_scan_kernel(w_ref, uu_ref, qg_ref, kd_ref, qkd_ref, egl_ref, z_ref, s0_ref, og_ref, o_ref, s_ref, *, C, G):
    @pl.when(pl.program_id(1) == 0)
    def _():
        s_ref[0] = s0_ref[0]

    heads = range(HEADS)
    for gi in range(G):
        rs = slice(gi * C, (gi + 1) * C)
        egl = egl_ref[0, gi]

        def hs(h):
            return slice(h * HEAD_DIM, (h + 1) * HEAD_DIM)

        s_old = [s_ref[0, h] for h in heads]
        s_bf = [s.astype(BF16) for s in s_old]
        wq = [_dot(jnp.concatenate([w_ref[0, rs, hs(h)], qg_ref[0, rs, hs(h)]], axis=0), s_bf[h]) for h in heads]
        u_bf = [(uu_ref[0, rs, hs(h)].astype(F32) - wq[h][0:C]).astype(BF16) for h in heads]
        o = [wq[h][C:2 * C] + _dot(qkd_ref[0, rs, h * C:(h + 1) * C], u_bf[h]) for h in heads]
        for h in heads:
            s_ref[0, h] = egl[:, HEADS + h:HEADS + h + 1] * s_old[h] + _dot_tn(kd_ref[0, rs, hs(h)], u_bf[h])
        for h in heads:
            z = z_ref[0, rs, hs(h)]
            o_ref[0, rs, hs(h)] = (_rms(o[h], og_ref[...]) * (z * _sigmoid(z))).astype(o_ref.dtype)


def gdn_scan(w, uu, qg, kd, qkd, egl, proj, s0, o_gain, *, C, G):
    b, t, _ = w.shape
    r = C * G
    row_spec = pl.BlockSpec((1, r, D_MODEL), lambda i, c: (i, c, 0))
    state_spec = pl.BlockSpec((1, HEADS, HEAD_DIM, HEAD_DIM), lambda i, c: (i, 0, 0, 0))
    return pl.pallas_call(
        functools.partial(_gdn_scan_kernel, C=C, G=G),
        grid=(b, t // r),
        in_specs=[
            row_spec, row_spec, row_spec, row_spec,
            pl.BlockSpec((1, r, HEADS * C), lambda i, c: (i, c, 0)),
            pl.BlockSpec((1, G, 1, LANES), lambda i, c: (i, c, 0, 0)),
            pl.BlockSpec((1, r, D_MODEL), lambda i, c: (i, c, A_QKV // D_MODEL)),
            state_spec,
            pl.BlockSpec((1, HEAD_DIM), lambda i, c: (0, 0)),
        ],
        out_specs=[row_spec, state_spec],
        out_shape=[
            jax.ShapeDtypeStruct((b, t, D_MODEL), BF16),
            jax.ShapeDtypeStruct((b, HEADS, HEAD_DIM, HEAD_DIM), F32),
        ],
        compiler_params=_params("arbitrary", "arbitrary"),
        name="gdn_scan",
    )(w, uu, qg, kd, qkd, egl, proj, s0, o_gain.reshape(1, HEAD_DIM))


def _lambda(lam_ref, lam_init):
    lp = lam_ref[...]
    return (jnp.exp(jnp.sum(lp[0:1] * lp[1:2], axis=-1, keepdims=True))
            - jnp.exp(jnp.sum(lp[2:3] * lp[3:4], axis=-1, keepdims=True)) + lam_init)


KEY_COLS = 3


def _attn_prompt_kernel(qi_ref, kj_ref, slope_ref, lam_ref, sg_ref, kx_ref, q_ref, k_ref, vt_ref, o_ref,
                        s_even_ref, s_odd_ref, m_ref, acc_ref, *, tq, tk, n_pairs, lam_init):
    step = pl.program_id(1)
    slope = slope_ref[0, :, 0:1] * LOG2E
    lane = lax.broadcasted_iota(jnp.int32, (1, HEAD_DIM), 1)
    map_lanes = [lane < MAP_DIM, lane >= MAP_DIM]

    def score_stage(s_ref):
        q = q_ref[...]
        k = k_ref[...]
        for mp in range(2):
            a0 = MAP_DIM if mp == 0 else 0
            extra = (lane >= a0) & (lane < a0 + KEY_COLS)
            qa = jnp.where(map_lanes[mp], q, jnp.where(extra, 1.0, 0.0).astype(BF16))
            ka = jnp.where(map_lanes[mp], k, kx_ref[0, mp])
            s_ref[mp] = _dot_nt(ka, qa)

    def mask_diagonal(s_ref, key_offset):
        j = lax.broadcasted_iota(jnp.int32, (tk, 1), 0)
        i = lax.broadcasted_iota(jnp.int32, (1, tk), 1)
        allowed = (j >> CHUNK_SHIFT) <= (i >> CHUNK_SHIFT)
        fix = jnp.where(j > i, (2.0 * slope) * (i - j).astype(F32), 0.0)
        facing = slice(key_offset, key_offset + tk)
        for mp in range(2):
            if key_offset:
                s_ref[mp, :, 0:key_offset] = jnp.full((tk, key_offset), NEG, F32)
            s_ref[mp, :, facing] = jnp.where(allowed, s_ref[mp, :, facing] + fix, NEG)

    def softmax_stage(s_ref, shift_const):
        vt_aug = vt_ref[...]
        probs, alphas = [], []
        for mp in range(2):
            scores = s_ref[mp]
            m_old = m_ref[mp]
            m_new = jnp.maximum(m_old, jnp.max(scores, axis=0, keepdims=True) + shift_const)
            probs.append(jnp.exp2(scores - (m_new - shift_const)).astype(BF16))
            alphas.append(jnp.exp2(m_old - m_new))
            m_ref[mp] = m_new
        pv = [_dot(vt_aug, probs[mp]) for mp in range(2)]
        for mp in range(2):
            acc_ref[mp] = alphas[mp] * acc_ref[mp] + pv[mp]

    def finalize():
        lam = _lambda(lam_ref, lam_init)
        a0 = acc_ref[0]
        a1 = acc_ref[1]
        o_t = (a0[0:HEAD_DIM] / a0[HEAD_DIM:HEAD_DIM + 1]
               - lam * (a1[0:HEAD_DIM] / a1[HEAD_DIM:HEAD_DIM + 1]))
        o_t = o_t * (lax.rsqrt(jnp.mean(o_t * o_t, axis=0, keepdims=True) + EPS) * (1.0 - lam_init))
        o_ref[...] = (o_t.T * sg_ref[...]).astype(o_ref.dtype)

    scored = jnp.minimum(step, n_pairs - 1)
    key_offset_s = kj_ref[scored] * tk - qi_ref[scored] * tq
    consumed = jnp.maximum(step - 1, 0)
    kj_c = kj_ref[consumed]
    key_offset_c = kj_c * tk - qi_ref[consumed] * tq

    @pl.when(step == 0)
    def _():
        score_stage(s_even_ref)

    @pl.when((step > 0) & (kj_c == 0))
    def _():
        m_ref[...] = jnp.full_like(m_ref, NEG)
        acc_ref[...] = jnp.zeros_like(acc_ref)

    tile_shift = slope * key_offset_c.astype(F32)
    for parity, (s_write, s_read) in enumerate([(s_even_ref, s_odd_ref), (s_odd_ref, s_even_ref)]):
        @pl.when((step > 0) & (step % 2 == parity))
        def _():
            score_stage(s_write)
            softmax_stage(s_read, tile_shift)

    for parity, s_write in enumerate([s_even_ref, s_odd_ref]):
        for key_offset in range(0, tq, tk):
            @pl.when((key_offset_s == key_offset) & (step % 2 == parity))
            def _():
                mask_diagonal(s_write, key_offset)

    @pl.when((step > 0) & (key_offset_c == tq - tk))
    def _():
        finalize()


def alibi_slopes():
    return 2.0 ** (-8.0 * np.arange(1, HEADS + 1, dtype=np.float32) / HEADS)


def alibi_key_columns(t):
    term = (np.float32(LOG2E) * alibi_slopes())[:, None] * np.arange(t, dtype=np.float32)[None, :]
    cols = np.zeros((HEADS, 2, t, HEAD_DIM), np.float32)
    for c in range(KEY_COLS):
        piece = term.astype(BF16).astype(np.float32)
        term = term - piece
        for mp in range(2):
            a0 = MAP_DIM if mp == 0 else 0
            cols[:, mp, :, a0 + c] = piece
    return jnp.asarray(cols.astype(BF16))


def diff_attention_prompt(q, k, v_t, slopes, key_cols, lam_p, sub_gain, *, tq, tk, lam_init):
    t_all = q.shape[0]
    assert tq % tk == 0 and tk % CHUNK == 0 and t_all % tq == 0
    per_q = tq // tk
    qi = np.concatenate([np.full((i + 1) * per_q, i, np.int32) for i in range(t_all // tq)])
    kj = np.concatenate([np.arange((i + 1) * per_q, dtype=np.int32) for i in range(t_all // tq)])
    n_pairs = len(qi)

    def scored(p):
        return jnp.minimum(p, n_pairs - 1)

    def consumed(p):
        return jnp.maximum(p - 1, 0)

    grid_spec = pltpu.PrefetchScalarGridSpec(
        num_scalar_prefetch=2,
        grid=(HEADS, n_pairs + 1),
        in_specs=[
            pl.BlockSpec((1, 1, LANES), lambda h, p, qi_r, kj_r: (h, 0, 0)),
            pl.BlockSpec((4, MAP_DIM), lambda h, p, qi_r, kj_r: (0, 0)),
            pl.BlockSpec((1, HEAD_DIM), lambda h, p, qi_r, kj_r: (0, 0)),
            pl.BlockSpec((1, 2, tk, HEAD_DIM), lambda h, p, qi_r, kj_r: (h, 0, 0, 0)),
            pl.BlockSpec((tq, HEAD_DIM), lambda h, p, qi_r, kj_r: (qi_r[scored(p)], h)),
            pl.BlockSpec((tk, HEAD_DIM), lambda h, p, qi_r, kj_r: (kj_r[scored(p)], h)),
            pl.BlockSpec((HEAD_ROWS_T, tk), lambda h, p, qi_r, kj_r: (h, kj_r[consumed(p)])),
        ],
        out_specs=pl.BlockSpec((tq, HEAD_DIM), lambda h, p, qi_r, kj_r: (qi_r[consumed(p)], h)),
        scratch_shapes=[
            pltpu.VMEM((2, tk, tq), F32), pltpu.VMEM((2, tk, tq), F32),
            pltpu.VMEM((2, 1, tq), F32), pltpu.VMEM((2, HEAD_ROWS_T, tq), F32),
        ],
    )
    return pl.pallas_call(
        functools.partial(_attn_prompt_kernel, tq=tq, tk=tk, n_pairs=n_pairs, lam_init=lam_init),
        grid_spec=grid_spec,
        out_shape=jax.ShapeDtypeStruct((t_all, D_MODEL), BF16),
        compiler_params=_params("arbitrary", "arbitrary"),
        name="diff_attention_prompt",
    )(jnp.asarray(qi), jnp.asarray(kj), slopes, lam_p, sub_gain.reshape(1, HEAD_DIM), key_cols, q, k, v_t)


def _attn_decode_kernel(slope_ref, lam_ref, sg_ref, q_ref, k_ref, v_ref, kn_ref, vn_ref, o_ref, m_ref, l_ref, acc_ref,
                        *, tk, pos0, lam_init):
    j = pl.program_id(1)
    nk = pl.num_programs(1)
    n = q_ref.shape[1]

    @pl.when(j == 0)
    def _():
        m_ref[...] = jnp.full_like(m_ref, NEG)
        l_ref[...] = jnp.zeros_like(l_ref)
        acc_ref[...] = jnp.zeros_like(acc_ref)

    row = lax.broadcasted_iota(jnp.int32, (2 * n, 1), 0)
    q_pos = pos0 + jnp.where(row < n, row, row - n)
    lane = lax.broadcasted_iota(jnp.int32, (1, HEAD_DIM), 1)
    map_of_row_has_lane = (row < n) == (lane < MAP_DIM)

    def update(h, kb, vb, k0, n_keys):
        hs = slice(h * HEAD_DIM, (h + 1) * HEAD_DIM)
        slope = slope_ref[h, :, 0:1] * LOG2E
        q = q_ref[0, :, hs]
        q2 = jnp.concatenate([q, q], axis=0)
        q_stack = jnp.where(map_of_row_has_lane, q2, jnp.zeros_like(q2))
        k_pos = k0 + lax.broadcasted_iota(jnp.int32, (1, n_keys), 1)
        allowed = (k_pos >> CHUNK_SHIFT) <= (q_pos >> CHUNK_SHIFT)
        bias = -slope * jnp.abs(q_pos - k_pos).astype(F32)
        s = jnp.where(allowed, _dot_nt(q_stack, kb.astype(BF16)) + bias, NEG)
        m_old = m_ref[h]
        m_new = jnp.maximum(m_old, jnp.max(s, axis=-1, keepdims=True))
        alpha = jnp.exp2(m_old - m_new)
        p = jnp.exp2(s - m_new)
        l_ref[h] = alpha * l_ref[h] + jnp.sum(p, axis=-1, keepdims=True)
        acc_ref[h] = alpha * acc_ref[h] + _dot(p.astype(BF16), vb.astype(BF16))
        m_ref[h] = m_new

    for h in range(HEADS):
        head_rows = pl.ds(h, tk, stride=HEADS)
        update(h, k_ref[0, head_rows, :], v_ref[0, head_rows, :], j * tk, tk)

    @pl.when(j == nk - 1)
    def _():
        lam = _lambda(lam_ref, lam_init)
        for h in range(HEADS):
            hs = slice(h * HEAD_DIM, (h + 1) * HEAD_DIM)
            update(h, kn_ref[0, :, hs], vn_ref[0, :, hs], pos0, n)
            o = acc_ref[h] / l_ref[h]
            o = o[0:n] - lam * o[n:2 * n]
            o_ref[0, :, hs] = (_rms(o, sg_ref[...]) * (1.0 - lam_init)).astype(o_ref.dtype)


def diff_attention_decode(q, k, v, k_new, v_new, slopes, lam_p, sub_gain, *, tk, lam_init):
    b, n, _ = q.shape
    t_k = k.shape[1]
    assert t_k % tk == 0 and t_k % CHUNK == 0
    k = k.reshape(b, t_k * HEADS, HEAD_DIM)
    v = v.reshape(b, t_k * HEADS, HEAD_DIM)
    row_block = pl.BlockSpec((1, n, D_MODEL), lambda bi, j: (bi, 0, 0))
    kv_spec = pl.BlockSpec((1, tk * HEADS, HEAD_DIM), lambda bi, j: (bi, j, 0))
    return pl.pallas_call(
        functools.partial(_attn_decode_kernel, tk=tk, pos0=t_k, lam_init=lam_init),
        grid=(b, t_k // tk),
        in_specs=[
            pl.BlockSpec((HEADS, 1, LANES), lambda bi, j: (0, 0, 0)),
            pl.BlockSpec((4, MAP_DIM), lambda bi, j: (0, 0)),
            pl.BlockSpec((1, HEAD_DIM), lambda bi, j: (0, 0)),
            row_block, kv_spec, kv_spec, row_block, row_block,
        ],
        out_specs=row_block,
        out_shape=jax.ShapeDtypeStruct((b, n, D_MODEL), BF16),
        scratch_shapes=[
            pltpu.VMEM((HEADS, 2 * n, 1), F32),
            pltpu.VMEM((HEADS, 2 * n, 1), F32),
            pltpu.VMEM((HEADS, 2 * n, HEAD_DIM), F32),
        ],
        compiler_params=_params("arbitrary", "arbitrary"),
        name="diff_attention_decode",
    )(slopes, lam_p, sub_gain.reshape(1, HEAD_DIM), q, k, v, k_new, v_new)


def _trunk(x, conv_state, ssm_state, past_k, past_v, wts, *, tm, gdn_chunk, gdn_group, attn_tile, attn_q_tile=None):
    b, t, _ = x.shape
    m = b * t
    xf = x.reshape(m, D_MODEL)
    new_conv, new_ssm = [], []
    k_new = v_new = k_bf = v_t = None
    for l in range(DEPTH):
        gains = wts["norm_gains"][l]
        if l < N_A:
            proj = norm_matmul(xf, gains[0], wts["a_w_in"][l], tm=tm, tn=A_IN_PAD // 3).reshape(b, t, A_IN_PAD)
            w, uu, qg, kd, qkd, egl, conv_l = gdn_prep(proj, conv_state[l], wts["a_conv_w"][l], wts["gate_par"][l],
                                                       C=gdn_chunk, G=gdn_group)
            og, s_l = gdn_scan(w, uu, qg, kd, qkd, egl, proj, ssm_state[l], wts["a_o_gain"][l],
                               C=gdn_chunk, G=gdn_group)
            new_conv.append(conv_l)
            new_ssm.append(s_l)
            xf = matmul_norm_res(og.reshape(m, D_MODEL), wts["a_w_out"][l], gains[1], xf, tm=tm)
        else:
            j = l - N_A
            prompt = past_k is None
            if j == 0:
                k_new, k_bf = norm_matmul(xf, wts["kv_gain"], wts["w_k"], tm=tm, tn=D_MODEL, out_dtypes=(F32, BF16))
                v_new = norm_matmul(xf, wts["kv_gain"], wts["w_v"], tm=tm, tn=D_MODEL, head_transposed_copy=prompt)
                if prompt:
                    v_new, v_t = v_new
            q = norm_matmul(xf, gains[0], wts["b_w_q"][j], tm=tm, tn=D_MODEL, out_dtypes=(BF16,), out_scale=Q_SCALE)
            lam_init = 0.8 - 0.6 * math.exp(-0.3 * l)
            tail = (wts["b_lam"][j], wts["b_sub_gain"][j])
            if prompt:
                assert b == 1
                o = diff_attention_prompt(q, k_bf, v_t, wts["slopes"], alibi_key_columns(attn_tile),
                                          *tail, tq=attn_q_tile, tk=attn_tile, lam_init=lam_init)
            else:
                o = diff_attention_decode(q.reshape(b, t, D_MODEL), past_k, past_v, k_new.reshape(b, t, D_MODEL),
                                          v_new.reshape(b, t, D_MODEL), wts["slopes"], *tail, tk=attn_tile,
                                          lam_init=lam_init)
            xf = matmul_norm_res(o.reshape(m, D_MODEL), wts["b_w_out"][j], gains[1], xf, tm=tm)
        xf = mlp_block(xf, gains[2], wts["mlp_w1"][l], wts["mlp_w2"][l], gains[3], tm=tm, tf=1024)
    kv_shape = (b, t, HEADS, HEAD_DIM)
    return (xf.reshape(b, t, D_MODEL), jnp.stack(new_conv), jnp.stack(new_ssm),
            k_new.reshape(kv_shape), v_new.reshape(kv_shape))


def kernel(x_prompt, x_sample, state_conv, state_ssm, cache_k, cache_v, norm_gains, a_w_in, a_conv_w, a_log,
           a_dt_bias, a_o_gain, a_w_out, kv_gain, w_kv, b_w_q, b_lam, b_sub_gain, b_w_out, mlp_w1, mlp_w2):
    a_in = a_w_in.shape[-1]
    gate_par = jnp.zeros((N_A, 2, LANES), F32)
    gate_par = gate_par.at[:, 0, HEADS:2 * HEADS].set(a_log.astype(F32))
    gate_par = gate_par.at[:, 1, HEADS:2 * HEADS].set(a_dt_bias.astype(F32))
    slopes = jnp.asarray(alibi_slopes())
    wts = dict(
        norm_gains=norm_gains.astype(F32),
        a_w_in=jnp.pad(a_w_in, ((0, 0), (0, 0), (0, A_IN_PAD - a_in))).astype(BF16),
        a_conv_w=a_conv_w.astype(F32),
        gate_par=gate_par,
        a_o_gain=a_o_gain.astype(F32),
        a_w_out=a_w_out.astype(BF16),
        kv_gain=kv_gain.astype(F32),
        w_k=w_kv[:, :D_MODEL].astype(BF16),
        w_v=w_kv[:, D_MODEL:].astype(BF16),
        b_w_q=b_w_q.astype(BF16),
        b_lam=b_lam.astype(F32),
        b_sub_gain=b_sub_gain.astype(F32),
        b_w_out=b_w_out.astype(BF16),
        mlp_w1=mlp_w1.astype(BF16),
        mlp_w2=mlp_w2.astype(BF16),
        slopes=jnp.broadcast_to(slopes[:, None, None], (HEADS, 1, LANES)),
    )
    bp, tp, _ = x_prompt.shape
    conv0 = jnp.zeros((N_A, bp, CONV_W - 1, A_QKV), F32)
    ssm0 = jnp.zeros((N_A, bp, HEADS, HEAD_DIM, HEAD_DIM), F32)
    y_p, p_conv, p_ssm, p_k, p_v = _trunk(x_prompt, conv0, ssm0, None, None, wts,
                                          tm=512, gdn_chunk=CHUNK, gdn_group=4, attn_tile=512, attn_q_tile=1024)
    bs, ts, _ = x_sample.shape
    y_s, s_conv, s_ssm, s_k, s_v = _trunk(x_sample, state_conv, state_ssm, cache_k, cache_v,
                                          wts, tm=bs * ts, gdn_chunk=ts, gdn_group=1, attn_tile=512)
    return (y_p, y_s, p_conv, p_ssm, p_k, p_v, s_conv, s_ssm, s_k, s_v)
```

```python
import functools
import math

import jax
import jax.numpy as jnp
import numpy as np
from jax import lax
from jax.experimental import pallas as pl
from jax.experimental.pallas import tpu as pltpu

F32 = jnp.float32
BF16 = jnp.bfloat16

D_MODEL = 1024
DEPTH = 4
N_A = DEPTH // 2
CHUNK = 64
CHUNK_SHIFT = CHUNK.bit_length() - 1
assert 1 << CHUNK_SHIFT == CHUNK
HEADS = 8
HEAD_DIM = D_MODEL // HEADS
CONV_W = 4
A_QKV = 3 * D_MODEL
A_GATE_COL = A_QKV + D_MODEL
LANES = 128
A_IN_PAD = A_GATE_COL + LANES
MAP_DIM = HEAD_DIM // 2
D_FF = 4 * D_MODEL
EPS = 1e-6
NEG = -1e30
VMEM_LIMIT = 48 * 1024 * 1024
HIGHEST = lax.Precision.HIGHEST
ONES_ROWS = 16
HEAD_ROWS_T = HEAD_DIM + ONES_ROWS
LOG2E = math.log2(math.e)
Q_SCALE = MAP_DIM ** -0.5 * LOG2E


def _dot(a, b, precision=None):
    return jnp.dot(a, b, preferred_element_type=F32, precision=precision)


def _dot_nt(a, b):
    return lax.dot_general(a, b, (((1,), (1,)), ((), ())), preferred_element_type=F32)


def _dot_tn(a, b):
    return lax.dot_general(a, b, (((0,), (0,)), ((), ())), preferred_element_type=F32)


def _split(a):
    hi = a.astype(BF16)
    return hi, (a - hi.astype(F32)).astype(BF16)


def _dot_split(a, b):
    return _dot(a[0], b[0]) + (_dot(a[0], b[1]) + _dot(a[1], b[0]))


def _rms(x, gain):
    return x * lax.rsqrt(jnp.mean(x * x, axis=-1, keepdims=True) + EPS) * gain


def _sigmoid(x):
    return 1.0 / (1.0 + jnp.exp(-x))


def _params(*semantics):
    return pltpu.CompilerParams(dimension_semantics=semantics, vmem_limit_bytes=VMEM_LIMIT)


def _norm_matmul_kernel(x_ref, g_ref, w_ref, *out_and_scratch, out_scale, head_transposed_copy):
    *o_refs, hn_ref = out_and_scratch

    @pl.when(pl.program_id(1) == 0)
    def _():
        hn_ref[...] = _rms(x_ref[...], g_ref[...]).astype(BF16)

    y = _dot(hn_ref[...], w_ref[...])
    if out_scale != 1.0:
        y = y * out_scale
    if head_transposed_copy:
        *o_refs, ot_ref = o_refs
        y_t = y.T.astype(ot_ref.dtype)
        for h in range(y_t.shape[0] // HEAD_DIM):
            r0 = h * HEAD_ROWS_T
            ot_ref[r0:r0 + HEAD_DIM, :] = y_t[h * HEAD_DIM:(h + 1) * HEAD_DIM]
            ot_ref[r0 + HEAD_DIM:r0 + HEAD_ROWS_T, :] = jnp.ones((ONES_ROWS, y_t.shape[1]), ot_ref.dtype)
    for o_ref in o_refs:
        o_ref[...] = y.astype(o_ref.dtype)


def norm_matmul(x, gain, w, *, tm, tn, out_dtypes=(F32,), out_scale=1.0, head_transposed_copy=False):
    m, k = x.shape
    n = w.shape[1]
    out_specs = [pl.BlockSpec((tm, tn), lambda i, j: (i, j)) for _ in out_dtypes]
    out_shape = [jax.ShapeDtypeStruct((m, n), dt) for dt in out_dtypes]
    if head_transposed_copy:
        out_specs.append(pl.BlockSpec((tn // HEAD_DIM * HEAD_ROWS_T, tm), lambda i, j: (j, i)))
        out_shape.append(jax.ShapeDtypeStruct((n // HEAD_DIM * HEAD_ROWS_T, m), BF16))
    outs = pl.pallas_call(
        functools.partial(_norm_matmul_kernel, out_scale=out_scale, head_transposed_copy=head_transposed_copy),
        grid=(m // tm, n // tn),
        in_specs=[
            pl.BlockSpec((tm, k), lambda i, j: (i, 0)),
            pl.BlockSpec((1, k), lambda i, j: (0, 0)),
            pl.BlockSpec((k, tn), lambda i, j: (0, j)),
        ],
        out_specs=out_specs,
        out_shape=out_shape,
        scratch_shapes=[pltpu.VMEM((tm, k), BF16)],
        compiler_params=_params("parallel", "arbitrary"),
        name="norm_matmul",
    )(x, gain.reshape(1, k), w)
    return outs[0] if len(outs) == 1 else outs


def _matmul_norm_res_kernel(a_ref, w_ref, g_ref, x_ref, o_ref):
    y = _dot(a_ref[...], w_ref[...])
    o_ref[...] = x_ref[...] + _rms(y, g_ref[...])


def matmul_norm_res(a, w, gain, x, *, tm):
    m, k = a.shape
    d = w.shape[1]
    return pl.pallas_call(
        _matmul_norm_res_kernel,
        grid=(m // tm,),
        in_specs=[
            pl.BlockSpec((tm, k), lambda i: (i, 0)),
            pl.BlockSpec((k, d), lambda i: (0, 0)),
            pl.BlockSpec((1, d), lambda i: (0, 0)),
            pl.BlockSpec((tm, d), lambda i: (i, 0)),
        ],
        out_specs=pl.BlockSpec((tm, d), lambda i: (i, 0)),
        out_shape=jax.ShapeDtypeStruct((m, d), F32),
        compiler_params=_params("parallel"),
        name="matmul_norm_res",
    )(a, w, gain.reshape(1, d), x)


def _mlp_kernel(x_ref, g_in_ref, w1_ref, w2_ref, g_out_ref, o_ref, hn_ref, acc_ref):
    f = pl.program_id(1)

    @pl.when(f == 0)
    def _():
        hn_ref[...] = _rms(x_ref[...], g_in_ref[...]).astype(BF16)
        acc_ref[...] = jnp.zeros_like(acc_ref)

    h = _dot(hn_ref[...], w1_ref[...])
    h = jnp.square(jnp.maximum(h, 0.0)).astype(BF16)
    acc_ref[...] += _dot(h, w2_ref[...])

    @pl.when(f == pl.num_programs(1) - 1)
    def _():
        o_ref[...] = x_ref[...] + _rms(acc_ref[...], g_out_ref[...])


def mlp_block(x, g_in, w1, w2, g_out, *, tm, tf):
    m, d = x.shape
    ff = w1.shape[1]
    return pl.pallas_call(
        _mlp_kernel,
        grid=(m // tm, ff // tf),
        in_specs=[
            pl.BlockSpec((tm, d), lambda i, f: (i, 0)),
            pl.BlockSpec((1, d), lambda i, f: (0, 0)),
            pl.BlockSpec((d, tf), lambda i, f: (0, f)),
            pl.BlockSpec((tf, d), lambda i, f: (f, 0)),
            pl.BlockSpec((1, d), lambda i, f: (0, 0)),
        ],
        out_specs=pl.BlockSpec((tm, d), lambda i, f: (i, 0)),
        out_shape=jax.ShapeDtypeStruct((m, d), F32),
        scratch_shapes=[pltpu.VMEM((tm, d), BF16), pltpu.VMEM((tm, d), F32)],
        compiler_params=_params("parallel", "arbitrary"),
        name="mlp_block",
    )(x, g_in.reshape(1, d), w1, w2, g_out.reshape(1, d))


CONV_ROW0 = 8


def _gdn_prep_kernel(proj_ref, conv0_ref, cw_ref, gp_ref,
                     w_ref, uu_ref, qg_ref, kd_ref, qkd_ref, egl_ref, conv_ref, xbuf_ref, *, C, G):
    R = C * G
    prev0 = CONV_ROW0 - (CONV_W - 1)

    @pl.when(pl.program_id(1) == 0)
    def _():
        xbuf_ref[prev0:CONV_ROW0, :] = conv0_ref[0]

    xbuf_ref[CONV_ROW0:CONV_ROW0 + R, :] = proj_ref[0, :, 0:A_QKV]
    cw = cw_ref[...]
    y = xbuf_ref[prev0:prev0 + R, :] * cw[0:1, :]
    for j in range(1, CONV_W):
        y = y + xbuf_ref[prev0 + j:prev0 + j + R, :] * cw[j:j + 1, :]
    y = y * _sigmoid(y)
    last = xbuf_ref[prev0 + R:CONV_ROW0 + R, :]
    xbuf_ref[prev0:CONV_ROW0, :] = last
    conv_ref[0] = last

    tail = proj_ref[0, :, A_GATE_COL:A_IN_PAD]
    gp = gp_ref[...]
    beta = _sigmoid(tail)
    t = tail + gp[1:2, :]
    softplus = jnp.maximum(t, 0.0) + jnp.log(1.0 + jnp.exp(-jnp.abs(t)))
    g = -jnp.exp(gp[0:1, :]) * softplus

    rr = lax.broadcasted_iota(jnp.int32, (R, R), 0)
    cc = lax.broadcasted_iota(jnp.int32, (R, R), 1)
    shift = C.bit_length() - 1
    same = (rr >> shift) == (cc >> shift)
    gc = _dot(jnp.where(same & (rr >= cc), 1.0, 0.0), g, HIGHEST)
    gl = _dot(jnp.where(same, 1.0, 0.0), g, HIGHEST)
    pad_rows = -R % LANES
    gc_t = (jnp.concatenate([gc, jnp.zeros((pad_rows, LANES), F32)], axis=0) if pad_rows else gc).T
    eg = jnp.exp(gc)
    ekd = jnp.exp(gl - gc)
    egl = jnp.exp(gl)
    for gi in range(G):
        egl_ref[0, gi] = egl[gi * C:gi * C + 1, :]

    row = lax.broadcasted_iota(jnp.int32, (C, C), 0)
    col = lax.broadcasted_iota(jnp.int32, (C, C), 1)
    incl = row >= col
    strict = row > col
    eye = (row == col).astype(F32)

    units = [(gi, h) for gi in range(G) for h in range(HEADS)]
    qn, kn = [], []
    for h in range(HEADS):
        q = y[:, h * HEAD_DIM:(h + 1) * HEAD_DIM]
        k = y[:, D_MODEL + h * HEAD_DIM:D_MODEL + (h + 1) * HEAD_DIM]
        qn.append(q * lax.rsqrt(jnp.sum(q * q, axis=-1, keepdims=True) + EPS) * (HEAD_DIM ** -0.5))
        kn.append(k * lax.rsqrt(jnp.sum(k * k, axis=-1, keepdims=True) + EPS))

    def rows(gi):
        return slice(gi * C, (gi + 1) * C)

    decay, qk, a_mat = {}, {}, {}
    for gi, h in units:
        rs, gl_h = rows(gi), HEADS + h
        g_col = gc[rs, gl_h:gl_h + 1]
        g_row = gc_t[gl_h:gl_h + 1, gi * C:(gi + 1) * C]
        decay[gi, h] = jnp.where(incl, jnp.exp(jnp.minimum(g_col - g_row, 0.0)), 0.0)
        k_bf = kn[h][rs].astype(BF16)
        qk_kk = _dot_nt(jnp.concatenate([qn[h][rs].astype(BF16), k_bf], axis=0), k_bf)
        qk[gi, h] = qk_kk[0:C]
        a_mat[gi, h] = jnp.where(strict, beta[rs, h:h + 1] * qk_kk[C:2 * C] * decay[gi, h], 0.0)
    t_inv = {u: eye - a_mat[u] for u in units}
    pw = {}
    for u in units:
        a_s = _split(a_mat[u])
        pw[u] = _dot_split(a_s, a_s)
    n_double = C.bit_length() - 2
    for step in range(n_double):
        for u in units:
            p_s = _split(pw[u])
            t_s = _split(t_inv[u])
            if step + 1 < n_double:
                tp = _dot_split(tuple(jnp.concatenate([t_part, p_part], axis=0) for t_part, p_part in zip(t_s, p_s)),
                                p_s)
                t_inv[u] = t_inv[u] + tp[0:C]
                pw[u] = tp[C:2 * C]
            else:
                t_inv[u] = t_inv[u] + _dot_split(t_s, p_s)
    for gi, h in units:
        rs, gl_h = rows(gi), HEADS + h
        hs = slice(h * HEAD_DIM, (h + 1) * HEAD_DIM)
        b_col = beta[rs, h:h + 1]
        eg_col = eg[rs, gl_h:gl_h + 1]
        k = kn[h][rs]
        v = y[rs, 2 * D_MODEL + h * HEAD_DIM:2 * D_MODEL + (h + 1) * HEAD_DIM]
        rhs = jnp.concatenate([b_col * v, (b_col * eg_col) * k], axis=-1).astype(BF16)
        sol = _dot(t_inv[gi, h].astype(BF16), rhs)
        uu_ref[0, rs, hs] = sol[:, 0:HEAD_DIM].astype(uu_ref.dtype)
        w_ref[0, rs, hs] = sol[:, HEAD_DIM:].astype(w_ref.dtype)
        qg_ref[0, rs, hs] = (eg_col * qn[h][rs]).astype(qg_ref.dtype)
        kd_ref[0, rs, hs] = (ekd[rs, gl_h:gl_h + 1] * k).astype(kd_ref.dtype)
        qkd_ref[0, rs, h * C:(h + 1) * C] = (qk[gi, h] * decay[gi, h]).astype(qkd_ref.dtype)


def gdn_prep(proj, conv0, conv_w, gate_par, *, C, G):
    b, t, _ = proj.shape
    r = C * G
    nc = t // C
    row_spec = pl.BlockSpec((1, r, D_MODEL), lambda i, c: (i, c, 0))
    conv_spec = pl.BlockSpec((1, CONV_W - 1, A_QKV), lambda i, c: (i, 0, 0))
    return pl.pallas_call(
        functools.partial(_gdn_prep_kernel, C=C, G=G),
        grid=(b, t // r),
        in_specs=[
            pl.BlockSpec((1, r, A_IN_PAD), lambda i, c: (i, c, 0)),
            conv_spec,
            pl.BlockSpec((CONV_W, A_QKV), lambda i, c: (0, 0)),
            pl.BlockSpec((2, LANES), lambda i, c: (0, 0)),
        ],
        out_specs=[
            row_spec, row_spec, row_spec, row_spec,
            pl.BlockSpec((1, r, HEADS * C), lambda i, c: (i, c, 0)),
            pl.BlockSpec((1, G, 1, LANES), lambda i, c: (i, c, 0, 0)),
            conv_spec,
        ],
        out_shape=[jax.ShapeDtypeStruct((b, t, D_MODEL), BF16)] * 4 + [
            jax.ShapeDtypeStruct((b, t, HEADS * C), BF16),
            jax.ShapeDtypeStruct((b, nc, 1, LANES), F32),
            jax.ShapeDtypeStruct((b, CONV_W - 1, A_QKV), F32),
        ],
        scratch_shapes=[pltpu.VMEM((CONV_ROW0 + r, A_QKV), F32)],
        compiler_params=_params("arbitrary", "arbitrary"),
        name="gdn_prep",
    )(proj, conv0, conv_w, gate_par)


def _gdn_scan_kernel(w_ref, uu_ref, qg_ref, kd_ref, qkd_ref, egl_ref, z_ref, s0_ref, og_ref, o_ref, s_ref, *, C, G):
    @pl.when(pl.program_id(1) == 0)
    def _():
        s_ref[0] = s0_ref[0]

    heads = range(HEADS)
    for gi in range(G):
        rs = slice(gi * C, (gi + 1) * C)
        egl = egl_ref[0, gi]

        def hs(h):
            return slice(h * HEAD_DIM, (h + 1) * HEAD_DIM)

        s_old = [s_ref[0, h] for h in heads]
        s_bf = [s.astype(BF16) for s in s_old]
        wq = [_dot(jnp.concatenate([w_ref[0, rs, hs(h)], qg_ref[0, rs, hs(h)]], axis=0), s_bf[h]) for h in heads]
        u_bf = [(uu_ref[0, rs, hs(h)].astype(F32) - wq[h][0:C]).astype(BF16) for h in heads]
        o = [wq[h][C:2 * C] + _dot(qkd_ref[0, rs, h * C:(h + 1) * C], u_bf[h]) for h in heads]
        for h in heads:
            s_ref[0, h] = egl[:, HEADS + h:HEADS + h + 1] * s_old[h] + _dot_tn(kd_ref[0, rs, hs(h)], u_bf[h])
        for h in heads:
            z = z_ref[0, rs, hs(h)]
            o_ref[0, rs, hs(h)] = (_rms(o[h], og_ref[...]) * (z * _sigmoid(z))).astype(o_ref.dtype)


def gdn_scan(w, uu, qg, kd, qkd, egl, proj, s0, o_gain, *, C, G):
    b, t, _ = w.shape
    r = C * G
    row_spec = pl.BlockSpec((1, r, D_MODEL), lambda i, c: (i, c, 0))
    state_spec = pl.BlockSpec((1, HEADS, HEAD_DIM, HEAD_DIM), lambda i, c: (i, 0, 0, 0))
    return pl.pallas_call(
        functools.partial(_gdn_scan_kernel, C=C, G=G),
        grid=(b, t // r),
        in_specs=[
            row_spec, row_spec, row_spec, row_spec,
            pl.BlockSpec((1, r, HEADS * C), lambda i, c: (i, c, 0)),
            pl.BlockSpec((1, G, 1, LANES), lambda i, c: (i, c, 0, 0)),
            pl.BlockSpec((1, r, D_MODEL), lambda i, c: (i, c, A_QKV // D_MODEL)),
            state_spec,
            pl.BlockSpec((1, HEAD_DIM), lambda i, c: (0, 0)),
        ],
        out_specs=[row_spec, state_spec],
        out_shape=[
            jax.ShapeDtypeStruct((b, t, D_MODEL), BF16),
            jax.ShapeDtypeStruct((b, HEADS, HEAD_DIM, HEAD_DIM), F32),
        ],
        compiler_params=_params("arbitrary", "arbitrary"),
        name="gdn_scan",
    )(w, uu, qg, kd, qkd, egl, proj, s0, o_gain.reshape(1, HEAD_DIM))


def _lambda(lam_ref, lam_init):
    lp = lam_ref[...]
    return (jnp.exp(jnp.sum(lp[0:1] * lp[1:2], axis=-1, keepdims=True))
            - jnp.exp(jnp.sum(lp[2:3] * lp[3:4], axis=-1, keepdims=True)) + lam_init)


KEY_COLS = 3


def _attn_prompt_kernel(qi_ref, kj_ref, slope_ref, lam_ref, sg_ref, kx_ref, q_ref, k_ref, vt_ref, o_ref,
                        s_even_ref, s_odd_ref, smax_even_ref, smax_odd_ref, m_ref, acc_ref,
                        *, tq, tk, n_pairs, lam_init):
    step = pl.program_id(1)
    slope = slope_ref[0, :, 0:1] * LOG2E
    lane = lax.broadcasted_iota(jnp.int32, (1, HEAD_DIM), 1)
    map_lanes = [lane < MAP_DIM, lane >= MAP_DIM]

    def score_stage(s_ref, smax_ref):
        q = q_ref[...]
        k = k_ref[...]
        for mp in range(2):
            a0 = MAP_DIM if mp == 0 else 0
            extra = (lane >= a0) & (lane < a0 + KEY_COLS)
            qa = jnp.where(map_lanes[mp], q, jnp.where(extra, 1.0, 0.0).astype(BF16))
            ka = jnp.where(map_lanes[mp], k, kx_ref[0, mp])
            scores = _dot_nt(ka, qa)
            s_ref[mp] = scores
            smax_ref[mp] = jnp.max(scores, axis=0, keepdims=True)

    def mask_diagonal(s_ref, smax_ref, key_offset):
        j = lax.broadcasted_iota(jnp.int32, (tk, 1), 0)
        i = lax.broadcasted_iota(jnp.int32, (1, tk), 1)
        allowed = (j >> CHUNK_SHIFT) <= (i >> CHUNK_SHIFT)
        fix = jnp.where(j > i, (2.0 * slope) * (i - j).astype(F32), 0.0)
        facing = slice(key_offset, key_offset + tk)
        for mp in range(2):
            if key_offset:
                s_ref[mp, :, 0:key_offset] = jnp.full((tk, key_offset), NEG, F32)
            s_ref[mp, :, facing] = jnp.where(allowed, s_ref[mp, :, facing] + fix, NEG)
            smax_ref[mp] = jnp.max(s_ref[mp], axis=0, keepdims=True)

    def softmax_stage(s_ref, smax_ref, shift_const):
        vt_aug = vt_ref[...]
        probs, alphas = [], []
        for mp in range(2):
            scores = s_ref[mp]
            m_old = m_ref[mp]
            m_new = jnp.maximum(m_old, smax_ref[mp] + shift_const)
            probs.append(jnp.exp2(scores - (m_new - shift_const)).astype(BF16))
            alphas.append(jnp.exp2(m_old - m_new))
            m_ref[mp] = m_new
        pv = [_dot(vt_aug, probs[mp]) for mp in range(2)]
        for mp in range(2):
            acc_ref[mp] = alphas[mp] * acc_ref[mp] + pv[mp]

    def finalize():
        lam = _lambda(lam_ref, lam_init)
        a0 = acc_ref[0]
        a1 = acc_ref[1]
        o_t = (a0[0:HEAD_DIM] / a0[HEAD_DIM:HEAD_DIM + 1]
               - lam * (a1[0:HEAD_DIM] / a1[HEAD_DIM:HEAD_DIM + 1]))
        o_t = o_t * (lax.rsqrt(jnp.mean(o_t * o_t, axis=0, keepdims=True) + EPS) * (1.0 - lam_init))
        o_ref[...] = (o_t.T * sg_ref[...]).astype(o_ref.dtype)

    scored = jnp.minimum(step, n_pairs - 1)
    key_offset_s = kj_ref[scored] * tk - qi_ref[scored] * tq
    consumed = jnp.maximum(step - 1, 0)
    kj_c = kj_ref[consumed]
    key_offset_c = kj_c * tk - qi_ref[consumed] * tq

    @pl.when(step == 0)
    def _():
        score_stage(s_even_ref, smax_even_ref)

    @pl.when((step > 0) & (kj_c == 0))
    def _():
        m_ref[...] = jnp.full_like(m_ref, NEG)
        acc_ref[...] = jnp.zeros_like(acc_ref)

    tile_shift = slope * key_offset_c.astype(F32)
    even, odd = (s_even_ref, smax_even_ref), (s_odd_ref, smax_odd_ref)
    for parity, (written, read) in enumerate([(even, odd), (odd, even)]):
        @pl.when((step > 0) & (step % 2 == parity))
        def _():
            score_stage(*written)
            softmax_stage(*read, tile_shift)

    for parity, written in enumerate([even, odd]):
        for key_offset in range(0, tq, tk):
            @pl.when((key_offset_s == key_offset) & (step % 2 == parity))
            def _():
                mask_diagonal(*written, key_offset)

    @pl.when((step > 0) & (key_offset_c == tq - tk))
    def _():
        finalize()


def alibi_slopes():
    return 2.0 ** (-8.0 * np.arange(1, HEADS + 1, dtype=np.float32) / HEADS)


def alibi_key_columns(t):
    term = (np.float32(LOG2E) * alibi_slopes())[:, None] * np.arange(t, dtype=np.float32)[None, :]
    cols = np.zeros((HEADS, 2, t, HEAD_DIM), np.float32)
    for c in range(KEY_COLS):
        piece = term.astype(BF16).astype(np.float32)
        term = term - piece
        for mp in range(2):
            a0 = MAP_DIM if mp == 0 else 0
            cols[:, mp, :, a0 + c] = piece
    return jnp.asarray(cols.astype(BF16))


def diff_attention_prompt(q, k, v_t, slopes, key_cols, lam_p, sub_gain, *, tq, tk, lam_init):
    t_all = q.shape[0]
    assert tq % tk == 0 and tk % CHUNK == 0 and t_all % tq == 0
    per_q = tq // tk
    qi = np.concatenate([np.full((i + 1) * per_q, i, np.int32) for i in range(t_all // tq)])
    kj = np.concatenate([np.arange((i + 1) * per_q, dtype=np.int32) for i in range(t_all // tq)])
    n_pairs = len(qi)

    def scored(p):
        return jnp.minimum(p, n_pairs - 1)

    def consumed(p):
        return jnp.maximum(p - 1, 0)

    grid_spec = pltpu.PrefetchScalarGridSpec(
        num_scalar_prefetch=2,
        grid=(HEADS, n_pairs + 1),
        in_specs=[
            pl.BlockSpec((1, 1, LANES), lambda h, p, qi_r, kj_r: (h, 0, 0)),
            pl.BlockSpec((4, MAP_DIM), lambda h, p, qi_r, kj_r: (0, 0)),
            pl.BlockSpec((1, HEAD_DIM), lambda h, p, qi_r, kj_r: (0, 0)),
            pl.BlockSpec((1, 2, tk, HEAD_DIM), lambda h, p, qi_r, kj_r: (h, 0, 0, 0)),
            pl.BlockSpec((tq, HEAD_DIM), lambda h, p, qi_r, kj_r: (qi_r[scored(p)], h)),
            pl.BlockSpec((tk, HEAD_DIM), lambda h, p, qi_r, kj_r: (kj_r[scored(p)], h)),
            pl.BlockSpec((HEAD_ROWS_T, tk), lambda h, p, qi_r, kj_r: (h, kj_r[consumed(p)])),
        ],
        out_specs=pl.BlockSpec((tq, HEAD_DIM), lambda h, p, qi_r, kj_r: (qi_r[consumed(p)], h)),
        scratch_shapes=[
            pltpu.VMEM((2, tk, tq), F32), pltpu.VMEM((2, tk, tq), F32),
            pltpu.VMEM((2, 1, tq), F32), pltpu.VMEM((2, 1, tq), F32),
            pltpu.VMEM((2, 1, tq), F32), pltpu.VMEM((2, HEAD_ROWS_T, tq), F32),
        ],
    )
    return pl.pallas_call(
        functools.partial(_attn_prompt_kernel, tq=tq, tk=tk, n_pairs=n_pairs, lam_init=lam_init),
        grid_spec=grid_spec,
        out_shape=jax.ShapeDtypeStruct((t_all, D_MODEL), BF16),
        compiler_params=_params("arbitrary", "arbitrary"),
        name="diff_attention_prompt",
    )(jnp.asarray(qi), jnp.asarray(kj), slopes, lam_p, sub_gain.reshape(1, HEAD_DIM), key_cols, q, k, v_t)


def _attn_decode_kernel(slope_ref, lam_ref, sg_ref, q_ref, k_ref, v_ref, kn_ref, vn_ref, o_ref, m_ref, l_ref, acc_ref,
                        *, tk, pos0, lam_init):
    j = pl.program_id(1)
    nk = pl.num_programs(1)
    n = q_ref.shape[1]

    @pl.when(j == 0)
    def _():
        m_ref[...] = jnp.full_like(m_ref, NEG)
        l_ref[...] = jnp.zeros_like(l_ref)
        acc_ref[...] = jnp.zeros_like(acc_ref)

    row = lax.broadcasted_iota(jnp.int32, (2 * n, 1), 0)
    q_pos = pos0 + jnp.where(row < n, row, row - n)
    lane = lax.broadcasted_iota(jnp.int32, (1, HEAD_DIM), 1)
    map_of_row_has_lane = (row < n) == (lane < MAP_DIM)

    def update(keys, values, k0, n_keys):
        heads = range(HEADS)
        k_pos = k0 + lax.broadcasted_iota(jnp.int32, (1, n_keys), 1)
        allowed = (k_pos >> CHUNK_SHIFT) <= (q_pos >> CHUNK_SHIFT)
        dist = jnp.abs(q_pos - k_pos).astype(F32)
        scores = []
        for h in heads:
            slope = slope_ref[h, :, 0:1] * LOG2E
            q = q_ref[0, :, h * HEAD_DIM:(h + 1) * HEAD_DIM]
            q2 = jnp.concatenate([q, q], axis=0)
            q_stack = jnp.where(map_of_row_has_lane, q2, jnp.zeros_like(q2))
            scores.append(jnp.where(allowed, _dot_nt(q_stack, keys(h).astype(BF16)) - slope * dist, NEG))
        probs, alphas = [], []
        for h in heads:
            m_old = m_ref[h]
            m_new = jnp.maximum(m_old, jnp.max(scores[h], axis=-1, keepdims=True))
            alphas.append(jnp.exp2(m_old - m_new))
            p = jnp.exp2(scores[h] - m_new)
            l_ref[h] = alphas[h] * l_ref[h] + jnp.sum(p, axis=-1, keepdims=True)
            probs.append(p.astype(BF16))
            m_ref[h] = m_new
        pv = [_dot(probs[h], values(h).astype(BF16)) for h in heads]
        for h in heads:
            acc_ref[h] = alphas[h] * acc_ref[h] + pv[h]

    def head_rows(h):
        return pl.ds(h, tk, stride=HEADS)

    update(lambda h: k_ref[0, head_rows(h), :], lambda h: v_ref[0, head_rows(h), :], j * tk, tk)

    @pl.when(j == nk - 1)
    def _():
        update(lambda h: kn_ref[0, :, h * HEAD_DIM:(h + 1) * HEAD_DIM],
               lambda h: vn_ref[0, :, h * HEAD_DIM:(h + 1) * HEAD_DIM], pos0, n)
        lam = _lambda(lam_ref, lam_init)
        for h in range(HEADS):
            o = acc_ref[h] / l_ref[h]
            o = o[0:n] - lam * o[n:2 * n]
            o_ref[0, :, h * HEAD_DIM:(h + 1) * HEAD_DIM] = (_rms(o, sg_ref[...]) * (1.0 - lam_init)).astype(o_ref.dtype)


def diff_attention_decode(q, k, v, k_new, v_new, slopes, lam_p, sub_gain, *, tk, lam_init):
    b, n, _ = q.shape
    t_k = k.shape[1]
    assert t_k % tk == 0 and t_k % CHUNK == 0
    k = k.reshape(b, t_k * HEADS, HEAD_DIM)
    v = v.reshape(b, t_k * HEADS, HEAD_DIM)
    row_block = pl.BlockSpec((1, n, D_MODEL), lambda bi, j: (bi, 0, 0))
    kv_spec = pl.BlockSpec((1, tk * HEADS, HEAD_DIM), lambda bi, j: (bi, j, 0))
    return pl.pallas_call(
        functools.partial(_attn_decode_kernel, tk=tk, pos0=t_k, lam_init=lam_init),
        grid=(b, t_k // tk),
        in_specs=[
            pl.BlockSpec((HEADS, 1, LANES), lambda bi, j: (0, 0, 0)),
            pl.BlockSpec((4, MAP_DIM), lambda bi, j: (0, 0)),
            pl.BlockSpec((1, HEAD_DIM), lambda bi, j: (0, 0)),
            row_block, kv_spec, kv_spec, row_block, row_block,
        ],
        out_specs=row_block,
        out_shape=jax.ShapeDtypeStruct((b, n, D_MODEL), BF16),
        scratch_shapes=[
            pltpu.VMEM((HEADS, 2 * n, 1), F32),
            pltpu.VMEM((HEADS, 2 * n, 1), F32),
            pltpu.VMEM((HEADS, 2 * n, HEAD_DIM), F32),
        ],
        compiler_params=_params("arbitrary", "arbitrary"),
        name="diff_attention_decode",
    )(slopes, lam_p, sub_gain.reshape(1, HEAD_DIM), q, k, v, k_new, v_new)


def _trunk(x, conv_state, ssm_state, past_k, past_v, wts, *, tm, gdn_chunk, gdn_group, attn_tile, attn_q_tile=None):
    b, t, _ = x.shape
    m = b * t
    xf = x.reshape(m, D_MODEL)
    new_conv, new_ssm = [], []
    k_new = v_new = k_bf = v_t = None
    for l in range(DEPTH):
        gains = wts["norm_gains"][l]
        if l < N_A:
            proj = norm_matmul(xf, gains[0], wts["a_w_in"][l], tm=tm, tn=A_IN_PAD // 3).reshape(b, t, A_IN_PAD)
            w, uu, qg, kd, qkd, egl, conv_l = gdn_prep(proj, conv_state[l], wts["a_conv_w"][l], wts["gate_par"][l],
                                                       C=gdn_chunk, G=gdn_group)
            og, s_l = gdn_scan(w, uu, qg, kd, qkd, egl, proj, ssm_state[l], wts["a_o_gain"][l],
                               C=gdn_chunk, G=gdn_group)
            new_conv.append(conv_l)
            new_ssm.append(s_l)
            xf = matmul_norm_res(og.reshape(m, D_MODEL), wts["a_w_out"][l], gains[1], xf, tm=tm)
        else:
            j = l - N_A
            prompt = past_k is None
            if j == 0:
                k_new, k_bf = norm_matmul(xf, wts["kv_gain"], wts["w_k"], tm=tm, tn=D_MODEL, out_dtypes=(F32, BF16))
                v_new = norm_matmul(xf, wts["kv_gain"], wts["w_v"], tm=tm, tn=D_MODEL, head_transposed_copy=prompt)
                if prompt:
                    v_new, v_t = v_new
            q = norm_matmul(xf, gains[0], wts["b_w_q"][j], tm=tm, tn=D_MODEL, out_dtypes=(BF16,), out_scale=Q_SCALE)
            lam_init = 0.8 - 0.6 * math.exp(-0.3 * l)
            tail = (wts["b_lam"][j], wts["b_sub_gain"][j])
            if prompt:
                assert b == 1
                o = diff_attention_prompt(q, k_bf, v_t, wts["slopes"], alibi_key_columns(attn_tile),
                                          *tail, tq=attn_q_tile, tk=attn_tile, lam_init=lam_init)
            else:
                o = diff_attention_decode(q.reshape(b, t, D_MODEL), past_k, past_v, k_new.reshape(b, t, D_MODEL),
                                          v_new.reshape(b, t, D_MODEL), wts["slopes"], *tail, tk=attn_tile,
                                          lam_init=lam_init)
            xf = matmul_norm_res(o.reshape(m, D_MODEL), wts["b_w_out"][j], gains[1], xf, tm=tm)
        xf = mlp_block(xf, gains[2], wts["mlp_w1"][l], wts["mlp_w2"][l], gains[3], tm=tm, tf=1024)
    kv_shape = (b, t, HEADS, HEAD_DIM)
    return (xf.reshape(b, t, D_MODEL), jnp.stack(new_conv), jnp.stack(new_ssm),
            k_new.reshape(kv_shape), v_new.reshape(kv_shape))


def kernel(x_prompt, x_sample, state_conv, state_ssm, cache_k, cache_v, norm_gains, a_w_in, a_conv_w, a_log,
           a_dt_bias, a_o_gain, a_w_out, kv_gain, w_kv, b_w_q, b_lam, b_sub_gain, b_w_out, mlp_w1, mlp_w2):
    a_in = a_w_in.shape[-1]
    gate_par = jnp.zeros((N_A, 2, LANES), F32)
    gate_par = gate_par.at[:, 0, HEADS:2 * HEADS].set(a_log.astype(F32))
    gate_par = gate_par.at[:, 1, HEADS:2 * HEADS].set(a_dt_bias.astype(F32))
    slopes = jnp.asarray(alibi_slopes())
    wts = dict(
        norm_gains=norm_gains.astype(F32),
        a_w_in=jnp.pad(a_w_in, ((0, 0), (0, 0), (0, A_IN_PAD - a_in))).astype(BF16),
        a_conv_w=a_conv_w.astype(F32),
        gate_par=gate_par,
        a_o_gain=a_o_gain.astype(F32),
        a_w_out=a_w_out.astype(BF16),
        kv_gain=kv_gain.astype(F32),
        w_k=w_kv[:, :D_MODEL].astype(BF16),
        w_v=w_kv[:, D_MODEL:].astype(BF16),
        b_w_q=b_w_q.astype(BF16),
        b_lam=b_lam.astype(F32),
        b_sub_gain=b_sub_gain.astype(F32),
        b_w_out=b_w_out.astype(BF16),
        mlp_w1=mlp_w1.astype(BF16),
        mlp_w2=mlp_w2.astype(BF16),
        slopes=jnp.broadcast_to(slopes[:, None, None], (HEADS, 1, LANES)),
    )
    bp, tp, _ = x_prompt.shape
    conv0 = jnp.zeros((N_A, bp, CONV_W - 1, A_QKV), F32)
    ssm0 = jnp.zeros((N_A, bp, HEADS, HEAD_DIM, HEAD_DIM), F32)
    y_p, p_conv, p_ssm, p_k, p_v = _trunk(x_prompt, conv0, ssm0, None, None, wts,
                                          tm=512, gdn_chunk=CHUNK, gdn_group=4, attn_tile=512, attn_q_tile=1024)
    bs, ts, _ = x_sample.shape
    y_s, s_conv, s_ssm, s_k, s_v = _trunk(x_sample, state_conv, state_ssm, cache_k, cache_v,
                                          wts, tm=bs * ts, gdn_chunk=ts, gdn_group=1, attn_tile=512)
    return (y_p, y_s, p_conv, p_ssm, p_k, p_v, s_conv, s_ssm, s_k, s_v)
```

```python
import functools
import math

import jax
import jax.numpy as jnp
import numpy as np
from jax import lax
from jax.experimental import pallas as pl
from jax.experimental.pallas import tpu as pltpu

F32 = jnp.float32
BF16 = jnp.bfloat16

D_MODEL = 1024
DEPTH = 4
N_A = DEPTH // 2
CHUNK = 64
CHUNK_SHIFT = CHUNK.bit_length() - 1
assert 1 << CHUNK_SHIFT == CHUNK
HEADS = 8
HEAD_DIM = D_MODEL // HEADS
CONV_W = 4
A_QKV = 3 * D_MODEL
A_GATE_COL = A_QKV + D_MODEL
LANES = 128
A_IN_PAD = A_GATE_COL + LANES
MAP_DIM = HEAD_DIM // 2
D_FF = 4 * D_MODEL
EPS = 1e-6
NEG = -1e30
VMEM_LIMIT = 48 * 1024 * 1024
HIGHEST = lax.Precision.HIGHEST
ONES_ROWS = 16
HEAD_ROWS_T = HEAD_DIM + ONES_ROWS
LOG2E = math.log2(math.e)
Q_SCALE = MAP_DIM ** -0.5 * LOG2E


def _dot(a, b, precision=None):
    return jnp.dot(a, b, preferred_element_type=F32, precision=precision)


def _dot_nt(a, b):
    return lax.dot_general(a, b, (((1,), (1,)), ((), ())), preferred_element_type=F32)


def _dot_tn(a, b):
    return lax.dot_general(a, b, (((0,), (0,)), ((), ())), preferred_element_type=F32)


def _split(a):
    hi = a.astype(BF16)
    return hi, (a - hi.astype(F32)).astype(BF16)


def _dot_split(a, b):
    return _dot(a[0], b[0]) + (_dot(a[0], b[1]) + _dot(a[1], b[0]))


def _rms(x, gain):
    return x * lax.rsqrt(jnp.mean(x * x, axis=-1, keepdims=True) + EPS) * gain


def _sigmoid(x):
    return 1.0 / (1.0 + jnp.exp(-x))


def _params(*semantics):
    return pltpu.CompilerParams(dimension_semantics=semantics, vmem_limit_bytes=VMEM_LIMIT)


def _norm_matmul_kernel(x_ref, g_ref, w_ref, *out_and_scratch, out_scale, head_transposed_copy):
    *o_refs, hn_ref = out_and_scratch

    @pl.when(pl.program_id(1) == 0)
    def _():
        hn_ref[...] = _rms(x_ref[...], g_ref[...]).astype(BF16)

    y = _dot(hn_ref[...], w_ref[...])
    if out_scale != 1.0:
        y = y * out_scale
    if head_transposed_copy:
        *o_refs, ot_ref = o_refs
        y_t = y.T.astype(ot_ref.dtype)
        for h in range(y_t.shape[0] // HEAD_DIM):
            r0 = h * HEAD_ROWS_T
            ot_ref[r0:r0 + HEAD_DIM, :] = y_t[h * HEAD_DIM:(h + 1) * HEAD_DIM]
            ot_ref[r0 + HEAD_DIM:r0 + HEAD_ROWS_T, :] = jnp.ones((ONES_ROWS, y_t.shape[1]), ot_ref.dtype)
    for o_ref in o_refs:
        o_ref[...] = y.astype(o_ref.dtype)


def norm_matmul(x, gain, w, *, tm, tn, out_dtypes=(F32,), out_scale=1.0, head_transposed_copy=False):
    m, k = x.shape
    n = w.shape[1]
    out_specs = [pl.BlockSpec((tm, tn), lambda i, j: (i, j)) for _ in out_dtypes]
    out_shape = [jax.ShapeDtypeStruct((m, n), dt) for dt in out_dtypes]
    if head_transposed_copy:
        out_specs.append(pl.BlockSpec((tn // HEAD_DIM * HEAD_ROWS_T, tm), lambda i, j: (j, i)))
        out_shape.append(jax.ShapeDtypeStruct((n // HEAD_DIM * HEAD_ROWS_T, m), BF16))
    outs = pl.pallas_call(
        functools.partial(_norm_matmul_kernel, out_scale=out_scale, head_transposed_copy=head_transposed_copy),
        grid=(m // tm, n // tn),
        in_specs=[
            pl.BlockSpec((tm, k), lambda i, j: (i, 0)),
            pl.BlockSpec((1, k), lambda i, j: (0, 0)),
            pl.BlockSpec((k, tn), lambda i, j: (0, j)),
        ],
        out_specs=out_specs,
        out_shape=out_shape,
        scratch_shapes=[pltpu.VMEM((tm, k), BF16)],
        compiler_params=_params("parallel", "arbitrary"),
        name="norm_matmul",
    )(x, gain.reshape(1, k), w)
    return outs[0] if len(outs) == 1 else outs


def _matmul_norm_res_kernel(a_ref, w_ref, g_ref, x_ref, o_ref):
    y = _dot(a_ref[...], w_ref[...])
    o_ref[...] = x_ref[...] + _rms(y, g_ref[...])


def matmul_norm_res(a, w, gain, x, *, tm):
    m, k = a.shape
    d = w.shape[1]
    return pl.pallas_call(
        _matmul_norm_res_kernel,
        grid=(m // tm,),
        in_specs=[
            pl.BlockSpec((tm, k), lambda i: (i, 0)),
            pl.BlockSpec((k, d), lambda i: (0, 0)),
            pl.BlockSpec((1, d), lambda i: (0, 0)),
            pl.BlockSpec((tm, d), lambda i: (i, 0)),
        ],
        out_specs=pl.BlockSpec((tm, d), lambda i: (i, 0)),
        out_shape=jax.ShapeDtypeStruct((m, d), F32),
        compiler_params=_params("parallel"),
        name="matmul_norm_res",
    )(a, w, gain.reshape(1, d), x)


def _mlp_kernel(x_ref, g_in_ref, w1_ref, w2_ref, g_out_ref, o_ref, hn_ref, acc_ref):
    f = pl.program_id(1)

    @pl.when(f == 0)
    def _():
        hn_ref[...] = _rms(x_ref[...], g_in_ref[...]).astype(BF16)
        acc_ref[...] = jnp.zeros_like(acc_ref)

    h = _dot(hn_ref[...], w1_ref[...])
    h = jnp.square(jnp.maximum(h, 0.0)).astype(BF16)
    acc_ref[...] += _dot(h, w2_ref[...])

    @pl.when(f == pl.num_programs(1) - 1)
    def _():
        o_ref[...] = x_ref[...] + _rms(acc_ref[...], g_out_ref[...])


def mlp_block(x, g_in, w1, w2, g_out, *, tm, tf):
    m, d = x.shape
    ff = w1.shape[1]
    return pl.pallas_call(
        _mlp_kernel,
        grid=(m // tm, ff // tf),
        in_specs=[
            pl.BlockSpec((tm, d), lambda i, f: (i, 0)),
            pl.BlockSpec((1, d), lambda i, f: (0, 0)),
            pl.BlockSpec((d, tf), lambda i, f: (0, f)),
            pl.BlockSpec((tf, d), lambda i, f: (f, 0)),
            pl.BlockSpec((1, d), lambda i, f: (0, 0)),
        ],
        out_specs=pl.BlockSpec((tm, d), lambda i, f: (i, 0)),
        out_shape=jax.ShapeDtypeStruct((m, d), F32),
        scratch_shapes=[pltpu.VMEM((tm, d), BF16), pltpu.VMEM((tm, d), F32)],
        compiler_params=_params("parallel", "arbitrary"),
        name="mlp_block",
    )(x, g_in.reshape(1, d), w1, w2, g_out.reshape(1, d))


CONV_ROW0 = 8


def _gdn_prep_kernel(proj_ref, conv0_ref, cw_ref, gp_ref,
                     w_ref, uu_ref, qg_ref, kd_ref, qkd_ref, egl_ref, conv_ref, xbuf_ref, *, C, G):
    R = C * G
    prev0 = CONV_ROW0 - (CONV_W - 1)

    @pl.when(pl.program_id(1) == 0)
    def _():
        xbuf_ref[prev0:CONV_ROW0, :] = conv0_ref[0]

    xbuf_ref[CONV_ROW0:CONV_ROW0 + R, :] = proj_ref[0, :, 0:A_QKV]
    cw = cw_ref[...]
    y = xbuf_ref[prev0:prev0 + R, :] * cw[0:1, :]
    for j in range(1, CONV_W):
        y = y + xbuf_ref[prev0 + j:prev0 + j + R, :] * cw[j:j + 1, :]
    y = y * _sigmoid(y)
    last = xbuf_ref[prev0 + R:CONV_ROW0 + R, :]
    xbuf_ref[prev0:CONV_ROW0, :] = last
    conv_ref[0] = last

    tail = proj_ref[0, :, A_GATE_COL:A_IN_PAD]
    gp = gp_ref[...]
    beta = _sigmoid(tail)
    t = tail + gp[1:2, :]
    softplus = jnp.maximum(t, 0.0) + jnp.log(1.0 + jnp.exp(-jnp.abs(t)))
    g = -jnp.exp(gp[0:1, :]) * softplus

    rr = lax.broadcasted_iota(jnp.int32, (R, R), 0)
    cc = lax.broadcasted_iota(jnp.int32, (R, R), 1)
    shift = C.bit_length() - 1
    same = (rr >> shift) == (cc >> shift)
    gc = _dot(jnp.where(same & (rr >= cc), 1.0, 0.0), g, HIGHEST)
    gl = _dot(jnp.where(same, 1.0, 0.0), g, HIGHEST)
    pad_rows = -R % LANES
    gc_t = (jnp.concatenate([gc, jnp.zeros((pad_rows, LANES), F32)], axis=0) if pad_rows else gc).T
    eg = jnp.exp(gc)
    ekd = jnp.exp(gl - gc)
    egl = jnp.exp(gl)
    for gi in range(G):
        egl_ref[0, gi] = egl[gi * C:gi * C + 1, :]

    row = lax.broadcasted_iota(jnp.int32, (C, C), 0)
    col = lax.broadcasted_iota(jnp.int32, (C, C), 1)
    incl = row >= col
    strict = row > col
    eye = (row == col).astype(F32)

    units = [(gi, h) for gi in range(G) for h in range(HEADS)]
    qn, kn = [], []
    for h in range(HEADS):
        q = y[:, h * HEAD_DIM:(h + 1) * HEAD_DIM]
        k = y[:, D_MODEL + h * HEAD_DIM:D_MODEL + (h + 1) * HEAD_DIM]
        qn.append(q * lax.rsqrt(jnp.sum(q * q, axis=-1, keepdims=True) + EPS) * (HEAD_DIM ** -0.5))
        kn.append(k * lax.rsqrt(jnp.sum(k * k, axis=-1, keepdims=True) + EPS))

    def rows(gi):
        return slice(gi * C, (gi + 1) * C)

    decay, qk, a_mat = {}, {}, {}
    for gi, h in units:
        rs, gl_h = rows(gi), HEADS + h
        g_col = gc[rs, gl_h:gl_h + 1]
        g_row = gc_t[gl_h:gl_h + 1, gi * C:(gi + 1) * C]
        decay[gi, h] = jnp.where(incl, jnp.exp(jnp.minimum(g_col - g_row, 0.0)), 0.0)
        k_bf = kn[h][rs].astype(BF16)
        qk_kk = _dot_nt(jnp.concatenate([qn[h][rs].astype(BF16), k_bf], axis=0), k_bf)
        qk[gi, h] = qk_kk[0:C]
        a_mat[gi, h] = jnp.where(strict, beta[rs, h:h + 1] * qk_kk[C:2 * C] * decay[gi, h], 0.0)
    t_inv = {u: eye - a_mat[u] for u in units}
    pw = {}
    for u in units:
        a_s = _split(a_mat[u])
        pw[u] = _dot_split(a_s, a_s)
    n_double = C.bit_length() - 2
    for step in range(n_double):
        for u in units:
            p_s = _split(pw[u])
            t_s = _split(t_inv[u])
            if step + 1 < n_double:
                tp = _dot_split(tuple(jnp.concatenate([t_part, p_part], axis=0) for t_part, p_part in zip(t_s, p_s)),
                                p_s)
                t_inv[u] = t_inv[u] + tp[0:C]
                pw[u] = tp[C:2 * C]
            else:
                t_inv[u] = t_inv[u] + _dot_split(t_s, p_s)
    for gi, h in units:
        rs, gl_h = rows(gi), HEADS + h
        hs = slice(h * HEAD_DIM, (h + 1) * HEAD_DIM)
        b_col = beta[rs, h:h + 1]
        eg_col = eg[rs, gl_h:gl_h + 1]
        k = kn[h][rs]
        v = y[rs, 2 * D_MODEL + h * HEAD_DIM:2 * D_MODEL + (h + 1) * HEAD_DIM]
        rhs = jnp.concatenate([b_col * v, (b_col * eg_col) * k], axis=-1).astype(BF16)
        sol = _dot(t_inv[gi, h].astype(BF16), rhs)
        uu_ref[0, rs, hs] = sol[:, 0:HEAD_DIM].astype(uu_ref.dtype)
        w_ref[0, rs, hs] = sol[:, HEAD_DIM:].astype(w_ref.dtype)
        qg_ref[0, rs, hs] = (eg_col * qn[h][rs]).astype(qg_ref.dtype)
        kd_ref[0, rs, hs] = (ekd[rs, gl_h:gl_h + 1] * k).astype(kd_ref.dtype)
        qkd_ref[0, rs, h * C:(h + 1) * C] = (qk[gi, h] * decay[gi, h]).astype(qkd_ref.dtype)


def gdn_prep(proj, conv0, conv_w, gate_par, *, C, G):
    b, t, _ = proj.shape
    r = C * G
    nc = t // C
    row_spec = pl.BlockSpec((1, r, D_MODEL), lambda i, c: (i, c, 0))
    conv_spec = pl.BlockSpec((1, CONV_W - 1, A_QKV), lambda i, c: (i, 0, 0))
    return pl.pallas_call(
        functools.partial(_gdn_prep_kernel, C=C, G=G),
        grid=(b, t // r),
        in_specs=[
            pl.BlockSpec((1, r, A_IN_PAD), lambda i, c: (i, c, 0)),
            conv_spec,
            pl.BlockSpec((CONV_W, A_QKV), lambda i, c: (0, 0)),
            pl.BlockSpec((2, LANES), lambda i, c: (0, 0)),
        ],
        out_specs=[
            row_spec, row_spec, row_spec, row_spec,
            pl.BlockSpec((1, r, HEADS * C), lambda i, c: (i, c, 0)),
            pl.BlockSpec((1, G, 1, LANES), lambda i, c: (i, c, 0, 0)),
            conv_spec,
        ],
        out_shape=[jax.ShapeDtypeStruct((b, t, D_MODEL), BF16)] * 4 + [
            jax.ShapeDtypeStruct((b, t, HEADS * C), BF16),
            jax.ShapeDtypeStruct((b, nc, 1, LANES), F32),
            jax.ShapeDtypeStruct((b, CONV_W - 1, A_QKV), F32),
        ],
        scratch_shapes=[pltpu.VMEM((CONV_ROW0 + r, A_QKV), F32)],
        compiler_params=_params("arbitrary", "arbitrary"),
        name="gdn_prep",
    )(proj, conv0, conv_w, gate_par)


def _gdn_scan_kernel(w_ref, uu_ref, qg_ref, kd_ref, qkd_ref, egl_ref, z_ref, s0_ref, og_ref, o_ref, s_ref, *, C, G):
    @pl.when(pl.program_id(1) == 0)
    def _():
        s_ref[0] = s0_ref[0]

    heads = range(HEADS)
    for gi in range(G):
        rs = slice(gi * C, (gi + 1) * C)
        egl = egl_ref[0, gi]

        def hs(h):
            return slice(h * HEAD_DIM, (h + 1) * HEAD_DIM)

        s_old = [s_ref[0, h] for h in heads]
        s_bf = [s.astype(BF16) for s in s_old]
        wq = [_dot(jnp.concatenate([w_ref[0, rs, hs(h)], qg_ref[0, rs, hs(h)]], axis=0), s_bf[h]) for h in heads]
        u_bf = [(uu_ref[0, rs, hs(h)].astype(F32) - wq[h][0:C]).astype(BF16) for h in heads]
        o = [wq[h][C:2 * C] + _dot(qkd_ref[0, rs, h * C:(h + 1) * C], u_bf[h]) for h in heads]
        for h in heads:
            s_ref[0, h] = egl[:, HEADS + h:HEADS + h + 1] * s_old[h] + _dot_tn(kd_ref[0, rs, hs(h)], u_bf[h])
        for h in heads:
            z = z_ref[0, rs, hs(h)]
            o_ref[0, rs, hs(h)] = (_rms(o[h], og_ref[...]) * (z * _sigmoid(z))).astype(o_ref.dtype)


def gdn_scan(w, uu, qg, kd, qkd, egl, proj, s0, o_gain, *, C, G):
    b, t, _ = w.shape
    r = C * G
    row_spec = pl.BlockSpec((1, r, D_MODEL), lambda i, c: (i, c, 0))
    state_spec = pl.BlockSpec((1, HEADS, HEAD_DIM, HEAD_DIM), lambda i, c: (i, 0, 0, 0))
    return pl.pallas_call(
        functools.partial(_gdn_scan_kernel, C=C, G=G),
        grid=(b, t // r),
        in_specs=[
            row_spec, row_spec, row_spec, row_spec,
            pl.BlockSpec((1, r, HEADS * C), lambda i, c: (i, c, 0)),
            pl.BlockSpec((1, G, 1, LANES), lambda i, c: (i, c, 0, 0)),
            pl.BlockSpec((1, r, D_MODEL), lambda i, c: (i, c, A_QKV // D_MODEL)),
            state_spec,
            pl.BlockSpec((1, HEAD_DIM), lambda i, c: (0, 0)),
        ],
        out_specs=[row_spec, state_spec],
        out_shape=[
            jax.ShapeDtypeStruct((b, t, D_MODEL), BF16),
            jax.ShapeDtypeStruct((b, HEADS, HEAD_DIM, HEAD_DIM), F32),
        ],
        compiler_params=_params("arbitrary", "arbitrary"),
        name="gdn_scan",
    )(w, uu, qg, kd, qkd, egl, proj, s0, o_gain.reshape(1, HEAD_DIM))


def _lambda(lam_ref, lam_init):
    lp = lam_ref[...]
    return (jnp.exp(jnp.sum(lp[0:1] * lp[1:2], axis=-1, keepdims=True))
            - jnp.exp(jnp.sum(lp[2:3] * lp[3:4], axis=-1, keepdims=True)) + lam_init)


KEY_COLS = 3


def _attn_prompt_kernel(qi_ref, kj_ref, slope_ref, lam_ref, sg_ref, kx_ref, q_ref, k_ref, vt_ref, o_ref,
                        s_even_ref, s_odd_ref, smax_even_ref, smax_odd_ref, diag_ref, m_ref, acc_ref,
                        *, tq, tk, n_pairs, lam_init):
    step = pl.program_id(1)
    slope = slope_ref[0, :, 0:1] * LOG2E
    lane = lax.broadcasted_iota(jnp.int32, (1, HEAD_DIM), 1)
    map_lanes = [lane < MAP_DIM, lane >= MAP_DIM]

    def score_stage(s_ref, smax_ref):
        q = q_ref[...]
        k = k_ref[...]
        for mp in range(2):
            a0 = MAP_DIM if mp == 0 else 0
            extra = (lane >= a0) & (lane < a0 + KEY_COLS)
            qa = jnp.where(map_lanes[mp], q, jnp.where(extra, 1.0, 0.0).astype(BF16))
            ka = jnp.where(map_lanes[mp], k, kx_ref[0, mp])
            scores = _dot_nt(ka, qa)
            s_ref[mp] = scores
            smax_ref[mp] = jnp.max(scores, axis=0, keepdims=True)

    def mask_diagonal(s_ref, smax_ref, key_offset):
        facing = slice(key_offset, key_offset + tk)
        for mp in range(2):
            if key_offset:
                s_ref[mp, :, 0:key_offset] = jnp.full((tk, key_offset), NEG, F32)
                smax_ref[mp, :, 0:key_offset] = jnp.full((1, key_offset), NEG, F32)
            fixed = s_ref[mp, :, facing] + diag_ref[...]
            s_ref[mp, :, facing] = fixed
            smax_ref[mp, :, facing] = jnp.max(fixed, axis=0, keepdims=True)

    def build_diagonal_table():
        j = lax.broadcasted_iota(jnp.int32, (tk, 1), 0)
        i = lax.broadcasted_iota(jnp.int32, (1, tk), 1)
        allowed = (j >> CHUNK_SHIFT) <= (i >> CHUNK_SHIFT)
        fix = jnp.where(j > i, (2.0 * slope) * (i - j).astype(F32), 0.0)
        diag_ref[...] = jnp.where(allowed, fix, NEG)

    def softmax_stage(s_ref, smax_ref, shift_const):
        vt_aug = vt_ref[...]
        probs, alphas = [], []
        for mp in range(2):
            scores = s_ref[mp]
            m_old = m_ref[mp]
            m_new = jnp.maximum(m_old, smax_ref[mp] + shift_const)
            probs.append(jnp.exp2(scores - (m_new - shift_const)).astype(BF16))
            alphas.append(jnp.exp2(m_old - m_new))
            m_ref[mp] = m_new
        pv = [_dot(vt_aug, probs[mp]) for mp in range(2)]
        for mp in range(2):
            acc_ref[mp] = alphas[mp] * acc_ref[mp] + pv[mp]

    def finalize():
        lam = _lambda(lam_ref, lam_init)
        a0 = acc_ref[0]
        a1 = acc_ref[1]
        o_t = (a0[0:HEAD_DIM] / a0[HEAD_DIM:HEAD_DIM + 1]
               - lam * (a1[0:HEAD_DIM] / a1[HEAD_DIM:HEAD_DIM + 1]))
        o_t = o_t * (lax.rsqrt(jnp.mean(o_t * o_t, axis=0, keepdims=True) + EPS) * (1.0 - lam_init))
        o_ref[...] = (o_t.T * sg_ref[...]).astype(o_ref.dtype)

    scored = jnp.minimum(step, n_pairs - 1)
    key_offset_s = kj_ref[scored] * tk - qi_ref[scored] * tq
    consumed = jnp.maximum(step - 1, 0)
    kj_c = kj_ref[consumed]
    key_offset_c = kj_c * tk - qi_ref[consumed] * tq

    @pl.when(step == 0)
    def _():
        build_diagonal_table()
        score_stage(s_even_ref, smax_even_ref)

    @pl.when((step > 0) & (kj_c == 0))
    def _():
        m_ref[...] = jnp.full_like(m_ref, NEG)
        acc_ref[...] = jnp.zeros_like(acc_ref)

    tile_shift = slope * key_offset_c.astype(F32)
    even, odd = (s_even_ref, smax_even_ref), (s_odd_ref, smax_odd_ref)
    for parity, (written, read) in enumerate([(even, odd), (odd, even)]):
        @pl.when((step > 0) & (step % 2 == parity))
        def _():
            score_stage(*written)
            softmax_stage(*read, tile_shift)

    for parity, written in enumerate([even, odd]):
        for key_offset in range(0, tq, tk):
            @pl.when((key_offset_s == key_offset) & (step % 2 == parity))
            def _():
                mask_diagonal(*written, key_offset)

    @pl.when((step > 0) & (key_offset_c == tq - tk))
    def _():
        finalize()


def alibi_slopes():
    return 2.0 ** (-8.0 * np.arange(1, HEADS + 1, dtype=np.float32) / HEADS)


def alibi_key_columns(t):
    term = (np.float32(LOG2E) * alibi_slopes())[:, None] * np.arange(t, dtype=np.float32)[None, :]
    cols = np.zeros((HEADS, 2, t, HEAD_DIM), np.float32)
    for c in range(KEY_COLS):
        piece = term.astype(BF16).astype(np.float32)
        term = term - piece
        for mp in range(2):
            a0 = MAP_DIM if mp == 0 else 0
            cols[:, mp, :, a0 + c] = piece
    return jnp.asarray(cols.astype(BF16))


def diff_attention_prompt(q, k, v_t, slopes, key_cols, lam_p, sub_gain, *, tq, tk, lam_init):
    t_all = q.shape[0]
    assert tq % tk == 0 and tk % CHUNK == 0 and t_all % tq == 0
    per_q = tq // tk
    qi = np.concatenate([np.full((i + 1) * per_q, i, np.int32) for i in range(t_all // tq)])
    kj = np.concatenate([np.arange((i + 1) * per_q, dtype=np.int32) for i in range(t_all // tq)])
    n_pairs = len(qi)

    def scored(p):
        return jnp.minimum(p, n_pairs - 1)

    def consumed(p):
        return jnp.maximum(p - 1, 0)

    grid_spec = pltpu.PrefetchScalarGridSpec(
        num_scalar_prefetch=2,
        grid=(HEADS, n_pairs + 1),
        in_specs=[
            pl.BlockSpec((1, 1, LANES), lambda h, p, qi_r, kj_r: (h, 0, 0)),
            pl.BlockSpec((4, MAP_DIM), lambda h, p, qi_r, kj_r: (0, 0)),
            pl.BlockSpec((1, HEAD_DIM), lambda h, p, qi_r, kj_r: (0, 0)),
            pl.BlockSpec((1, 2, tk, HEAD_DIM), lambda h, p, qi_r, kj_r: (h, 0, 0, 0)),
            pl.BlockSpec((tq, HEAD_DIM), lambda h, p, qi_r, kj_r: (qi_r[scored(p)], h)),
            pl.BlockSpec((tk, HEAD_DIM), lambda h, p, qi_r, kj_r: (kj_r[scored(p)], h)),
            pl.BlockSpec((HEAD_ROWS_T, tk), lambda h, p, qi_r, kj_r: (h, kj_r[consumed(p)])),
        ],
        out_specs=pl.BlockSpec((tq, HEAD_DIM), lambda h, p, qi_r, kj_r: (qi_r[consumed(p)], h)),
        scratch_shapes=[
            pltpu.VMEM((2, tk, tq), F32), pltpu.VMEM((2, tk, tq), F32),
            pltpu.VMEM((2, 1, tq), F32), pltpu.VMEM((2, 1, tq), F32),
            pltpu.VMEM((tk, tk), F32),
            pltpu.VMEM((2, 1, tq), F32), pltpu.VMEM((2, HEAD_ROWS_T, tq), F32),
        ],
    )
    return pl.pallas_call(
        functools.partial(_attn_prompt_kernel, tq=tq, tk=tk, n_pairs=n_pairs, lam_init=lam_init),
        grid_spec=grid_spec,
        out_shape=jax.ShapeDtypeStruct((t_all, D_MODEL), BF16),
        compiler_params=_params("arbitrary", "arbitrary"),
        name="diff_attention_prompt",
    )(jnp.asarray(qi), jnp.asarray(kj), slopes, lam_p, sub_gain.reshape(1, HEAD_DIM), key_cols, q, k, v_t)


def _attn_decode_kernel(slope_ref, lam_ref, sg_ref, q_ref, k_ref, v_ref, kn_ref, vn_ref, o_ref, m_ref, l_ref, acc_ref,
                        *, tk, pos0, lam_init):
    j = pl.program_id(1)
    nk = pl.num_programs(1)
    n = q_ref.shape[1]

    @pl.when(j == 0)
    def _():
        m_ref[...] = jnp.full_like(m_ref, NEG)
        l_ref[...] = jnp.zeros_like(l_ref)
        acc_ref[...] = jnp.zeros_like(acc_ref)

    row = lax.broadcasted_iota(jnp.int32, (2 * n, 1), 0)
    q_pos = pos0 + jnp.where(row < n, row, row - n)
    lane = lax.broadcasted_iota(jnp.int32, (1, HEAD_DIM), 1)
    map_of_row_has_lane = (row < n) == (lane < MAP_DIM)

    def update(keys, values, k0, n_keys):
        heads = range(HEADS)
        k_pos = k0 + lax.broadcasted_iota(jnp.int32, (1, n_keys), 1)
        allowed = (k_pos >> CHUNK_SHIFT) <= (q_pos >> CHUNK_SHIFT)
        dist = jnp.abs(q_pos - k_pos).astype(F32)
        scores = []
        for h in heads:
            slope = slope_ref[h, :, 0:1] * LOG2E
            q = q_ref[0, :, h * HEAD_DIM:(h + 1) * HEAD_DIM]
            q2 = jnp.concatenate([q, q], axis=0)
            q_stack = jnp.where(map_of_row_has_lane, q2, jnp.zeros_like(q2))
            scores.append(jnp.where(allowed, _dot_nt(q_stack, keys(h).astype(BF16)) - slope * dist, NEG))
        probs, alphas = [], []
        for h in heads:
            m_old = m_ref[h]
            m_new = jnp.maximum(m_old, jnp.max(scores[h], axis=-1, keepdims=True))
            alphas.append(jnp.exp2(m_old - m_new))
            p = jnp.exp2(scores[h] - m_new)
            l_ref[h] = alphas[h] * l_ref[h] + jnp.sum(p, axis=-1, keepdims=True)
            probs.append(p.astype(BF16))
            m_ref[h] = m_new
        pv = [_dot(probs[h], values(h).astype(BF16)) for h in heads]
        for h in heads:
            acc_ref[h] = alphas[h] * acc_ref[h] + pv[h]

    def head_rows(h):
        return pl.ds(h, tk, stride=HEADS)

    update(lambda h: k_ref[0, head_rows(h), :], lambda h: v_ref[0, head_rows(h), :], j * tk, tk)

    @pl.when(j == nk - 1)
    def _():
        update(lambda h: kn_ref[0, :, h * HEAD_DIM:(h + 1) * HEAD_DIM],
               lambda h: vn_ref[0, :, h * HEAD_DIM:(h + 1) * HEAD_DIM], pos0, n)
        lam = _lambda(lam_ref, lam_init)
        for h in range(HEADS):
            o = acc_ref[h] / l_ref[h]
            o = o[0:n] - lam * o[n:2 * n]
            o_ref[0, :, h * HEAD_DIM:(h + 1) * HEAD_DIM] = (_rms(o, sg_ref[...]) * (1.0 - lam_init)).astype(o_ref.dtype)


def diff_attention_decode(q, k, v, k_new, v_new, slopes, lam_p, sub_gain, *, tk, lam_init):
    b, n, _ = q.shape
    t_k = k.shape[1]
    assert t_k % tk == 0 and t_k % CHUNK == 0
    k = k.reshape(b, t_k * HEADS, HEAD_DIM)
    v = v.reshape(b, t_k * HEADS, HEAD_DIM)
    row_block = pl.BlockSpec((1, n, D_MODEL), lambda bi, j: (bi, 0, 0))
    kv_spec = pl.BlockSpec((1, tk * HEADS, HEAD_DIM), lambda bi, j: (bi, j, 0))
    return pl.pallas_call(
        functools.partial(_attn_decode_kernel, tk=tk, pos0=t_k, lam_init=lam_init),
        grid=(b, t_k // tk),
        in_specs=[
            pl.BlockSpec((HEADS, 1, LANES), lambda bi, j: (0, 0, 0)),
            pl.BlockSpec((4, MAP_DIM), lambda bi, j: (0, 0)),
            pl.BlockSpec((1, HEAD_DIM), lambda bi, j: (0, 0)),
            row_block, kv_spec, kv_spec, row_block, row_block,
        ],
        out_specs=row_block,
        out_shape=jax.ShapeDtypeStruct((b, n, D_MODEL), BF16),
        scratch_shapes=[
            pltpu.VMEM((HEADS, 2 * n, 1), F32),
            pltpu.VMEM((HEADS, 2 * n, 1), F32),
            pltpu.VMEM((HEADS, 2 * n, HEAD_DIM), F32),
        ],
        compiler_params=_params("arbitrary", "arbitrary"),
        name="diff_attention_decode",
    )(slopes, lam_p, sub_gain.reshape(1, HEAD_DIM), q, k, v, k_new, v_new)


def _trunk(x, conv_state, ssm_state, past_k, past_v, wts, *, tm, gdn_chunk, gdn_group, attn_tile, attn_q_tile=None):
    b, t, _ = x.shape
    m = b * t
    xf = x.reshape(m, D_MODEL)
    new_conv, new_ssm = [], []
    k_new = v_new = k_bf = v_t = None
    for l in range(DEPTH):
        gains = wts["norm_gains"][l]
        if l < N_A:
            proj = norm_matmul(xf, gains[0], wts["a_w_in"][l], tm=tm, tn=A_IN_PAD // 3).reshape(b, t, A_IN_PAD)
            w, uu, qg, kd, qkd, egl, conv_l = gdn_prep(proj, conv_state[l], wts["a_conv_w"][l], wts["gate_par"][l],
                                                       C=gdn_chunk, G=gdn_group)
            og, s_l = gdn_scan(w, uu, qg, kd, qkd, egl, proj, ssm_state[l], wts["a_o_gain"][l],
                               C=gdn_chunk, G=gdn_group)
            new_conv.append(conv_l)
            new_ssm.append(s_l)
            xf = matmul_norm_res(og.reshape(m, D_MODEL), wts["a_w_out"][l], gains[1], xf, tm=tm)
        else:
            j = l - N_A
            prompt = past_k is None
            if j == 0:
                k_new, k_bf = norm_matmul(xf, wts["kv_gain"], wts["w_k"], tm=tm, tn=D_MODEL, out_dtypes=(F32, BF16))
                v_new = norm_matmul(xf, wts["kv_gain"], wts["w_v"], tm=tm, tn=D_MODEL, head_transposed_copy=prompt)
                if prompt:
                    v_new, v_t = v_new
            q = norm_matmul(xf, gains[0], wts["b_w_q"][j], tm=tm, tn=D_MODEL, out_dtypes=(BF16,), out_scale=Q_SCALE)
            lam_init = 0.8 - 0.6 * math.exp(-0.3 * l)
            tail = (wts["b_lam"][j], wts["b_sub_gain"][j])
            if prompt:
                assert b == 1
                o = diff_attention_prompt(q, k_bf, v_t, wts["slopes"], alibi_key_columns(attn_tile),
                                          *tail, tq=attn_q_tile, tk=attn_tile, lam_init=lam_init)
            else:
                o = diff_attention_decode(q.reshape(b, t, D_MODEL), past_k, past_v, k_new.reshape(b, t, D_MODEL),
                                          v_new.reshape(b, t, D_MODEL), wts["slopes"], *tail, tk=attn_tile,
                                          lam_init=lam_init)
            xf = matmul_norm_res(o.reshape(m, D_MODEL), wts["b_w_out"][j], gains[1], xf, tm=tm)
        xf = mlp_block(xf, gains[2], wts["mlp_w1"][l], wts["mlp_w2"][l], gains[3], tm=tm, tf=1024)
    kv_shape = (b, t, HEADS, HEAD_DIM)
    return (xf.reshape(b, t, D_MODEL), jnp.stack(new_conv), jnp.stack(new_ssm),
            k_new.reshape(kv_shape), v_new.reshape(kv_shape))


def kernel(x_prompt, x_sample, state_conv, state_ssm, cache_k, cache_v, norm_gains, a_w_in, a_conv_w, a_log,
           a_dt_bias, a_o_gain, a_w_out, kv_gain, w_kv, b_w_q, b_lam, b_sub_gain, b_w_out, mlp_w1, mlp_w2):
    a_in = a_w_in.shape[-1]
    gate_par = jnp.zeros((N_A, 2, LANES), F32)
    gate_par = gate_par.at[:, 0, HEADS:2 * HEADS].set(a_log.astype(F32))
    gate_par = gate_par.at[:, 1, HEADS:2 * HEADS].set(a_dt_bias.astype(F32))
    slopes = jnp.asarray(alibi_slopes())
    wts = dict(
        norm_gains=norm_gains.astype(F32),
        a_w_in=jnp.pad(a_w_in, ((0, 0), (0, 0), (0, A_IN_PAD - a_in))).astype(BF16),
        a_conv_w=a_conv_w.astype(F32),
        gate_par=gate_par,
        a_o_gain=a_o_gain.astype(F32),
        a_w_out=a_w_out.astype(BF16),
        kv_gain=kv_gain.astype(F32),
        w_k=w_kv[:, :D_MODEL].astype(BF16),
        w_v=w_kv[:, D_MODEL:].astype(BF16),
        b_w_q=b_w_q.astype(BF16),
        b_lam=b_lam.astype(F32),
        b_sub_gain=b_sub_gain.astype(F32),
        b_w_out=b_w_out.astype(BF16),
        mlp_w1=mlp_w1.astype(BF16),
        mlp_w2=mlp_w2.astype(BF16),
        slopes=jnp.broadcast_to(slopes[:, None, None], (HEADS, 1, LANES)),
    )
    bp, tp, _ = x_prompt.shape
    conv0 = jnp.zeros((N_A, bp, CONV_W - 1, A_QKV), F32)
    ssm0 = jnp.zeros((N_A, bp, HEADS, HEAD_DIM, HEAD_DIM), F32)
    y_p, p_conv, p_ssm, p_k, p_v = _trunk(x_prompt, conv0, ssm0, None, None, wts,
                                          tm=512, gdn_chunk=CHUNK, gdn_group=4, attn_tile=512, attn_q_tile=1024)
    bs, ts, _ = x_sample.shape
    y_s, s_conv, s_ssm, s_k, s_v = _trunk(x_sample, state_conv, state_ssm, cache_k, cache_v,
                                          wts, tm=bs * ts, gdn_chunk=ts, gdn_group=1, attn_tile=512)
    return (y_p, y_s, p_conv, p_ssm, p_k, p_v, s_conv, s_ssm, s_k, s_v)
```

```python
import functools
import math

import jax
import jax.numpy as jnp
import numpy as np
from jax import lax
from jax.experimental import pallas as pl
from jax.experimental.pallas import tpu as pltpu

F32 = jnp.float32
BF16 = jnp.bfloat16

D_MODEL = 1024
DEPTH = 4
N_A = DEPTH // 2
CHUNK = 64
CHUNK_SHIFT = CHUNK.bit_length() - 1
assert 1 << CHUNK_SHIFT == CHUNK
HEADS = 8
HEAD_DIM = D_MODEL // HEADS
CONV_W = 4
A_QKV = 3 * D_MODEL
A_GATE_COL = A_QKV + D_MODEL
LANES = 128
A_IN_PAD = A_GATE_COL + LANES
MAP_DIM = HEAD_DIM // 2
D_FF = 4 * D_MODEL
EPS = 1e-6
NEG = -1e30
VMEM_LIMIT = 48 * 1024 * 1024
HIGHEST = lax.Precision.HIGHEST
ONES_ROWS = 16
HEAD_ROWS_T = HEAD_DIM + ONES_ROWS
LOG2E = math.log2(math.e)
Q_SCALE = MAP_DIM ** -0.5 * LOG2E


def _dot(a, b, precision=None):
    return jnp.dot(a, b, preferred_element_type=F32, precision=precision)


def _dot_nt(a, b):
    return lax.dot_general(a, b, (((1,), (1,)), ((), ())), preferred_element_type=F32)


def _dot_tn(a, b):
    return lax.dot_general(a, b, (((0,), (0,)), ((), ())), preferred_element_type=F32)


def _split(a):
    hi = a.astype(BF16)
    return hi, (a - hi.astype(F32)).astype(BF16)


def _dot_split(a, b):
    return _dot(a[0], b[0]) + (_dot(a[0], b[1]) + _dot(a[1], b[0]))


def _rms(x, gain):
    return x * lax.rsqrt(jnp.mean(x * x, axis=-1, keepdims=True) + EPS) * gain


def _sigmoid(x):
    return 1.0 / (1.0 + jnp.exp(-x))


def _params(*semantics):
    return pltpu.CompilerParams(dimension_semantics=semantics, vmem_limit_bytes=VMEM_LIMIT)


def _norm_matmul_kernel(x_ref, g_ref, w_ref, *out_and_scratch, out_scale, head_transposed_copy):
    *o_refs, hn_ref = out_and_scratch

    @pl.when(pl.program_id(1) == 0)
    def _():
        hn_ref[...] = _rms(x_ref[...], g_ref[...]).astype(BF16)

    y = _dot(hn_ref[...], w_ref[...])
    if out_scale != 1.0:
        y = y * out_scale
    if head_transposed_copy:
        *o_refs, ot_ref = o_refs
        y_t = y.T.astype(ot_ref.dtype)
        for h in range(y_t.shape[0] // HEAD_DIM):
            r0 = h * HEAD_ROWS_T
            ot_ref[r0:r0 + HEAD_DIM, :] = y_t[h * HEAD_DIM:(h + 1) * HEAD_DIM]
            ot_ref[r0 + HEAD_DIM:r0 + HEAD_ROWS_T, :] = jnp.ones((ONES_ROWS, y_t.shape[1]), ot_ref.dtype)
    for o_ref in o_refs:
        o_ref[...] = y.astype(o_ref.dtype)


def norm_matmul(x, gain, w, *, tm, tn, out_dtypes=(F32,), out_scale=1.0, head_transposed_copy=False):
    m, k = x.shape
    n = w.shape[1]
    out_specs = [pl.BlockSpec((tm, tn), lambda i, j: (i, j)) for _ in out_dtypes]
    out_shape = [jax.ShapeDtypeStruct((m, n), dt) for dt in out_dtypes]
    if head_transposed_copy:
        out_specs.append(pl.BlockSpec((tn // HEAD_DIM * HEAD_ROWS_T, tm), lambda i, j: (j, i)))
        out_shape.append(jax.ShapeDtypeStruct((n // HEAD_DIM * HEAD_ROWS_T, m), BF16))
    outs = pl.pallas_call(
        functools.partial(_norm_matmul_kernel, out_scale=out_scale, head_transposed_copy=head_transposed_copy),
        grid=(m // tm, n // tn),
        in_specs=[
            pl.BlockSpec((tm, k), lambda i, j: (i, 0)),
            pl.BlockSpec((1, k), lambda i, j: (0, 0)),
            pl.BlockSpec((k, tn), lambda i, j: (0, j)),
        ],
        out_specs=out_specs,
        out_shape=out_shape,
        scratch_shapes=[pltpu.VMEM((tm, k), BF16)],
        compiler_params=_params("parallel", "arbitrary"),
        name="norm_matmul",
    )(x, gain.reshape(1, k), w)
    return outs[0] if len(outs) == 1 else outs


def _matmul_norm_res_kernel(a_ref, w_ref, g_ref, x_ref, o_ref):
    y = _dot(a_ref[...], w_ref[...])
    o_ref[...] = x_ref[...] + _rms(y, g_ref[...])


def matmul_norm_res(a, w, gain, x, *, tm):
    m, k = a.shape
    d = w.shape[1]
    return pl.pallas_call(
        _matmul_norm_res_kernel,
        grid=(m // tm,),
        in_specs=[
            pl.BlockSpec((tm, k), lambda i: (i, 0)),
            pl.BlockSpec((k, d), lambda i: (0, 0)),
            pl.BlockSpec((1, d), lambda i: (0, 0)),
            pl.BlockSpec((tm, d), lambda i: (i, 0)),
        ],
        out_specs=pl.BlockSpec((tm, d), lambda i: (i, 0)),
        out_shape=jax.ShapeDtypeStruct((m, d), F32),
        compiler_params=_params("parallel"),
        name="matmul_norm_res",
    )(a, w, gain.reshape(1, d), x)


def _mlp_kernel(x_ref, g_in_ref, w1_ref, w2_ref, g_out_ref, o_ref, hn_ref, acc_ref):
    f = pl.program_id(1)

    @pl.when(f == 0)
    def _():
        hn_ref[...] = _rms(x_ref[...], g_in_ref[...]).astype(BF16)
        acc_ref[...] = jnp.zeros_like(acc_ref)

    h = _dot(hn_ref[...], w1_ref[...])
    h = jnp.square(jnp.maximum(h, 0.0)).astype(BF16)
    acc_ref[...] += _dot(h, w2_ref[...])

    @pl.when(f == pl.num_programs(1) - 1)
    def _():
        o_ref[...] = x_ref[...] + _rms(acc_ref[...], g_out_ref[...])


def mlp_block(x, g_in, w1, w2, g_out, *, tm, tf):
    m, d = x.shape
    ff = w1.shape[1]
    return pl.pallas_call(
        _mlp_kernel,
        grid=(m // tm, ff // tf),
        in_specs=[
            pl.BlockSpec((tm, d), lambda i, f: (i, 0)),
            pl.BlockSpec((1, d), lambda i, f: (0, 0)),
            pl.BlockSpec((d, tf), lambda i, f: (0, f)),
            pl.BlockSpec((tf, d), lambda i, f: (f, 0)),
            pl.BlockSpec((1, d), lambda i, f: (0, 0)),
        ],
        out_specs=pl.BlockSpec((tm, d), lambda i, f: (i, 0)),
        out_shape=jax.ShapeDtypeStruct((m, d), F32),
        scratch_shapes=[pltpu.VMEM((tm, d), BF16), pltpu.VMEM((tm, d), F32)],
        compiler_params=_params("parallel", "arbitrary"),
        name="mlp_block",
    )(x, g_in.reshape(1, d), w1, w2, g_out.reshape(1, d))


CONV_ROW0 = 8


def _gdn_prep_kernel(proj_ref, conv0_ref, cw_ref, gp_ref,
                     w_ref, uu_ref, qg_ref, kd_ref, qkd_ref, egl_ref, conv_ref, xbuf_ref, *, C, G):
    R = C * G
    prev0 = CONV_ROW0 - (CONV_W - 1)

    @pl.when(pl.program_id(1) == 0)
    def _():
        xbuf_ref[prev0:CONV_ROW0, :] = conv0_ref[0]

    xbuf_ref[CONV_ROW0:CONV_ROW0 + R, :] = proj_ref[0, :, 0:A_QKV]
    cw = cw_ref[...]
    y = xbuf_ref[prev0:prev0 + R, :] * cw[0:1, :]
    for j in range(1, CONV_W):
        y = y + xbuf_ref[prev0 + j:prev0 + j + R, :] * cw[j:j + 1, :]
    y = y * _sigmoid(y)
    last = xbuf_ref[prev0 + R:CONV_ROW0 + R, :]
    xbuf_ref[prev0:CONV_ROW0, :] = last
    conv_ref[0] = last

    tail = proj_ref[0, :, A_GATE_COL:A_IN_PAD]
    gp = gp_ref[...]
    beta = _sigmoid(tail)
    t = tail + gp[1:2, :]
    softplus = jnp.maximum(t, 0.0) + jnp.log(1.0 + jnp.exp(-jnp.abs(t)))
    g = -jnp.exp(gp[0:1, :]) * softplus

    rr = lax.broadcasted_iota(jnp.int32, (R, R), 0)
    cc = lax.broadcasted_iota(jnp.int32, (R, R), 1)
    shift = C.bit_length() - 1
    same = (rr >> shift) == (cc >> shift)
    gc = _dot(jnp.where(same & (rr >= cc), 1.0, 0.0), g, HIGHEST)
    gl = _dot(jnp.where(same, 1.0, 0.0), g, HIGHEST)
    pad_rows = -R % LANES
    gc_t = (jnp.concatenate([gc, jnp.zeros((pad_rows, LANES), F32)], axis=0) if pad_rows else gc).T
    eg = jnp.exp(gc)
    ekd = jnp.exp(gl - gc)
    egl = jnp.exp(gl)
    for gi in range(G):
        egl_ref[0, gi] = egl[gi * C:gi * C + 1, :]

    row = lax.broadcasted_iota(jnp.int32, (C, C), 0)
    col = lax.broadcasted_iota(jnp.int32, (C, C), 1)
    incl = row >= col
    strict = row > col
    eye = (row == col).astype(F32)

    units = [(gi, h) for gi in range(G) for h in range(HEADS)]
    qn, kn = [], []
    for h in range(HEADS):
        q = y[:, h * HEAD_DIM:(h + 1) * HEAD_DIM]
        k = y[:, D_MODEL + h * HEAD_DIM:D_MODEL + (h + 1) * HEAD_DIM]
        qn.append(q * lax.rsqrt(jnp.sum(q * q, axis=-1, keepdims=True) + EPS) * (HEAD_DIM ** -0.5))
        kn.append(k * lax.rsqrt(jnp.sum(k * k, axis=-1, keepdims=True) + EPS))

    def rows(gi):
        return slice(gi * C, (gi + 1) * C)

    decay, qk, a_mat = {}, {}, {}
    for gi, h in units:
        rs, gl_h = rows(gi), HEADS + h
        g_col = gc[rs, gl_h:gl_h + 1]
        g_row = gc_t[gl_h:gl_h + 1, gi * C:(gi + 1) * C]
        decay[gi, h] = jnp.where(incl, jnp.exp(jnp.minimum(g_col - g_row, 0.0)), 0.0)
        k_bf = kn[h][rs].astype(BF16)
        qk_kk = _dot_nt(jnp.concatenate([qn[h][rs].astype(BF16), k_bf], axis=0), k_bf)
        qk[gi, h] = qk_kk[0:C]
        a_mat[gi, h] = jnp.where(strict, beta[rs, h:h + 1] * qk_kk[C:2 * C] * decay[gi, h], 0.0)
    t_inv = {u: eye - a_mat[u] for u in units}
    pw = {}
    for u in units:
        a_s = _split(a_mat[u])
        pw[u] = _dot_split(a_s, a_s)
    n_double = C.bit_length() - 2
    for step in range(n_double):
        for u in units:
            p_s = _split(pw[u])
            t_s = _split(t_inv[u])
            if step + 1 < n_double:
                tp = _dot_split(tuple(jnp.concatenate([t_part, p_part], axis=0) for t_part, p_part in zip(t_s, p_s)),
                                p_s)
                t_inv[u] = t_inv[u] + tp[0:C]
                pw[u] = tp[C:2 * C]
            else:
                t_inv[u] = t_inv[u] + _dot_split(t_s, p_s)
    for gi, h in units:
        rs, gl_h = rows(gi), HEADS + h
        hs = slice(h * HEAD_DIM, (h + 1) * HEAD_DIM)
        b_col = beta[rs, h:h + 1]
        eg_col = eg[rs, gl_h:gl_h + 1]
        k = kn[h][rs]
        v = y[rs, 2 * D_MODEL + h * HEAD_DIM:2 * D_MODEL + (h + 1) * HEAD_DIM]
        rhs = jnp.concatenate([b_col * v, (b_col * eg_col) * k], axis=-1).astype(BF16)
        sol = _dot(t_inv[gi, h].astype(BF16), rhs)
        uu_ref[0, rs, hs] = sol[:, 0:HEAD_DIM].astype(uu_ref.dtype)
        w_ref[0, rs, hs] = sol[:, HEAD_DIM:].astype(w_ref.dtype)
        qg_ref[0, rs, hs] = (eg_col * qn[h][rs]).astype(qg_ref.dtype)
        kd_ref[0, rs, hs] = (ekd[rs, gl_h:gl_h + 1] * k).astype(kd_ref.dtype)
        qkd_ref[0, rs, h * C:(h + 1) * C] = (qk[gi, h] * decay[gi, h]).astype(qkd_ref.dtype)


def gdn_prep(proj, conv0, conv_w, gate_par, *, C, G):
    b, t, _ = proj.shape
    r = C * G
    nc = t // C
    row_spec = pl.BlockSpec((1, r, D_MODEL), lambda i, c: (i, c, 0))
    conv_spec = pl.BlockSpec((1, CONV_W - 1, A_QKV), lambda i, c: (i, 0, 0))
    return pl.pallas_call(
        functools.partial(_gdn_prep_kernel, C=C, G=G),
        grid=(b, t // r),
        in_specs=[
            pl.BlockSpec((1, r, A_IN_PAD), lambda i, c: (i, c, 0)),
            conv_spec,
            pl.BlockSpec((CONV_W, A_QKV), lambda i, c: (0, 0)),
            pl.BlockSpec((2, LANES), lambda i, c: (0, 0)),
        ],
        out_specs=[
            row_spec, row_spec, row_spec, row_spec,
            pl.BlockSpec((1, r, HEADS * C), lambda i, c: (i, c, 0)),
            pl.BlockSpec((1, G, 1, LANES), lambda i, c: (i, c, 0, 0)),
            conv_spec,
        ],
        out_shape=[jax.ShapeDtypeStruct((b, t, D_MODEL), BF16)] * 4 + [
            jax.ShapeDtypeStruct((b, t, HEADS * C), BF16),
            jax.ShapeDtypeStruct((b, nc, 1, LANES), F32),
            jax.ShapeDtypeStruct((b, CONV_W - 1, A_QKV), F32),
        ],
        scratch_shapes=[pltpu.VMEM((CONV_ROW0 + r, A_QKV), F32)],
        compiler_params=_params("arbitrary", "arbitrary"),
        name="gdn_prep",
    )(proj, conv0, conv_w, gate_par)


def _gdn_scan_kernel(w_ref, uu_ref, qg_ref, kd_ref, qkd_ref, egl_ref, z_ref, s0_ref, og_ref, o_ref, s_ref, *, C, G):
    @pl.when(pl.program_id(1) == 0)
    def _():
        s_ref[0] = s0_ref[0]

    heads = range(HEADS)
    for gi in range(G):
        rs = slice(gi * C, (gi + 1) * C)
        egl = egl_ref[0, gi]

        def hs(h):
            return slice(h * HEAD_DIM, (h + 1) * HEAD_DIM)

        s_old = [s_ref[0, h] for h in heads]
        s_bf = [s.astype(BF16) for s in s_old]
        wq = [_dot(jnp.concatenate([w_ref[0, rs, hs(h)], qg_ref[0, rs, hs(h)]], axis=0), s_bf[h]) for h in heads]
        u_bf = [(uu_ref[0, rs, hs(h)].astype(F32) - wq[h][0:C]).astype(BF16) for h in heads]
        o = [wq[h][C:2 * C] + _dot(qkd_ref[0, rs, h * C:(h + 1) * C], u_bf[h]) for h in heads]
        for h in heads:
            s_ref[0, h] = egl[:, HEADS + h:HEADS + h + 1] * s_old[h] + _dot_tn(kd_ref[0, rs, hs(h)], u_bf[h])
        for h in heads:
            z = z_ref[0, rs, hs(h)]
            o_ref[0, rs, hs(h)] = (_rms(o[h], og_ref[...]) * (z * _sigmoid(z))).astype(o_ref.dtype)


def gdn_scan(w, uu, qg, kd, qkd, egl, proj, s0, o_gain, *, C, G):
    b, t, _ = w.shape
    r = C * G
    row_spec = pl.BlockSpec((1, r, D_MODEL), lambda i, c: (i, c, 0))
    state_spec = pl.BlockSpec((1, HEADS, HEAD_DIM, HEAD_DIM), lambda i, c: (i, 0, 0, 0))
    return pl.pallas_call(
        functools.partial(_gdn_scan_kernel, C=C, G=G),
        grid=(b, t // r),
        in_specs=[
            row_spec, row_spec, row_spec, row_spec,
            pl.BlockSpec((1, r, HEADS * C), lambda i, c: (i, c, 0)),
            pl.BlockSpec((1, G, 1, LANES), lambda i, c: (i, c, 0, 0)),
            pl.BlockSpec((1, r, D_MODEL), lambda i, c: (i, c, A_QKV // D_MODEL)),
            state_spec,
            pl.BlockSpec((1, HEAD_DIM), lambda i, c: (0, 0)),
        ],
        out_specs=[row_spec, state_spec],
        out_shape=[
            jax.ShapeDtypeStruct((b, t, D_MODEL), BF16),
            jax.ShapeDtypeStruct((b, HEADS, HEAD_DIM, HEAD_DIM), F32),
        ],
        compiler_params=_params("arbitrary", "arbitrary"),
        name="gdn_scan",
    )(w, uu, qg, kd, qkd, egl, proj, s0, o_gain.reshape(1, HEAD_DIM))


def _lambda(lam_ref, lam_init):
    lp = lam_ref[...]
    return (jnp.exp(jnp.sum(lp[0:1] * lp[1:2], axis=-1, keepdims=True))
            - jnp.exp(jnp.sum(lp[2:3] * lp[3:4], axis=-1, keepdims=True)) + lam_init)


KEY_COLS = 3


def _attn_prompt_kernel(qi_ref, kj_ref, slope_ref, lam_ref, sg_ref, kx_ref, q_ref, k_ref, vt_ref, o_ref,
                        s_even_ref, s_odd_ref, smax_even_ref, smax_odd_ref, diag_ref, m_ref, acc_ref,
                        *, tq, tk, col_block, n_pairs, lam_init):
    step = pl.program_id(1)
    slope = slope_ref[0, :, 0:1] * LOG2E
    lane = lax.broadcasted_iota(jnp.int32, (1, HEAD_DIM), 1)
    map_lanes = [lane < MAP_DIM, lane >= MAP_DIM]

    def score_stage(s_ref, smax_ref, mp, cols):
        q = q_ref[cols, :]
        k = k_ref[...]
        a0 = MAP_DIM if mp == 0 else 0
        extra = (lane >= a0) & (lane < a0 + KEY_COLS)
        qa = jnp.where(map_lanes[mp], q, jnp.where(extra, 1.0, 0.0).astype(BF16))
        ka = jnp.where(map_lanes[mp], k, kx_ref[0, mp])
        scores = _dot_nt(ka, qa)
        s_ref[mp, :, cols] = scores
        smax_ref[mp, :, cols] = jnp.max(scores, axis=0, keepdims=True)

    def mask_diagonal(s_ref, smax_ref, key_offset):
        facing = slice(key_offset, key_offset + tk)
        for mp in range(2):
            if key_offset:
                s_ref[mp, :, 0:key_offset] = jnp.full((tk, key_offset), NEG, F32)
                smax_ref[mp, :, 0:key_offset] = jnp.full((1, key_offset), NEG, F32)
            fixed = s_ref[mp, :, facing] + diag_ref[...]
            s_ref[mp, :, facing] = fixed
            smax_ref[mp, :, facing] = jnp.max(fixed, axis=0, keepdims=True)

    def build_diagonal_table():
        j = lax.broadcasted_iota(jnp.int32, (tk, 1), 0)
        i = lax.broadcasted_iota(jnp.int32, (1, tk), 1)
        allowed = (j >> CHUNK_SHIFT) <= (i >> CHUNK_SHIFT)
        fix = jnp.where(j > i, (2.0 * slope) * (i - j).astype(F32), 0.0)
        diag_ref[...] = jnp.where(allowed, fix, NEG)

    def softmax_stage(s_ref, smax_ref, shift_const, mp, cols):
        scores = s_ref[mp, :, cols]
        m_old = m_ref[mp, :, cols]
        m_new = jnp.maximum(m_old, smax_ref[mp, :, cols] + shift_const)
        probs = jnp.exp2(scores - (m_new - shift_const)).astype(BF16)
        m_ref[mp, :, cols] = m_new
        acc_ref[mp, :, cols] = jnp.exp2(m_old - m_new) * acc_ref[mp, :, cols] + _dot(vt_ref[...], probs)

    def finalize():
        lam = _lambda(lam_ref, lam_init)
        a0 = acc_ref[0]
        a1 = acc_ref[1]
        o_t = (a0[0:HEAD_DIM] / a0[HEAD_DIM:HEAD_DIM + 1]
               - lam * (a1[0:HEAD_DIM] / a1[HEAD_DIM:HEAD_DIM + 1]))
        o_t = o_t * (lax.rsqrt(jnp.mean(o_t * o_t, axis=0, keepdims=True) + EPS) * (1.0 - lam_init))
        o_ref[...] = (o_t.T * sg_ref[...]).astype(o_ref.dtype)

    units = [(mp, slice(c0, c0 + col_block)) for mp in range(2) for c0 in range(0, tq, col_block)]
    scored = jnp.minimum(step, n_pairs - 1)
    key_offset_s = kj_ref[scored] * tk - qi_ref[scored] * tq
    consumed = jnp.maximum(step - 1, 0)
    kj_c = kj_ref[consumed]
    key_offset_c = kj_c * tk - qi_ref[consumed] * tq

    @pl.when(step == 0)
    def _():
        build_diagonal_table()
        for mp, cols in units:
            score_stage(s_even_ref, smax_even_ref, mp, cols)

    @pl.when((step > 0) & (kj_c == 0))
    def _():
        m_ref[...] = jnp.full_like(m_ref, NEG)
        acc_ref[...] = jnp.zeros_like(acc_ref)

    tile_shift = slope * key_offset_c.astype(F32)
    even, odd = (s_even_ref, smax_even_ref), (s_odd_ref, smax_odd_ref)
    for parity, (written, read) in enumerate([(even, odd), (odd, even)]):
        @pl.when((step > 0) & (step % 2 == parity))
        def _():
            for mp, cols in units:
                score_stage(*written, mp, cols)
                softmax_stage(*read, tile_shift, mp, cols)

    for parity, written in enumerate([even, odd]):
        for key_offset in range(0, tq, tk):
            @pl.when((key_offset_s == key_offset) & (step % 2 == parity))
            def _():
                mask_diagonal(*written, key_offset)

    @pl.when((step > 0) & (key_offset_c == tq - tk))
    def _():
        finalize()


def alibi_slopes():
    return 2.0 ** (-8.0 * np.arange(1, HEADS + 1, dtype=np.float32) / HEADS)


def alibi_key_columns(t):
    term = (np.float32(LOG2E) * alibi_slopes())[:, None] * np.arange(t, dtype=np.float32)[None, :]
    cols = np.zeros((HEADS, 2, t, HEAD_DIM), np.float32)
    for c in range(KEY_COLS):
        piece = term.astype(BF16).astype(np.float32)
        term = term - piece
        for mp in range(2):
            a0 = MAP_DIM if mp == 0 else 0
            cols[:, mp, :, a0 + c] = piece
    return jnp.asarray(cols.astype(BF16))


def diff_attention_prompt(q, k, v_t, slopes, key_cols, lam_p, sub_gain, *, tq, tk, lam_init):
    t_all = q.shape[0]
    assert tq % tk == 0 and tk % CHUNK == 0 and t_all % tq == 0
    per_q = tq // tk
    qi = np.concatenate([np.full((i + 1) * per_q, i, np.int32) for i in range(t_all // tq)])
    kj = np.concatenate([np.arange((i + 1) * per_q, dtype=np.int32) for i in range(t_all // tq)])
    n_pairs = len(qi)

    def scored(p):
        return jnp.minimum(p, n_pairs - 1)

    def consumed(p):
        return jnp.maximum(p - 1, 0)

    grid_spec = pltpu.PrefetchScalarGridSpec(
        num_scalar_prefetch=2,
        grid=(HEADS, n_pairs + 1),
        in_specs=[
            pl.BlockSpec((1, 1, LANES), lambda h, p, qi_r, kj_r: (h, 0, 0)),
            pl.BlockSpec((4, MAP_DIM), lambda h, p, qi_r, kj_r: (0, 0)),
            pl.BlockSpec((1, HEAD_DIM), lambda h, p, qi_r, kj_r: (0, 0)),
            pl.BlockSpec((1, 2, tk, HEAD_DIM), lambda h, p, qi_r, kj_r: (h, 0, 0, 0)),
            pl.BlockSpec((tq, HEAD_DIM), lambda h, p, qi_r, kj_r: (qi_r[scored(p)], h)),
            pl.BlockSpec((tk, HEAD_DIM), lambda h, p, qi_r, kj_r: (kj_r[scored(p)], h)),
            pl.BlockSpec((HEAD_ROWS_T, tk), lambda h, p, qi_r, kj_r: (h, kj_r[consumed(p)])),
        ],
        out_specs=pl.BlockSpec((tq, HEAD_DIM), lambda h, p, qi_r, kj_r: (qi_r[consumed(p)], h)),
        scratch_shapes=[
            pltpu.VMEM((2, tk, tq), F32), pltpu.VMEM((2, tk, tq), F32),
            pltpu.VMEM((2, 1, tq), F32), pltpu.VMEM((2, 1, tq), F32),
            pltpu.VMEM((tk, tk), F32),
            pltpu.VMEM((2, 1, tq), F32), pltpu.VMEM((2, HEAD_ROWS_T, tq), F32),
        ],
    )
    return pl.pallas_call(
        functools.partial(_attn_prompt_kernel, tq=tq, tk=tk, col_block=tq, n_pairs=n_pairs, lam_init=lam_init),
        grid_spec=grid_spec,
        out_shape=jax.ShapeDtypeStruct((t_all, D_MODEL), BF16),
        compiler_params=_params("arbitrary", "arbitrary"),
        name="diff_attention_prompt",
    )(jnp.asarray(qi), jnp.asarray(kj), slopes, lam_p, sub_gain.reshape(1, HEAD_DIM), key_cols, q, k, v_t)


def _attn_decode_kernel(slope_ref, lam_ref, sg_ref, q_ref, k_ref, v_ref, kn_ref, vn_ref, o_ref, m_ref, l_ref, acc_ref,
                        *, tk, pos0, lam_init):
    j = pl.program_id(1)
    nk = pl.num_programs(1)
    n = q_ref.shape[1]

    @pl.when(j == 0)
    def _():
        m_ref[...] = jnp.full_like(m_ref, NEG)
        l_ref[...] = jnp.zeros_like(l_ref)
        acc_ref[...] = jnp.zeros_like(acc_ref)

    row = lax.broadcasted_iota(jnp.int32, (2 * n, 1), 0)
    q_pos = pos0 + jnp.where(row < n, row, row - n)
    lane = lax.broadcasted_iota(jnp.int32, (1, HEAD_DIM), 1)
    map_of_row_has_lane = (row < n) == (lane < MAP_DIM)

    def update(keys, values, k0, n_keys):
        heads = range(HEADS)
        k_pos = k0 + lax.broadcasted_iota(jnp.int32, (1, n_keys), 1)
        allowed = (k_pos >> CHUNK_SHIFT) <= (q_pos >> CHUNK_SHIFT)
        dist = jnp.abs(q_pos - k_pos).astype(F32)
        scores = []
        for h in heads:
            slope = slope_ref[h, :, 0:1] * LOG2E
            q = q_ref[0, :, h * HEAD_DIM:(h + 1) * HEAD_DIM]
            q2 = jnp.concatenate([q, q], axis=0)
            q_stack = jnp.where(map_of_row_has_lane, q2, jnp.zeros_like(q2))
            scores.append(jnp.where(allowed, _dot_nt(q_stack, keys(h).astype(BF16)) - slope * dist, NEG))
        probs, alphas = [], []
        for h in heads:
            m_old = m_ref[h]
            m_new = jnp.maximum(m_old, jnp.max(scores[h], axis=-1, keepdims=True))
            alphas.append(jnp.exp2(m_old - m_new))
            p = jnp.exp2(scores[h] - m_new)
            l_ref[h] = alphas[h] * l_ref[h] + jnp.sum(p, axis=-1, keepdims=True)
            probs.append(p.astype(BF16))
            m_ref[h] = m_new
        pv = [_dot(probs[h], values(h).astype(BF16)) for h in heads]
        for h in heads:
            acc_ref[h] = alphas[h] * acc_ref[h] + pv[h]

    def head_rows(h):
        return pl.ds(h, tk, stride=HEADS)

    update(lambda h: k_ref[0, head_rows(h), :], lambda h: v_ref[0, head_rows(h), :], j * tk, tk)

    @pl.when(j == nk - 1)
    def _():
        update(lambda h: kn_ref[0, :, h * HEAD_DIM:(h + 1) * HEAD_DIM],
               lambda h: vn_ref[0, :, h * HEAD_DIM:(h + 1) * HEAD_DIM], pos0, n)
        lam = _lambda(lam_ref, lam_init)
        for h in range(HEADS):
            o = acc_ref[h] / l_ref[h]
            o = o[0:n] - lam * o[n:2 * n]
            o_ref[0, :, h * HEAD_DIM:(h + 1) * HEAD_DIM] = (_rms(o, sg_ref[...]) * (1.0 - lam_init)).astype(o_ref.dtype)


def diff_attention_decode(q, k, v, k_new, v_new, slopes, lam_p, sub_gain, *, tk, lam_init):
    b, n, _ = q.shape
    t_k = k.shape[1]
    assert t_k % tk == 0 and t_k % CHUNK == 0
    k = k.reshape(b, t_k * HEADS, HEAD_DIM)
    v = v.reshape(b, t_k * HEADS, HEAD_DIM)
    row_block = pl.BlockSpec((1, n, D_MODEL), lambda bi, j: (bi, 0, 0))
    kv_spec = pl.BlockSpec((1, tk * HEADS, HEAD_DIM), lambda bi, j: (bi, j, 0))
    return pl.pallas_call(
        functools.partial(_attn_decode_kernel, tk=tk, pos0=t_k, lam_init=lam_init),
        grid=(b, t_k // tk),
        in_specs=[
            pl.BlockSpec((HEADS, 1, LANES), lambda bi, j: (0, 0, 0)),
            pl.BlockSpec((4, MAP_DIM), lambda bi, j: (0, 0)),
            pl.BlockSpec((1, HEAD_DIM), lambda bi, j: (0, 0)),
            row_block, kv_spec, kv_spec, row_block, row_block,
        ],
        out_specs=row_block,
        out_shape=jax.ShapeDtypeStruct((b, n, D_MODEL), BF16),
        scratch_shapes=[
            pltpu.VMEM((HEADS, 2 * n, 1), F32),
            pltpu.VMEM((HEADS, 2 * n, 1), F32),
            pltpu.VMEM((HEADS, 2 * n, HEAD_DIM), F32),
        ],
        compiler_params=_params("arbitrary", "arbitrary"),
        name="diff_attention_decode",
    )(slopes, lam_p, sub_gain.reshape(1, HEAD_DIM), q, k, v, k_new, v_new)


def _trunk(x, conv_state, ssm_state, past_k, past_v, wts, *, tm, gdn_chunk, gdn_group, attn_tile, attn_q_tile=None):
    b, t, _ = x.shape
    m = b * t
    xf = x.reshape(m, D_MODEL)
    new_conv, new_ssm = [], []
    k_new = v_new = k_bf = v_t = None
    for l in range(DEPTH):
        gains = wts["norm_gains"][l]
        if l < N_A:
            proj = norm_matmul(xf, gains[0], wts["a_w_in"][l], tm=tm, tn=A_IN_PAD // 3).reshape(b, t, A_IN_PAD)
            w, uu, qg, kd, qkd, egl, conv_l = gdn_prep(proj, conv_state[l], wts["a_conv_w"][l], wts["gate_par"][l],
                                                       C=gdn_chunk, G=gdn_group)
            og, s_l = gdn_scan(w, uu, qg, kd, qkd, egl, proj, ssm_state[l], wts["a_o_gain"][l],
                               C=gdn_chunk, G=gdn_group)
            new_conv.append(conv_l)
            new_ssm.append(s_l)
            xf = matmul_norm_res(og.reshape(m, D_MODEL), wts["a_w_out"][l], gains[1], xf, tm=tm)
        else:
            j = l - N_A
            prompt = past_k is None
            if j == 0:
                k_new, k_bf = norm_matmul(xf, wts["kv_gain"], wts["w_k"], tm=tm, tn=D_MODEL, out_dtypes=(F32, BF16))
                v_new = norm_matmul(xf, wts["kv_gain"], wts["w_v"], tm=tm, tn=D_MODEL, head_transposed_copy=prompt)
                if prompt:
                    v_new, v_t = v_new
            q = norm_matmul(xf, gains[0], wts["b_w_q"][j], tm=tm, tn=D_MODEL, out_dtypes=(BF16,), out_scale=Q_SCALE)
            lam_init = 0.8 - 0.6 * math.exp(-0.3 * l)
            tail = (wts["b_lam"][j], wts["b_sub_gain"][j])
            if prompt:
                assert b == 1
                o = diff_attention_prompt(q, k_bf, v_t, wts["slopes"], alibi_key_columns(attn_tile),
                                          *tail, tq=attn_q_tile, tk=attn_tile, lam_init=lam_init)
            else:
                o = diff_attention_decode(q.reshape(b, t, D_MODEL), past_k, past_v, k_new.reshape(b, t, D_MODEL),
                                          v_new.reshape(b, t, D_MODEL), wts["slopes"], *tail, tk=attn_tile,
                                          lam_init=lam_init)
            xf = matmul_norm_res(o.reshape(m, D_MODEL), wts["b_w_out"][j], gains[1], xf, tm=tm)
        xf = mlp_block(xf, gains[2], wts["mlp_w1"][l], wts["mlp_w2"][l], gains[3], tm=tm, tf=1024)
    kv_shape = (b, t, HEADS, HEAD_DIM)
    return (xf.reshape(b, t, D_MODEL), jnp.stack(new_conv), jnp.stack(new_ssm),
            k_new.reshape(kv_shape), v_new.reshape(kv_shape))


def kernel(x_prompt, x_sample, state_conv, state_ssm, cache_k, cache_v, norm_gains, a_w_in, a_conv_w, a_log,
           a_dt_bias, a_o_gain, a_w_out, kv_gain, w_kv, b_w_q, b_lam, b_sub_gain, b_w_out, mlp_w1, mlp_w2):
    a_in = a_w_in.shape[-1]
    gate_par = jnp.zeros((N_A, 2, LANES), F32)
    gate_par = gate_par.at[:, 0, HEADS:2 * HEADS].set(a_log.astype(F32))
    gate_par = gate_par.at[:, 1, HEADS:2 * HEADS].set(a_dt_bias.astype(F32))
    slopes = jnp.asarray(alibi_slopes())
    wts = dict(
        norm_gains=norm_gains.astype(F32),
        a_w_in=jnp.pad(a_w_in, ((0, 0), (0, 0), (0, A_IN_PAD - a_in))).astype(BF16),
        a_conv_w=a_conv_w.astype(F32),
        gate_par=gate_par,
        a_o_gain=a_o_gain.astype(F32),
        a_w_out=a_w_out.astype(BF16),
        kv_gain=kv_gain.astype(F32),
        w_k=w_kv[:, :D_MODEL].astype(BF16),
        w_v=w_kv[:, D_MODEL:].astype(BF16),
        b_w_q=b_w_q.astype(BF16),
        b_lam=b_lam.astype(F32),
        b_sub_gain=b_sub_gain.astype(F32),
        b_w_out=b_w_out.astype(BF16),
        mlp_w1=mlp_w1.astype(BF16),
        mlp_w2=mlp_w2.astype(BF16),
        slopes=jnp.broadcast_to(slopes[:, None, None], (HEADS, 1, LANES)),
    )
    bp, tp, _ = x_prompt.shape
    conv0 = jnp.zeros((N_A, bp, CONV_W - 1, A_QKV), F32)
    ssm0 = jnp.zeros((N_A, bp, HEADS, HEAD_DIM, HEAD_DIM), F32)
    y_p, p_conv, p_ssm, p_k, p_v = _trunk(x_prompt, conv0, ssm0, None, None, wts,
                                          tm=512, gdn_chunk=CHUNK, gdn_group=4, attn_tile=512, attn_q_tile=1024)
    bs, ts, _ = x_sample.shape
    y_s, s_conv, s_ssm, s_k, s_v = _trunk(x_sample, state_conv, state_ssm, cache_k, cache_v,
                                          wts, tm=bs * ts, gdn_chunk=ts, gdn_group=1, attn_tile=512)
    return (y_p, y_s, p_conv, p_ssm, p_k, p_v, s_conv, s_ssm, s_k, s_v)
```

```python
import functools
import math

import jax
import jax.numpy as jnp
import numpy as np
from jax import lax
from jax.experimental import pallas as pl
from jax.experimental.pallas import tpu as pltpu

F32 = jnp.float32
BF16 = jnp.bfloat16

D_MODEL = 1024
DEPTH = 4
N_A = DEPTH // 2
CHUNK = 64
CHUNK_SHIFT = CHUNK.bit_length() - 1
assert 1 << CHUNK_SHIFT == CHUNK
HEADS = 8
HEAD_DIM = D_MODEL // HEADS
CONV_W = 4
A_QKV = 3 * D_MODEL
A_GATE_COL = A_QKV + D_MODEL
LANES = 128
A_IN_PAD = A_GATE_COL + LANES
MAP_DIM = HEAD_DIM // 2
D_FF = 4 * D_MODEL
EPS = 1e-6
NEG = -1e30
VMEM_LIMIT = 48 * 1024 * 1024
HIGHEST = lax.Precision.HIGHEST
ONES_ROWS = 16
HEAD_ROWS_T = HEAD_DIM + ONES_ROWS
LOG2E = math.log2(math.e)
Q_SCALE = MAP_DIM ** -0.5 * LOG2E


def _dot(a, b, precision=None):
    return jnp.dot(a, b, preferred_element_type=F32, precision=precision)


def _dot_nt(a, b):
    return lax.dot_general(a, b, (((1,), (1,)), ((), ())), preferred_element_type=F32)


def _dot_tn(a, b):
    return lax.dot_general(a, b, (((0,), (0,)), ((), ())), preferred_element_type=F32)


def _split(a):
    hi = a.astype(BF16)
    return hi, (a - hi.astype(F32)).astype(BF16)


def _dot_split(a, b):
    return _dot(a[0], b[0]) + (_dot(a[0], b[1]) + _dot(a[1], b[0]))


def _rms(x, gain):
    return x * lax.rsqrt(jnp.mean(x * x, axis=-1, keepdims=True) + EPS) * gain


def _sigmoid(x):
    return 1.0 / (1.0 + jnp.exp(-x))


def _params(*semantics):
    return pltpu.CompilerParams(dimension_semantics=semantics, vmem_limit_bytes=VMEM_LIMIT)


def _norm_matmul_kernel(x_ref, g_ref, w_ref, *out_and_scratch, out_scale, head_transposed_copy):
    *o_refs, hn_ref = out_and_scratch

    @pl.when(pl.program_id(1) == 0)
    def _():
        hn_ref[...] = _rms(x_ref[...], g_ref[...]).astype(BF16)

    y = _dot(hn_ref[...], w_ref[...])
    if out_scale != 1.0:
        y = y * out_scale
    if head_transposed_copy:
        *o_refs, ot_ref = o_refs
        y_t = y.T.astype(ot_ref.dtype)
        for h in range(y_t.shape[0] // HEAD_DIM):
            r0 = h * HEAD_ROWS_T
            ot_ref[r0:r0 + HEAD_DIM, :] = y_t[h * HEAD_DIM:(h + 1) * HEAD_DIM]
            ot_ref[r0 + HEAD_DIM:r0 + HEAD_ROWS_T, :] = jnp.ones((ONES_ROWS, y_t.shape[1]), ot_ref.dtype)
    for o_ref in o_refs:
        o_ref[...] = y.astype(o_ref.dtype)


def norm_matmul(x, gain, w, *, tm, tn, out_dtypes=(F32,), out_scale=1.0, head_transposed_copy=False):
    m, k = x.shape
    n = w.shape[1]
    out_specs = [pl.BlockSpec((tm, tn), lambda i, j: (i, j)) for _ in out_dtypes]
    out_shape = [jax.ShapeDtypeStruct((m, n), dt) for dt in out_dtypes]
    if head_transposed_copy:
        out_specs.append(pl.BlockSpec((tn // HEAD_DIM * HEAD_ROWS_T, tm), lambda i, j: (j, i)))
        out_shape.append(jax.ShapeDtypeStruct((n // HEAD_DIM * HEAD_ROWS_T, m), BF16))
    outs = pl.pallas_call(
        functools.partial(_norm_matmul_kernel, out_scale=out_scale, head_transposed_copy=head_transposed_copy),
        grid=(m // tm, n // tn),
        in_specs=[
            pl.BlockSpec((tm, k), lambda i, j: (i, 0)),
            pl.BlockSpec((1, k), lambda i, j: (0, 0)),
            pl.BlockSpec((k, tn), lambda i, j: (0, j)),
        ],
        out_specs=out_specs,
        out_shape=out_shape,
        scratch_shapes=[pltpu.VMEM((tm, k), BF16)],
        compiler_params=_params("parallel", "arbitrary"),
        name="norm_matmul",
    )(x, gain.reshape(1, k), w)
    return outs[0] if len(outs) == 1 else outs


def _matmul_norm_res_kernel(a_ref, w_ref, g_ref, x_ref, o_ref):
    y = _dot(a_ref[...], w_ref[...])
    o_ref[...] = x_ref[...] + _rms(y, g_ref[...])


def matmul_norm_res(a, w, gain, x, *, tm):
    m, k = a.shape
    d = w.shape[1]
    return pl.pallas_call(
        _matmul_norm_res_kernel,
        grid=(m // tm,),
        in_specs=[
            pl.BlockSpec((tm, k), lambda i: (i, 0)),
            pl.BlockSpec((k, d), lambda i: (0, 0)),
            pl.BlockSpec((1, d), lambda i: (0, 0)),
            pl.BlockSpec((tm, d), lambda i: (i, 0)),
        ],
        out_specs=pl.BlockSpec((tm, d), lambda i: (i, 0)),
        out_shape=jax.ShapeDtypeStruct((m, d), F32),
        compiler_params=_params("parallel"),
        name="matmul_norm_res",
    )(a, w, gain.reshape(1, d), x)


def _mlp_kernel(x_ref, g_in_ref, w1_ref, w2_ref, g_out_ref, o_ref, hn_ref, acc_ref):
    f = pl.program_id(1)

    @pl.when(f == 0)
    def _():
        hn_ref[...] = _rms(x_ref[...], g_in_ref[...]).astype(BF16)
        acc_ref[...] = jnp.zeros_like(acc_ref)

    h = _dot(hn_ref[...], w1_ref[...])
    h = jnp.square(jnp.maximum(h, 0.0)).astype(BF16)
    acc_ref[...] += _dot(h, w2_ref[...])

    @pl.when(f == pl.num_programs(1) - 1)
    def _():
        o_ref[...] = x_ref[...] + _rms(acc_ref[...], g_out_ref[...])


def mlp_block(x, g_in, w1, w2, g_out, *, tm, tf):
    m, d = x.shape
    ff = w1.shape[1]
    return pl.pallas_call(
        _mlp_kernel,
        grid=(m // tm, ff // tf),
        in_specs=[
            pl.BlockSpec((tm, d), lambda i, f: (i, 0)),
            pl.BlockSpec((1, d), lambda i, f: (0, 0)),
            pl.BlockSpec((d, tf), lambda i, f: (0, f)),
            pl.BlockSpec((tf, d), lambda i, f: (f, 0)),
            pl.BlockSpec((1, d), lambda i, f: (0, 0)),
        ],
        out_specs=pl.BlockSpec((tm, d), lambda i, f: (i, 0)),
        out_shape=jax.ShapeDtypeStruct((m, d), F32),
        scratch_shapes=[pltpu.VMEM((tm, d), BF16), pltpu.VMEM((tm, d), F32)],
        compiler_params=_params("parallel", "arbitrary"),
        name="mlp_block",
    )(x, g_in.reshape(1, d), w1, w2, g_out.reshape(1, d))


CONV_ROW0 = 8
UNIT_GROUP = 16


def _gdn_prep_kernel(proj_ref, conv0_ref, cw_ref, gp_ref,
                     w_ref, uu_ref, qg_ref, kd_ref, qkd_ref, egl_ref, conv_ref, xbuf_ref, *, C, G):
    R = C * G
    prev0 = CONV_ROW0 - (CONV_W - 1)

    @pl.when(pl.program_id(1) == 0)
    def _():
        xbuf_ref[prev0:CONV_ROW0, :] = conv0_ref[0]

    xbuf_ref[CONV_ROW0:CONV_ROW0 + R, :] = proj_ref[0, :, 0:A_QKV]
    cw = cw_ref[...]
    y = xbuf_ref[prev0:prev0 + R, :] * cw[0:1, :]
    for j in range(1, CONV_W):
        y = y + xbuf_ref[prev0 + j:prev0 + j + R, :] * cw[j:j + 1, :]
    y = y * _sigmoid(y)
    last = xbuf_ref[prev0 + R:CONV_ROW0 + R, :]
    xbuf_ref[prev0:CONV_ROW0, :] = last
    conv_ref[0] = last

    tail = proj_ref[0, :, A_GATE_COL:A_IN_PAD]
    gp = gp_ref[...]
    beta = _sigmoid(tail)
    t = tail + gp[1:2, :]
    softplus = jnp.maximum(t, 0.0) + jnp.log(1.0 + jnp.exp(-jnp.abs(t)))
    g = -jnp.exp(gp[0:1, :]) * softplus

    rr = lax.broadcasted_iota(jnp.int32, (R, R), 0)
    cc = lax.broadcasted_iota(jnp.int32, (R, R), 1)
    shift = C.bit_length() - 1
    same = (rr >> shift) == (cc >> shift)
    gc = _dot(jnp.where(same & (rr >= cc), 1.0, 0.0), g, HIGHEST)
    gl = _dot(jnp.where(same, 1.0, 0.0), g, HIGHEST)
    pad_rows = -R % LANES
    gc_t = (jnp.concatenate([gc, jnp.zeros((pad_rows, LANES), F32)], axis=0) if pad_rows else gc).T
    eg = jnp.exp(gc)
    ekd = jnp.exp(gl - gc)
    egl = jnp.exp(gl)
    for gi in range(G):
        egl_ref[0, gi] = egl[gi * C:gi * C + 1, :]

    row = lax.broadcasted_iota(jnp.int32, (C, C), 0)
    col = lax.broadcasted_iota(jnp.int32, (C, C), 1)
    incl = row >= col
    strict = row > col
    eye = (row == col).astype(F32)

    units = [(gi, h) for gi in range(G) for h in range(HEADS)]
    qn, kn = [], []
    for h in range(HEADS):
        q = y[:, h * HEAD_DIM:(h + 1) * HEAD_DIM]
        k = y[:, D_MODEL + h * HEAD_DIM:D_MODEL + (h + 1) * HEAD_DIM]
        qn.append(q * lax.rsqrt(jnp.sum(q * q, axis=-1, keepdims=True) + EPS) * (HEAD_DIM ** -0.5))
        kn.append(k * lax.rsqrt(jnp.sum(k * k, axis=-1, keepdims=True) + EPS))

    def rows(gi):
        return slice(gi * C, (gi + 1) * C)

    def process(units):
        decay, qk, a_mat = {}, {}, {}
        for gi, h in units:
            rs, gl_h = rows(gi), HEADS + h
            g_col = gc[rs, gl_h:gl_h + 1]
            g_row = gc_t[gl_h:gl_h + 1, gi * C:(gi + 1) * C]
            decay[gi, h] = jnp.where(incl, jnp.exp(jnp.minimum(g_col - g_row, 0.0)), 0.0)
            k_bf = kn[h][rs].astype(BF16)
            qk_kk = _dot_nt(jnp.concatenate([qn[h][rs].astype(BF16), k_bf], axis=0), k_bf)
            qk[gi, h] = qk_kk[0:C]
            a_mat[gi, h] = jnp.where(strict, beta[rs, h:h + 1] * qk_kk[C:2 * C] * decay[gi, h], 0.0)
        t_inv = {u: eye - a_mat[u] for u in units}
        pw = {}
        for u in units:
            a_s = _split(a_mat[u])
            pw[u] = _dot_split(a_s, a_s)
        n_double = C.bit_length() - 2
        for step in range(n_double):
            for u in units:
                p_s = _split(pw[u])
                t_s = _split(t_inv[u])
                if step + 1 < n_double:
                    tp = _dot_split(tuple(jnp.concatenate([t_part, p_part], axis=0)
                                          for t_part, p_part in zip(t_s, p_s)), p_s)
                    t_inv[u] = t_inv[u] + tp[0:C]
                    pw[u] = tp[C:2 * C]
                else:
                    t_inv[u] = t_inv[u] + _dot_split(t_s, p_s)
        for gi, h in units:
            rs, gl_h = rows(gi), HEADS + h
            hs = slice(h * HEAD_DIM, (h + 1) * HEAD_DIM)
            b_col = beta[rs, h:h + 1]
            eg_col = eg[rs, gl_h:gl_h + 1]
            k = kn[h][rs]
            v = y[rs, 2 * D_MODEL + h * HEAD_DIM:2 * D_MODEL + (h + 1) * HEAD_DIM]
            rhs = jnp.concatenate([b_col * v, (b_col * eg_col) * k], axis=-1).astype(BF16)
            sol = _dot(t_inv[gi, h].astype(BF16), rhs)
            uu_ref[0, rs, hs] = sol[:, 0:HEAD_DIM].astype(uu_ref.dtype)
            w_ref[0, rs, hs] = sol[:, HEAD_DIM:].astype(w_ref.dtype)
            qg_ref[0, rs, hs] = (eg_col * qn[h][rs]).astype(qg_ref.dtype)
            kd_ref[0, rs, hs] = (ekd[rs, gl_h:gl_h + 1] * k).astype(kd_ref.dtype)
            qkd_ref[0, rs, h * C:(h + 1) * C] = (qk[gi, h] * decay[gi, h]).astype(qkd_ref.dtype)

    for g0 in range(0, len(units), UNIT_GROUP):
        process(units[g0:g0 + UNIT_GROUP])


def gdn_prep(proj, conv0, conv_w, gate_par, *, C, G):
    b, t, _ = proj.shape
    r = C * G
    nc = t // C
    row_spec = pl.BlockSpec((1, r, D_MODEL), lambda i, c: (i, c, 0))
    conv_spec = pl.BlockSpec((1, CONV_W - 1, A_QKV), lambda i, c: (i, 0, 0))
    return pl.pallas_call(
        functools.partial(_gdn_prep_kernel, C=C, G=G),
        grid=(b, t // r),
        in_specs=[
            pl.BlockSpec((1, r, A_IN_PAD), lambda i, c: (i, c, 0)),
            conv_spec,
            pl.BlockSpec((CONV_W, A_QKV), lambda i, c: (0, 0)),
            pl.BlockSpec((2, LANES), lambda i, c: (0, 0)),
        ],
        out_specs=[
            row_spec, row_spec, row_spec, row_spec,
            pl.BlockSpec((1, r, HEADS * C), lambda i, c: (i, c, 0)),
            pl.BlockSpec((1, G, 1, LANES), lambda i, c: (i, c, 0, 0)),
            conv_spec,
        ],
        out_shape=[jax.ShapeDtypeStruct((b, t, D_MODEL), BF16)] * 4 + [
            jax.ShapeDtypeStruct((b, t, HEADS * C), BF16),
            jax.ShapeDtypeStruct((b, nc, 1, LANES), F32),
            jax.ShapeDtypeStruct((b, CONV_W - 1, A_QKV), F32),
        ],
        scratch_shapes=[pltpu.VMEM((CONV_ROW0 + r, A_QKV), F32)],
        compiler_params=_params("arbitrary", "arbitrary"),
        name="gdn_prep",
    )(proj, conv0, conv_w, gate_par)


def _gdn_scan_kernel(w_ref, uu_ref, qg_ref, kd_ref, qkd_ref, egl_ref, z_ref, s0_ref, og_ref, o_ref, s_ref, *, C, G):
    @pl.when(pl.program_id(1) == 0)
    def _():
        s_ref[0] = s0_ref[0]

    heads = range(HEADS)
    for gi in range(G):
        rs = slice(gi * C, (gi + 1) * C)
        egl = egl_ref[0, gi]

        def hs(h):
            return slice(h * HEAD_DIM, (h + 1) * HEAD_DIM)

        s_old = [s_ref[0, h] for h in heads]
        s_bf = [s.astype(BF16) for s in s_old]
        wq = [_dot(jnp.concatenate([w_ref[0, rs, hs(h)], qg_ref[0, rs, hs(h)]], axis=0), s_bf[h]) for h in heads]
        u_bf = [(uu_ref[0, rs, hs(h)].astype(F32) - wq[h][0:C]).astype(BF16) for h in heads]
        o = [wq[h][C:2 * C] + _dot(qkd_ref[0, rs, h * C:(h + 1) * C], u_bf[h]) for h in heads]
        for h in heads:
            s_ref[0, h] = egl[:, HEADS + h:HEADS + h + 1] * s_old[h] + _dot_tn(kd_ref[0, rs, hs(h)], u_bf[h])
        for h in heads:
            z = z_ref[0, rs, hs(h)]
            o_ref[0, rs, hs(h)] = (_rms(o[h], og_ref[...]) * (z * _sigmoid(z))).astype(o_ref.dtype)


def gdn_scan(w, uu, qg, kd, qkd, egl, proj, s0, o_gain, *, C, G):
    b, t, _ = w.shape
    r = C * G
    row_spec = pl.BlockSpec((1, r, D_MODEL), lambda i, c: (i, c, 0))
    state_spec = pl.BlockSpec((1, HEADS, HEAD_DIM, HEAD_DIM), lambda i, c: (i, 0, 0, 0))
    return pl.pallas_call(
        functools.partial(_gdn_scan_kernel, C=C, G=G),
        grid=(b, t // r),
        in_specs=[
            row_spec, row_spec, row_spec, row_spec,
            pl.BlockSpec((1, r, HEADS * C), lambda i, c: (i, c, 0)),
            pl.BlockSpec((1, G, 1, LANES), lambda i, c: (i, c, 0, 0)),
            pl.BlockSpec((1, r, D_MODEL), lambda i, c: (i, c, A_QKV // D_MODEL)),
            state_spec,
            pl.BlockSpec((1, HEAD_DIM), lambda i, c: (0, 0)),
        ],
        out_specs=[row_spec, state_spec],
        out_shape=[
            jax.ShapeDtypeStruct((b, t, D_MODEL), BF16),
            jax.ShapeDtypeStruct((b, HEADS, HEAD_DIM, HEAD_DIM), F32),
        ],
        compiler_params=_params("arbitrary", "arbitrary"),
        name="gdn_scan",
    )(w, uu, qg, kd, qkd, egl, proj, s0, o_gain.reshape(1, HEAD_DIM))


def _lambda(lam_ref, lam_init):
    lp = lam_ref[...]
    return (jnp.exp(jnp.sum(lp[0:1] * lp[1:2], axis=-1, keepdims=True))
            - jnp.exp(jnp.sum(lp[2:3] * lp[3:4], axis=-1, keepdims=True)) + lam_init)


DECODE_HEAD_GROUP = HEADS
KEY_COLS = 3


def _attn_prompt_kernel(qi_ref, kj_ref, slope_ref, lam_ref, sg_ref, kx_ref, q_ref, k_ref, vt_ref, o_ref,
                        s_even_ref, s_odd_ref, smax_even_ref, smax_odd_ref, diag_ref, m_ref, acc_ref,
                        *, tq, tk, col_block, n_pairs, lam_init):
    step = pl.program_id(1)
    slope = slope_ref[0, :, 0:1] * LOG2E
    lane = lax.broadcasted_iota(jnp.int32, (1, HEAD_DIM), 1)
    map_lanes = [lane < MAP_DIM, lane >= MAP_DIM]

    def score_stage(s_ref, smax_ref, mp, cols):
        q = q_ref[cols, :]
        k = k_ref[...]
        a0 = MAP_DIM if mp == 0 else 0
        extra = (lane >= a0) & (lane < a0 + KEY_COLS)
        qa = jnp.where(map_lanes[mp], q, jnp.where(extra, 1.0, 0.0).astype(BF16))
        ka = jnp.where(map_lanes[mp], k, kx_ref[0, mp])
        scores = _dot_nt(ka, qa)
        s_ref[mp, :, cols] = scores
        smax_ref[mp, :, cols] = jnp.max(scores, axis=0, keepdims=True)

    def mask_diagonal(s_ref, smax_ref, key_offset):
        facing = slice(key_offset, key_offset + tk)
        for mp in range(2):
            if key_offset:
                s_ref[mp, :, 0:key_offset] = jnp.full((tk, key_offset), NEG, F32)
                smax_ref[mp, :, 0:key_offset] = jnp.full((1, key_offset), NEG, F32)
            fixed = s_ref[mp, :, facing] + diag_ref[...]
            s_ref[mp, :, facing] = fixed
            smax_ref[mp, :, facing] = jnp.max(fixed, axis=0, keepdims=True)

    def build_diagonal_table():
        j = lax.broadcasted_iota(jnp.int32, (tk, 1), 0)
        i = lax.broadcasted_iota(jnp.int32, (1, tk), 1)
        allowed = (j >> CHUNK_SHIFT) <= (i >> CHUNK_SHIFT)
        fix = jnp.where(j > i, (2.0 * slope) * (i - j).astype(F32), 0.0)
        diag_ref[...] = jnp.where(allowed, fix, NEG)

    def softmax_stage(s_ref, smax_ref, shift_const, mp, cols):
        scores = s_ref[mp, :, cols]
        m_old = m_ref[mp, :, cols]
        m_new = jnp.maximum(m_old, smax_ref[mp, :, cols] + shift_const)
        probs = jnp.exp2(scores - (m_new - shift_const)).astype(BF16)
        m_ref[mp, :, cols] = m_new
        acc_ref[mp, :, cols] = jnp.exp2(m_old - m_new) * acc_ref[mp, :, cols] + _dot(vt_ref[...], probs)

    def finalize():
        lam = _lambda(lam_ref, lam_init)
        a0 = acc_ref[0]
        a1 = acc_ref[1]
        o_t = (a0[0:HEAD_DIM] / a0[HEAD_DIM:HEAD_DIM + 1]
               - lam * (a1[0:HEAD_DIM] / a1[HEAD_DIM:HEAD_DIM + 1]))
        o_t = o_t * (lax.rsqrt(jnp.mean(o_t * o_t, axis=0, keepdims=True) + EPS) * (1.0 - lam_init))
        o_ref[...] = (o_t.T * sg_ref[...]).astype(o_ref.dtype)

    units = [(mp, slice(c0, c0 + col_block)) for mp in range(2) for c0 in range(0, tq, col_block)]
    scored = jnp.minimum(step, n_pairs - 1)
    key_offset_s = kj_ref[scored] * tk - qi_ref[scored] * tq
    consumed = jnp.maximum(step - 1, 0)
    kj_c = kj_ref[consumed]
    key_offset_c = kj_c * tk - qi_ref[consumed] * tq

    @pl.when(step == 0)
    def _():
        build_diagonal_table()
        for mp, cols in units:
            score_stage(s_even_ref, smax_even_ref, mp, cols)

    @pl.when((step > 0) & (kj_c == 0))
    def _():
        m_ref[...] = jnp.full_like(m_ref, NEG)
        acc_ref[...] = jnp.zeros_like(acc_ref)

    tile_shift = slope * key_offset_c.astype(F32)
    even, odd = (s_even_ref, smax_even_ref), (s_odd_ref, smax_odd_ref)
    for parity, (written, read) in enumerate([(even, odd), (odd, even)]):
        @pl.when((step > 0) & (step % 2 == parity))
        def _():
            for mp, cols in units:
                score_stage(*written, mp, cols)
                softmax_stage(*read, tile_shift, mp, cols)

    for parity, written in enumerate([even, odd]):
        for key_offset in range(0, tq, tk):
            @pl.when((key_offset_s == key_offset) & (step % 2 == parity))
            def _():
                mask_diagonal(*written, key_offset)

    @pl.when((step > 0) & (key_offset_c == tq - tk))
    def _():
        finalize()


def alibi_slopes():
    return 2.0 ** (-8.0 * np.arange(1, HEADS + 1, dtype=np.float32) / HEADS)


def alibi_key_columns(t):
    term = (np.float32(LOG2E) * alibi_slopes())[:, None] * np.arange(t, dtype=np.float32)[None, :]
    cols = np.zeros((HEADS, 2, t, HEAD_DIM), np.float32)
    for c in range(KEY_COLS):
        piece = term.astype(BF16).astype(np.float32)
        term = term - piece
        for mp in range(2):
            a0 = MAP_DIM if mp == 0 else 0
            cols[:, mp, :, a0 + c] = piece
    return jnp.asarray(cols.astype(BF16))


def diff_attention_prompt(q, k, v_t, slopes, key_cols, lam_p, sub_gain, *, tq, tk, lam_init):
    t_all = q.shape[0]
    assert tq % tk == 0 and tk % CHUNK == 0 and t_all % tq == 0
    per_q = tq // tk
    qi = np.concatenate([np.full((i + 1) * per_q, i, np.int32) for i in range(t_all // tq)])
    kj = np.concatenate([np.arange((i + 1) * per_q, dtype=np.int32) for i in range(t_all // tq)])
    n_pairs = len(qi)

    def scored(p):
        return jnp.minimum(p, n_pairs - 1)

    def consumed(p):
        return jnp.maximum(p - 1, 0)

    grid_spec = pltpu.PrefetchScalarGridSpec(
        num_scalar_prefetch=2,
        grid=(HEADS, n_pairs + 1),
        in_specs=[
            pl.BlockSpec((1, 1, LANES), lambda h, p, qi_r, kj_r: (h, 0, 0)),
            pl.BlockSpec((4, MAP_DIM), lambda h, p, qi_r, kj_r: (0, 0)),
            pl.BlockSpec((1, HEAD_DIM), lambda h, p, qi_r, kj_r: (0, 0)),
            pl.BlockSpec((1, 2, tk, HEAD_DIM), lambda h, p, qi_r, kj_r: (h, 0, 0, 0)),
            pl.BlockSpec((tq, HEAD_DIM), lambda h, p, qi_r, kj_r: (qi_r[scored(p)], h)),
            pl.BlockSpec((tk, HEAD_DIM), lambda h, p, qi_r, kj_r: (kj_r[scored(p)], h)),
            pl.BlockSpec((HEAD_ROWS_T, tk), lambda h, p, qi_r, kj_r: (h, kj_r[consumed(p)])),
        ],
        out_specs=pl.BlockSpec((tq, HEAD_DIM), lambda h, p, qi_r, kj_r: (qi_r[consumed(p)], h)),
        scratch_shapes=[
            pltpu.VMEM((2, tk, tq), F32), pltpu.VMEM((2, tk, tq), F32),
            pltpu.VMEM((2, 1, tq), F32), pltpu.VMEM((2, 1, tq), F32),
            pltpu.VMEM((tk, tk), F32),
            pltpu.VMEM((2, 1, tq), F32), pltpu.VMEM((2, HEAD_ROWS_T, tq), F32),
        ],
    )
    return pl.pallas_call(
        functools.partial(_attn_prompt_kernel, tq=tq, tk=tk, col_block=tq, n_pairs=n_pairs, lam_init=lam_init),
        grid_spec=grid_spec,
        out_shape=jax.ShapeDtypeStruct((t_all, D_MODEL), BF16),
        compiler_params=_params("arbitrary", "arbitrary"),
        name="diff_attention_prompt",
    )(jnp.asarray(qi), jnp.asarray(kj), slopes, lam_p, sub_gain.reshape(1, HEAD_DIM), key_cols, q, k, v_t)


def _attn_decode_kernel(slope_ref, lam_ref, sg_ref, q_ref, k_ref, v_ref, kn_ref, vn_ref, o_ref, m_ref, l_ref, acc_ref,
                        *, tk, pos0, lam_init):
    j = pl.program_id(1)
    nk = pl.num_programs(1)
    n = q_ref.shape[1]

    @pl.when(j == 0)
    def _():
        m_ref[...] = jnp.full_like(m_ref, NEG)
        l_ref[...] = jnp.zeros_like(l_ref)
        acc_ref[...] = jnp.zeros_like(acc_ref)

    row = lax.broadcasted_iota(jnp.int32, (2 * n, 1), 0)
    q_pos = pos0 + jnp.where(row < n, row, row - n)
    lane = lax.broadcasted_iota(jnp.int32, (1, HEAD_DIM), 1)
    map_of_row_has_lane = (row < n) == (lane < MAP_DIM)

    def update(keys, values, k0, n_keys):
        k_pos = k0 + lax.broadcasted_iota(jnp.int32, (1, n_keys), 1)
        allowed = (k_pos >> CHUNK_SHIFT) <= (q_pos >> CHUNK_SHIFT)
        dist = jnp.abs(q_pos - k_pos).astype(F32)
        for h0 in range(0, HEADS, DECODE_HEAD_GROUP):
            heads = range(h0, h0 + DECODE_HEAD_GROUP)
            scores, probs, alphas = {}, {}, {}
            for h in heads:
                slope = slope_ref[h, :, 0:1] * LOG2E
                q = q_ref[0, :, h * HEAD_DIM:(h + 1) * HEAD_DIM]
                q2 = jnp.concatenate([q, q], axis=0)
                q_stack = jnp.where(map_of_row_has_lane, q2, jnp.zeros_like(q2))
                scores[h] = jnp.where(allowed, _dot_nt(q_stack, keys(h).astype(BF16)) - slope * dist, NEG)
            for h in heads:
                m_old = m_ref[h]
                m_new = jnp.maximum(m_old, jnp.max(scores[h], axis=-1, keepdims=True))
                alphas[h] = jnp.exp2(m_old - m_new)
                p = jnp.exp2(scores[h] - m_new)
                l_ref[h] = alphas[h] * l_ref[h] + jnp.sum(p, axis=-1, keepdims=True)
                probs[h] = p.astype(BF16)
                m_ref[h] = m_new
            pv = {h: _dot(probs[h], values(h).astype(BF16)) for h in heads}
            for h in heads:
                acc_ref[h] = alphas[h] * acc_ref[h] + pv[h]

    def head_rows(h):
        return pl.ds(h, tk, stride=HEADS)

    update(lambda h: k_ref[0, head_rows(h), :], lambda h: v_ref[0, head_rows(h), :], j * tk, tk)

    @pl.when(j == nk - 1)
    def _():
        update(lambda h: kn_ref[0, :, h * HEAD_DIM:(h + 1) * HEAD_DIM],
               lambda h: vn_ref[0, :, h * HEAD_DIM:(h + 1) * HEAD_DIM], pos0, n)
        lam = _lambda(lam_ref, lam_init)
        for h in range(HEADS):
            o = acc_ref[h] / l_ref[h]
            o = o[0:n] - lam * o[n:2 * n]
            o_ref[0, :, h * HEAD_DIM:(h + 1) * HEAD_DIM] = (_rms(o, sg_ref[...]) * (1.0 - lam_init)).astype(o_ref.dtype)


def diff_attention_decode(q, k, v, k_new, v_new, slopes, lam_p, sub_gain, *, tk, lam_init):
    b, n, _ = q.shape
    t_k = k.shape[1]
    assert t_k % tk == 0 and t_k % CHUNK == 0
    k = k.reshape(b, t_k * HEADS, HEAD_DIM)
    v = v.reshape(b, t_k * HEADS, HEAD_DIM)
    row_block = pl.BlockSpec((1, n, D_MODEL), lambda bi, j: (bi, 0, 0))
    kv_spec = pl.BlockSpec((1, tk * HEADS, HEAD_DIM), lambda bi, j: (bi, j, 0))
    return pl.pallas_call(
        functools.partial(_attn_decode_kernel, tk=tk, pos0=t_k, lam_init=lam_init),
        grid=(b, t_k // tk),
        in_specs=[
            pl.BlockSpec((HEADS, 1, LANES), lambda bi, j: (0, 0, 0)),
            pl.BlockSpec((4, MAP_DIM), lambda bi, j: (0, 0)),
            pl.BlockSpec((1, HEAD_DIM), lambda bi, j: (0, 0)),
            row_block, kv_spec, kv_spec, row_block, row_block,
        ],
        out_specs=row_block,
        out_shape=jax.ShapeDtypeStruct((b, n, D_MODEL), BF16),
        scratch_shapes=[
            pltpu.VMEM((HEADS, 2 * n, 1), F32),
            pltpu.VMEM((HEADS, 2 * n, 1), F32),
            pltpu.VMEM((HEADS, 2 * n, HEAD_DIM), F32),
        ],
        compiler_params=_params("arbitrary", "arbitrary"),
        name="diff_attention_decode",
    )(slopes, lam_p, sub_gain.reshape(1, HEAD_DIM), q, k, v, k_new, v_new)


def _trunk(x, conv_state, ssm_state, past_k, past_v, wts, *, tm, gdn_chunk, gdn_group, attn_tile, attn_q_tile=None):
    b, t, _ = x.shape
    m = b * t
    xf = x.reshape(m, D_MODEL)
    new_conv, new_ssm = [], []
    k_new = v_new = k_bf = v_t = None
    for l in range(DEPTH):
        gains = wts["norm_gains"][l]
        if l < N_A:
            proj = norm_matmul(xf, gains[0], wts["a_w_in"][l], tm=tm, tn=A_IN_PAD).reshape(b, t, A_IN_PAD)
            w, uu, qg, kd, qkd, egl, conv_l = gdn_prep(proj, conv_state[l], wts["a_conv_w"][l], wts["gate_par"][l],
                                                       C=gdn_chunk, G=gdn_group)
            og, s_l = gdn_scan(w, uu, qg, kd, qkd, egl, proj, ssm_state[l], wts["a_o_gain"][l],
                               C=gdn_chunk, G=gdn_group)
            new_conv.append(conv_l)
            new_ssm.append(s_l)
            xf = matmul_norm_res(og.reshape(m, D_MODEL), wts["a_w_out"][l], gains[1], xf, tm=tm)
        else:
            j = l - N_A
            prompt = past_k is None
            if j == 0:
                k_new, k_bf = norm_matmul(xf, wts["kv_gain"], wts["w_k"], tm=tm, tn=D_MODEL, out_dtypes=(F32, BF16))
                v_new = norm_matmul(xf, wts["kv_gain"], wts["w_v"], tm=tm, tn=D_MODEL, head_transposed_copy=prompt)
                if prompt:
                    v_new, v_t = v_new
            q = norm_matmul(xf, gains[0], wts["b_w_q"][j], tm=tm, tn=D_MODEL, out_dtypes=(BF16,), out_scale=Q_SCALE)
            lam_init = 0.8 - 0.6 * math.exp(-0.3 * l)
            tail = (wts["b_lam"][j], wts["b_sub_gain"][j])
            if prompt:
                assert b == 1
                o = diff_attention_prompt(q, k_bf, v_t, wts["slopes"], alibi_key_columns(attn_tile),
                                          *tail, tq=attn_q_tile, tk=attn_tile, lam_init=lam_init)
            else:
                o = diff_attention_decode(q.reshape(b, t, D_MODEL), past_k, past_v, k_new.reshape(b, t, D_MODEL),
                                          v_new.reshape(b, t, D_MODEL), wts["slopes"], *tail, tk=attn_tile,
                                          lam_init=lam_init)
            xf = matmul_norm_res(o.reshape(m, D_MODEL), wts["b_w_out"][j], gains[1], xf, tm=tm)
        xf = mlp_block(xf, gains[2], wts["mlp_w1"][l], wts["mlp_w2"][l], gains[3], tm=tm, tf=2048)
    kv_shape = (b, t, HEADS, HEAD_DIM)
    return (xf.reshape(b, t, D_MODEL), jnp.stack(new_conv), jnp.stack(new_ssm),
            k_new.reshape(kv_shape), v_new.reshape(kv_shape))


def kernel(x_prompt, x_sample, state_conv, state_ssm, cache_k, cache_v, norm_gains, a_w_in, a_conv_w, a_log,
           a_dt_bias, a_o_gain, a_w_out, kv_gain, w_kv, b_w_q, b_lam, b_sub_gain, b_w_out, mlp_w1, mlp_w2):
    a_in = a_w_in.shape[-1]
    gate_par = jnp.zeros((N_A, 2, LANES), F32)
    gate_par = gate_par.at[:, 0, HEADS:2 * HEADS].set(a_log.astype(F32))
    gate_par = gate_par.at[:, 1, HEADS:2 * HEADS].set(a_dt_bias.astype(F32))
    slopes = jnp.asarray(alibi_slopes())
    wts = dict(
        norm_gains=norm_gains.astype(F32),
        a_w_in=jnp.pad(a_w_in, ((0, 0), (0, 0), (0, A_IN_PAD - a_in))).astype(BF16),
        a_conv_w=a_conv_w.astype(F32),
        gate_par=gate_par,
        a_o_gain=a_o_gain.astype(F32),
        a_w_out=a_w_out.astype(BF16),
        kv_gain=kv_gain.astype(F32),
        w_k=w_kv[:, :D_MODEL].astype(BF16),
        w_v=w_kv[:, D_MODEL:].astype(BF16),
        b_w_q=b_w_q.astype(BF16),
        b_lam=b_lam.astype(F32),
        b_sub_gain=b_sub_gain.astype(F32),
        b_w_out=b_w_out.astype(BF16),
        mlp_w1=mlp_w1.astype(BF16),
        mlp_w2=mlp_w2.astype(BF16),
        slopes=jnp.broadcast_to(slopes[:, None, None], (HEADS, 1, LANES)),
    )
    bp, tp, _ = x_prompt.shape
    conv0 = jnp.zeros((N_A, bp, CONV_W - 1, A_QKV), F32)
    ssm0 = jnp.zeros((N_A, bp, HEADS, HEAD_DIM, HEAD_DIM), F32)
    y_p, p_conv, p_ssm, p_k, p_v = _trunk(x_prompt, conv0, ssm0, None, None, wts,
                                          tm=512, gdn_chunk=CHUNK, gdn_group=4, attn_tile=512, attn_q_tile=1024)
    bs, ts, _ = x_sample.shape
    y_s, s_conv, s_ssm, s_k, s_v = _trunk(x_sample, state_conv, state_ssm, cache_k, cache_v,
                                          wts, tm=bs * ts, gdn_chunk=ts, gdn_group=1, attn_tile=512)
    return (y_p, y_s, p_conv, p_ssm, p_k, p_v, s_conv, s_ssm, s_k, s_v)
```

```python
import functools
import math
from typing import NamedTuple

import jax
import jax.numpy as jnp
import numpy as np
from jax import lax
from jax.experimental import pallas as pl
from jax.experimental.pallas import tpu as pltpu

F32 = jnp.float32
BF16 = jnp.bfloat16

D_MODEL = 1024
DEPTH = 4
N_A = DEPTH // 2
CHUNK = 64
CHUNK_SHIFT = CHUNK.bit_length() - 1
assert 1 << CHUNK_SHIFT == CHUNK
HEADS = 8
HEAD_DIM = D_MODEL // HEADS
CONV_W = 4
A_QKV = 3 * D_MODEL
A_GATE_COL = A_QKV + D_MODEL
LANES = 128
A_IN_PAD = A_GATE_COL + LANES
MAP_DIM = HEAD_DIM // 2
EPS = 1e-6
NEG = -1e30
VMEM_LIMIT = 48 * 1024 * 1024
HIGHEST = lax.Precision.HIGHEST
ONES_ROWS = 16
HEAD_ROWS_T = HEAD_DIM + ONES_ROWS
LOG2E = math.log2(math.e)
Q_SCALE = MAP_DIM ** -0.5 * LOG2E


def _dot(a, b, precision=None):
    return jnp.dot(a, b, preferred_element_type=F32, precision=precision)


def _dot_nt(a, b):
    return lax.dot_general(a, b, (((1,), (1,)), ((), ())), preferred_element_type=F32)


def _dot_tn(a, b):
    return lax.dot_general(a, b, (((0,), (0,)), ((), ())), preferred_element_type=F32)


def _split(a):
    hi = a.astype(BF16)
    return hi, (a - hi.astype(F32)).astype(BF16)


def _dot_split(a, b):
    return _dot(a[0], b[0]) + (_dot(a[0], b[1]) + _dot(a[1], b[0]))


def _rms(x, gain):
    return x * lax.rsqrt(jnp.mean(x * x, axis=-1, keepdims=True) + EPS) * gain


def _sigmoid(x):
    return 1.0 / (1.0 + jnp.exp(-x))


def _params(*semantics):
    return pltpu.CompilerParams(dimension_semantics=semantics, vmem_limit_bytes=VMEM_LIMIT)


def _norm_matmul_kernel(x_ref, g_ref, w_ref, *out_and_scratch, out_scale, head_transposed_copy):
    *o_refs, hn_ref = out_and_scratch

    @pl.when(pl.program_id(1) == 0)
    def _():
        hn_ref[...] = _rms(x_ref[...], g_ref[...]).astype(BF16)

    y = _dot(hn_ref[...], w_ref[...])
    if out_scale != 1.0:
        y = y * out_scale
    if head_transposed_copy:
        *o_refs, ot_ref = o_refs
        y_t = y.T.astype(ot_ref.dtype)
        for h in range(y_t.shape[0] // HEAD_DIM):
            r0 = h * HEAD_ROWS_T
            ot_ref[r0:r0 + HEAD_DIM, :] = y_t[h * HEAD_DIM:(h + 1) * HEAD_DIM]
            ot_ref[r0 + HEAD_DIM:r0 + HEAD_ROWS_T, :] = jnp.ones((ONES_ROWS, y_t.shape[1]), ot_ref.dtype)
    for o_ref in o_refs:
        o_ref[...] = y.astype(o_ref.dtype)


def norm_matmul(x, gain, w, *, tm, tn, out_dtypes=(F32,), out_scale=1.0, head_transposed_copy=False):
    m, k = x.shape
    n = w.shape[1]
    out_specs = [pl.BlockSpec((tm, tn), lambda i, j: (i, j)) for _ in out_dtypes]
    out_shape = [jax.ShapeDtypeStruct((m, n), dt) for dt in out_dtypes]
    if head_transposed_copy:
        out_specs.append(pl.BlockSpec((tn // HEAD_DIM * HEAD_ROWS_T, tm), lambda i, j: (j, i)))
        out_shape.append(jax.ShapeDtypeStruct((n // HEAD_DIM * HEAD_ROWS_T, m), BF16))
    outs = pl.pallas_call(
        functools.partial(_norm_matmul_kernel, out_scale=out_scale, head_transposed_copy=head_transposed_copy),
        grid=(m // tm, n // tn),
        in_specs=[
            pl.BlockSpec((tm, k), lambda i, j: (i, 0)),
            pl.BlockSpec((1, k), lambda i, j: (0, 0)),
            pl.BlockSpec((k, tn), lambda i, j: (0, j)),
        ],
        out_specs=out_specs,
        out_shape=out_shape,
        scratch_shapes=[pltpu.VMEM((tm, k), BF16)],
        compiler_params=_params("parallel", "arbitrary"),
        name="norm_matmul",
    )(x, gain.reshape(1, k), w)
    return outs[0] if len(outs) == 1 else outs


def _matmul_norm_res_kernel(a_ref, w_ref, g_ref, x_ref, o_ref):
    y = _dot(a_ref[...], w_ref[...])
    o_ref[...] = x_ref[...] + _rms(y, g_ref[...])


def matmul_norm_res(a, w, gain, x, *, tm):
    m, k = a.shape
    d = w.shape[1]
    return pl.pallas_call(
        _matmul_norm_res_kernel,
        grid=(m // tm,),
        in_specs=[
            pl.BlockSpec((tm, k), lambda i: (i, 0)),
            pl.BlockSpec((k, d), lambda i: (0, 0)),
            pl.BlockSpec((1, d), lambda i: (0, 0)),
            pl.BlockSpec((tm, d), lambda i: (i, 0)),
        ],
        out_specs=pl.BlockSpec((tm, d), lambda i: (i, 0)),
        out_shape=jax.ShapeDtypeStruct((m, d), F32),
        compiler_params=_params("parallel"),
        name="matmul_norm_res",
    )(a, w, gain.reshape(1, d), x)


def _mlp_kernel(x_ref, g_in_ref, w1_ref, w2_ref, g_out_ref, o_ref, hn_ref, acc_ref):
    f = pl.program_id(1)

    @pl.when(f == 0)
    def _():
        hn_ref[...] = _rms(x_ref[...], g_in_ref[...]).astype(BF16)
        acc_ref[...] = jnp.zeros_like(acc_ref)

    h = _dot(hn_ref[...], w1_ref[...])
    h = jnp.square(jnp.maximum(h, 0.0)).astype(BF16)
    acc_ref[...] += _dot(h, w2_ref[...])

    @pl.when(f == pl.num_programs(1) - 1)
    def _():
        o_ref[...] = x_ref[...] + _rms(acc_ref[...], g_out_ref[...])


def mlp_block(x, g_in, w1, w2, g_out, *, tm, tf):
    m, d = x.shape
    ff = w1.shape[1]
    return pl.pallas_call(
        _mlp_kernel,
        grid=(m // tm, ff // tf),
        in_specs=[
            pl.BlockSpec((tm, d), lambda i, f: (i, 0)),
            pl.BlockSpec((1, d), lambda i, f: (0, 0)),
            pl.BlockSpec((d, tf), lambda i, f: (0, f)),
            pl.BlockSpec((tf, d), lambda i, f: (f, 0)),
            pl.BlockSpec((1, d), lambda i, f: (0, 0)),
        ],
        out_specs=pl.BlockSpec((tm, d), lambda i, f: (i, 0)),
        out_shape=jax.ShapeDtypeStruct((m, d), F32),
        scratch_shapes=[pltpu.VMEM((tm, d), BF16), pltpu.VMEM((tm, d), F32)],
        compiler_params=_params("parallel", "arbitrary"),
        name="mlp_block",
    )(x, g_in.reshape(1, d), w1, w2, g_out.reshape(1, d))


CONV_ROW0 = 8
UNIT_GROUP = 16


def _gdn_prep_kernel(proj_ref, conv0_ref, cw_ref, gp_ref,
                     w_ref, uu_ref, qg_ref, kd_ref, qkd_ref, egl_ref, conv_ref, xbuf_ref, *, C, G):
    R = C * G
    prev0 = CONV_ROW0 - (CONV_W - 1)

    @pl.when(pl.program_id(1) == 0)
    def _():
        xbuf_ref[prev0:CONV_ROW0, :] = conv0_ref[0]

    xbuf_ref[CONV_ROW0:CONV_ROW0 + R, :] = proj_ref[0, :, 0:A_QKV]
    cw = cw_ref[...]
    y = xbuf_ref[prev0:prev0 + R, :] * cw[0:1, :]
    for j in range(1, CONV_W):
        y = y + xbuf_ref[prev0 + j:prev0 + j + R, :] * cw[j:j + 1, :]
    y = y * _sigmoid(y)
    last = xbuf_ref[prev0 + R:CONV_ROW0 + R, :]
    xbuf_ref[prev0:CONV_ROW0, :] = last
    conv_ref[0] = last

    tail = proj_ref[0, :, A_GATE_COL:A_IN_PAD]
    gp = gp_ref[...]
    beta = _sigmoid(tail)
    t = tail + gp[1:2, :]
    softplus = jnp.maximum(t, 0.0) + jnp.log(1.0 + jnp.exp(-jnp.abs(t)))
    g = -jnp.exp(gp[0:1, :]) * softplus

    rr = lax.broadcasted_iota(jnp.int32, (R, R), 0)
    cc = lax.broadcasted_iota(jnp.int32, (R, R), 1)
    shift = C.bit_length() - 1
    same = (rr >> shift) == (cc >> shift)
    gc = _dot(jnp.where(same & (rr >= cc), 1.0, 0.0), g, HIGHEST)
    gl = _dot(jnp.where(same, 1.0, 0.0), g, HIGHEST)
    pad_rows = -R % LANES
    gc_t = (jnp.concatenate([gc, jnp.zeros((pad_rows, LANES), F32)], axis=0) if pad_rows else gc).T
    eg = jnp.exp(gc)
    ekd = jnp.exp(gl - gc)
    egl = jnp.exp(gl)
    for gi in range(G):
        egl_ref[0, gi] = egl[gi * C:gi * C + 1, :]

    row = lax.broadcasted_iota(jnp.int32, (C, C), 0)
    col = lax.broadcasted_iota(jnp.int32, (C, C), 1)
    incl = row >= col
    strict = row > col
    eye = (row == col).astype(F32)

    units = [(gi, h) for gi in range(G) for h in range(HEADS)]
    qn, kn = [], []
    for h in range(HEADS):
        q = y[:, h * HEAD_DIM:(h + 1) * HEAD_DIM]
        k = y[:, D_MODEL + h * HEAD_DIM:D_MODEL + (h + 1) * HEAD_DIM]
        qn.append(q * lax.rsqrt(jnp.sum(q * q, axis=-1, keepdims=True) + EPS) * (HEAD_DIM ** -0.5))
        kn.append(k * lax.rsqrt(jnp.sum(k * k, axis=-1, keepdims=True) + EPS))

    def rows(gi):
        return slice(gi * C, (gi + 1) * C)

    def process(units):
        decay, qk, a_mat = {}, {}, {}
        for gi, h in units:
            rs, gl_h = rows(gi), HEADS + h
            g_col = gc[rs, gl_h:gl_h + 1]
            g_row = gc_t[gl_h:gl_h + 1, gi * C:(gi + 1) * C]
            decay[gi, h] = jnp.where(incl, jnp.exp(jnp.minimum(g_col - g_row, 0.0)), 0.0)
            k_bf = kn[h][rs].astype(BF16)
            qk_kk = _dot_nt(jnp.concatenate([qn[h][rs].astype(BF16), k_bf], axis=0), k_bf)
            qk[gi, h] = qk_kk[0:C]
            a_mat[gi, h] = jnp.where(strict, beta[rs, h:h + 1] * qk_kk[C:2 * C] * decay[gi, h], 0.0)
        t_inv = {u: eye - a_mat[u] for u in units}
        pw = {}
        for u in units:
            a_s = _split(a_mat[u])
            pw[u] = _dot_split(a_s, a_s)
        n_double = C.bit_length() - 2
        for step in range(n_double):
            for u in units:
                p_s = _split(pw[u])
                t_s = _split(t_inv[u])
                if step + 1 < n_double:
                    tp = _dot_split(tuple(jnp.concatenate([t_part, p_part], axis=0)
                                          for t_part, p_part in zip(t_s, p_s)), p_s)
                    t_inv[u] = t_inv[u] + tp[0:C]
                    pw[u] = tp[C:2 * C]
                else:
                    t_inv[u] = t_inv[u] + _dot_split(t_s, p_s)
        for gi, h in units:
            rs, gl_h = rows(gi), HEADS + h
            hs = slice(h * HEAD_DIM, (h + 1) * HEAD_DIM)
            b_col = beta[rs, h:h + 1]
            eg_col = eg[rs, gl_h:gl_h + 1]
            k = kn[h][rs]
            v = y[rs, 2 * D_MODEL + h * HEAD_DIM:2 * D_MODEL + (h + 1) * HEAD_DIM]
            rhs = jnp.concatenate([b_col * v, (b_col * eg_col) * k], axis=-1).astype(BF16)
            sol = _dot(t_inv[gi, h].astype(BF16), rhs)
            uu_ref[0, rs, hs] = sol[:, 0:HEAD_DIM].astype(uu_ref.dtype)
            w_ref[0, rs, hs] = sol[:, HEAD_DIM:].astype(w_ref.dtype)
            qg_ref[0, rs, hs] = (eg_col * qn[h][rs]).astype(qg_ref.dtype)
            kd_ref[0, rs, hs] = (ekd[rs, gl_h:gl_h + 1] * k).astype(kd_ref.dtype)
            qkd_ref[0, rs, h * C:(h + 1) * C] = (qk[gi, h] * decay[gi, h]).astype(qkd_ref.dtype)

    for g0 in range(0, len(units), UNIT_GROUP):
        process(units[g0:g0 + UNIT_GROUP])


def gdn_prep(proj, conv0, conv_w, gate_par, *, C, G):
    b, t, _ = proj.shape
    r = C * G
    nc = t // C
    row_spec = pl.BlockSpec((1, r, D_MODEL), lambda i, c: (i, c, 0))
    conv_spec = pl.BlockSpec((1, CONV_W - 1, A_QKV), lambda i, c: (i, 0, 0))
    return pl.pallas_call(
        functools.partial(_gdn_prep_kernel, C=C, G=G),
        grid=(b, t // r),
        in_specs=[
            pl.BlockSpec((1, r, A_IN_PAD), lambda i, c: (i, c, 0)),
            conv_spec,
            pl.BlockSpec((CONV_W, A_QKV), lambda i, c: (0, 0)),
            pl.BlockSpec((2, LANES), lambda i, c: (0, 0)),
        ],
        out_specs=[
            row_spec, row_spec, row_spec, row_spec,
            pl.BlockSpec((1, r, HEADS * C), lambda i, c: (i, c, 0)),
            pl.BlockSpec((1, G, 1, LANES), lambda i, c: (i, c, 0, 0)),
            conv_spec,
        ],
        out_shape=[jax.ShapeDtypeStruct((b, t, D_MODEL), BF16)] * 4 + [
            jax.ShapeDtypeStruct((b, t, HEADS * C), BF16),
            jax.ShapeDtypeStruct((b, nc, 1, LANES), F32),
            jax.ShapeDtypeStruct((b, CONV_W - 1, A_QKV), F32),
        ],
        scratch_shapes=[pltpu.VMEM((CONV_ROW0 + r, A_QKV), F32)],
        compiler_params=_params("arbitrary", "arbitrary"),
        name="gdn_prep",
    )(proj, conv0, conv_w, gate_par)


def _gdn_scan_kernel(w_ref, uu_ref, qg_ref, kd_ref, qkd_ref, egl_ref, z_ref, s0_ref, og_ref, o_ref, s_ref, *, C, G):
    @pl.when(pl.program_id(1) == 0)
    def _():
        s_ref[0] = s0_ref[0]

    heads = range(HEADS)
    for gi in range(G):
        rs = slice(gi * C, (gi + 1) * C)
        egl = egl_ref[0, gi]

        def hs(h):
            return slice(h * HEAD_DIM, (h + 1) * HEAD_DIM)

        s_old = [s_ref[0, h] for h in heads]
        s_bf = [s.astype(BF16) for s in s_old]
        wq = [_dot(jnp.concatenate([w_ref[0, rs, hs(h)], qg_ref[0, rs, hs(h)]], axis=0), s_bf[h]) for h in heads]
        u_bf = [(uu_ref[0, rs, hs(h)].astype(F32) - wq[h][0:C]).astype(BF16) for h in heads]
        o = [wq[h][C:2 * C] + _dot(qkd_ref[0, rs, h * C:(h + 1) * C], u_bf[h]) for h in heads]
        for h in heads:
            s_ref[0, h] = egl[:, HEADS + h:HEADS + h + 1] * s_old[h] + _dot_tn(kd_ref[0, rs, hs(h)], u_bf[h])
        for h in heads:
            z = z_ref[0, rs, hs(h)]
            o_ref[0, rs, hs(h)] = (_rms(o[h], og_ref[...]) * (z * _sigmoid(z))).astype(o_ref.dtype)


def gdn_scan(w, uu, qg, kd, qkd, egl, proj, s0, o_gain, *, C, G):
    b, t, _ = w.shape
    r = C * G
    row_spec = pl.BlockSpec((1, r, D_MODEL), lambda i, c: (i, c, 0))
    state_spec = pl.BlockSpec((1, HEADS, HEAD_DIM, HEAD_DIM), lambda i, c: (i, 0, 0, 0))
    return pl.pallas_call(
        functools.partial(_gdn_scan_kernel, C=C, G=G),
        grid=(b, t // r),
        in_specs=[
            row_spec, row_spec, row_spec, row_spec,
            pl.BlockSpec((1, r, HEADS * C), lambda i, c: (i, c, 0)),
            pl.BlockSpec((1, G, 1, LANES), lambda i, c: (i, c, 0, 0)),
            pl.BlockSpec((1, r, D_MODEL), lambda i, c: (i, c, A_QKV // D_MODEL)),
            state_spec,
            pl.BlockSpec((1, HEAD_DIM), lambda i, c: (0, 0)),
        ],
        out_specs=[row_spec, state_spec],
        out_shape=[
            jax.ShapeDtypeStruct((b, t, D_MODEL), BF16),
            jax.ShapeDtypeStruct((b, HEADS, HEAD_DIM, HEAD_DIM), F32),
        ],
        compiler_params=_params("arbitrary", "arbitrary"),
        name="gdn_scan",
    )(w, uu, qg, kd, qkd, egl, proj, s0, o_gain.reshape(1, HEAD_DIM))


def _lambda(lam_ref, lam_init):
    lp = lam_ref[...]
    return (jnp.exp(jnp.sum(lp[0:1] * lp[1:2], axis=-1, keepdims=True))
            - jnp.exp(jnp.sum(lp[2:3] * lp[3:4], axis=-1, keepdims=True)) + lam_init)


DECODE_HEAD_GROUP = HEADS
KEY_COLS = 3


def _attn_prompt_kernel(qi_ref, kj_ref, slope_ref, lam_ref, sg_ref, kx_ref, q_ref, k_ref, vt_ref, o_ref,
                        s_even_ref, s_odd_ref, smax_even_ref, smax_odd_ref, diag_ref, m_ref, acc_ref,
                        *, tq, tk, col_block, n_pairs, lam_init):
    step = pl.program_id(1)
    slope = slope_ref[0, :, 0:1] * LOG2E
    lane = lax.broadcasted_iota(jnp.int32, (1, HEAD_DIM), 1)
    map_lanes = [lane < MAP_DIM, lane >= MAP_DIM]

    def score_stage(s_ref, smax_ref, mp, cols):
        q = q_ref[cols, :]
        k = k_ref[...]
        a0 = MAP_DIM if mp == 0 else 0
        extra = (lane >= a0) & (lane < a0 + KEY_COLS)
        qa = jnp.where(map_lanes[mp], q, jnp.where(extra, 1.0, 0.0).astype(BF16))
        ka = jnp.where(map_lanes[mp], k, kx_ref[0, mp])
        scores = _dot_nt(ka, qa)
        s_ref[mp, :, cols] = scores
        smax_ref[mp, :, cols] = jnp.max(scores, axis=0, keepdims=True)

    def mask_diagonal(s_ref, smax_ref, key_offset):
        facing = slice(key_offset, key_offset + tk)
        for mp in range(2):
            if key_offset:
                s_ref[mp, :, 0:key_offset] = jnp.full((tk, key_offset), NEG, F32)
                smax_ref[mp, :, 0:key_offset] = jnp.full((1, key_offset), NEG, F32)
            fixed = s_ref[mp, :, facing] + diag_ref[...]
            s_ref[mp, :, facing] = fixed
            smax_ref[mp, :, facing] = jnp.max(fixed, axis=0, keepdims=True)

    def build_diagonal_table():
        j = lax.broadcasted_iota(jnp.int32, (tk, 1), 0)
        i = lax.broadcasted_iota(jnp.int32, (1, tk), 1)
        allowed = (j >> CHUNK_SHIFT) <= (i >> CHUNK_SHIFT)
        fix = jnp.where(j > i, (2.0 * slope) * (i - j).astype(F32), 0.0)
        diag_ref[...] = jnp.where(allowed, fix, NEG)

    def softmax_stage(s_ref, smax_ref, shift_const, mp, cols):
        scores = s_ref[mp, :, cols]
        m_old = m_ref[mp, :, cols]
        m_new = jnp.maximum(m_old, smax_ref[mp, :, cols] + shift_const)
        probs = jnp.exp2(scores - (m_new - shift_const)).astype(BF16)
        m_ref[mp, :, cols] = m_new
        acc_ref[mp, :, cols] = jnp.exp2(m_old - m_new) * acc_ref[mp, :, cols] + _dot(vt_ref[...], probs)

    def finalize():
        lam = _lambda(lam_ref, lam_init)
        a0 = acc_ref[0]
        a1 = acc_ref[1]
        o_t = (a0[0:HEAD_DIM] / a0[HEAD_DIM:HEAD_DIM + 1]
               - lam * (a1[0:HEAD_DIM] / a1[HEAD_DIM:HEAD_DIM + 1]))
        o_t = o_t * (lax.rsqrt(jnp.mean(o_t * o_t, axis=0, keepdims=True) + EPS) * (1.0 - lam_init))
        o_ref[...] = (o_t.T * sg_ref[...]).astype(o_ref.dtype)

    units = [(mp, slice(c0, c0 + col_block)) for mp in range(2) for c0 in range(0, tq, col_block)]
    scored = jnp.minimum(step, n_pairs - 1)
    key_offset_s = kj_ref[scored] * tk - qi_ref[scored] * tq
    consumed = jnp.maximum(step - 1, 0)
    kj_c = kj_ref[consumed]
    key_offset_c = kj_c * tk - qi_ref[consumed] * tq

    @pl.when(step == 0)
    def _():
        build_diagonal_table()
        for mp, cols in units:
            score_stage(s_even_ref, smax_even_ref, mp, cols)

    @pl.when((step > 0) & (kj_c == 0))
    def _():
        m_ref[...] = jnp.full_like(m_ref, NEG)
        acc_ref[...] = jnp.zeros_like(acc_ref)

    tile_shift = slope * key_offset_c.astype(F32)
    even, odd = (s_even_ref, smax_even_ref), (s_odd_ref, smax_odd_ref)
    for parity, (written, read) in enumerate([(even, odd), (odd, even)]):
        @pl.when((step > 0) & (step % 2 == parity))
        def _():
            for mp, cols in units:
                score_stage(*written, mp, cols)
                softmax_stage(*read, tile_shift, mp, cols)

    for parity, written in enumerate([even, odd]):
        for key_offset in range(0, tq, tk):
            @pl.when((key_offset_s == key_offset) & (step % 2 == parity))
            def _():
                mask_diagonal(*written, key_offset)

    @pl.when((step > 0) & (key_offset_c == tq - tk))
    def _():
        finalize()


def alibi_slopes():
    return 2.0 ** (-8.0 * np.arange(1, HEADS + 1, dtype=np.float32) / HEADS)


def alibi_key_columns(t):
    term = (np.float32(LOG2E) * alibi_slopes())[:, None] * np.arange(t, dtype=np.float32)[None, :]
    cols = np.zeros((HEADS, 2, t, HEAD_DIM), np.float32)
    for c in range(KEY_COLS):
        piece = term.astype(BF16).astype(np.float32)
        term = term - piece
        for mp in range(2):
            a0 = MAP_DIM if mp == 0 else 0
            cols[:, mp, :, a0 + c] = piece
    return jnp.asarray(cols.astype(BF16))


def diff_attention_prompt(q, k, v_t, slopes, key_cols, lam_p, sub_gain, *, tq, tk, lam_init):
    t_all = q.shape[0]
    assert tq % tk == 0 and tk % CHUNK == 0 and t_all % tq == 0
    per_q = tq // tk
    qi = np.concatenate([np.full((i + 1) * per_q, i, np.int32) for i in range(t_all // tq)])
    kj = np.concatenate([np.arange((i + 1) * per_q, dtype=np.int32) for i in range(t_all // tq)])
    n_pairs = len(qi)

    def scored(p):
        return jnp.minimum(p, n_pairs - 1)

    def consumed(p):
        return jnp.maximum(p - 1, 0)

    grid_spec = pltpu.PrefetchScalarGridSpec(
        num_scalar_prefetch=2,
        grid=(HEADS, n_pairs + 1),
        in_specs=[
            pl.BlockSpec((1, 1, LANES), lambda h, p, qi_r, kj_r: (h, 0, 0)),
            pl.BlockSpec((4, MAP_DIM), lambda h, p, qi_r, kj_r: (0, 0)),
            pl.BlockSpec((1, HEAD_DIM), lambda h, p, qi_r, kj_r: (0, 0)),
            pl.BlockSpec((1, 2, tk, HEAD_DIM), lambda h, p, qi_r, kj_r: (h, 0, 0, 0)),
            pl.BlockSpec((tq, HEAD_DIM), lambda h, p, qi_r, kj_r: (qi_r[scored(p)], h)),
            pl.BlockSpec((tk, HEAD_DIM), lambda h, p, qi_r, kj_r: (kj_r[scored(p)], h)),
            pl.BlockSpec((HEAD_ROWS_T, tk), lambda h, p, qi_r, kj_r: (h, kj_r[consumed(p)])),
        ],
        out_specs=pl.BlockSpec((tq, HEAD_DIM), lambda h, p, qi_r, kj_r: (qi_r[consumed(p)], h)),
        scratch_shapes=[
            pltpu.VMEM((2, tk, tq), F32), pltpu.VMEM((2, tk, tq), F32),
            pltpu.VMEM((2, 1, tq), F32), pltpu.VMEM((2, 1, tq), F32),
            pltpu.VMEM((tk, tk), F32),
            pltpu.VMEM((2, 1, tq), F32), pltpu.VMEM((2, HEAD_ROWS_T, tq), F32),
        ],
    )
    return pl.pallas_call(
        functools.partial(_attn_prompt_kernel, tq=tq, tk=tk, col_block=tq, n_pairs=n_pairs, lam_init=lam_init),
        grid_spec=grid_spec,
        out_shape=jax.ShapeDtypeStruct((t_all, D_MODEL), BF16),
        compiler_params=_params("arbitrary", "arbitrary"),
        name="diff_attention_prompt",
    )(jnp.asarray(qi), jnp.asarray(kj), slopes, lam_p, sub_gain.reshape(1, HEAD_DIM), key_cols, q, k, v_t)


def _attn_decode_kernel(slope_ref, lam_ref, sg_ref, q_ref, k_ref, v_ref, kn_ref, vn_ref, o_ref, m_ref, l_ref, acc_ref,
                        *, tk, pos0, lam_init):
    j = pl.program_id(1)
    nk = pl.num_programs(1)
    n = q_ref.shape[1]

    @pl.when(j == 0)
    def _():
        m_ref[...] = jnp.full_like(m_ref, NEG)
        l_ref[...] = jnp.zeros_like(l_ref)
        acc_ref[...] = jnp.zeros_like(acc_ref)

    row = lax.broadcasted_iota(jnp.int32, (2 * n, 1), 0)
    q_pos = pos0 + jnp.where(row < n, row, row - n)
    lane = lax.broadcasted_iota(jnp.int32, (1, HEAD_DIM), 1)
    map_of_row_has_lane = (row < n) == (lane < MAP_DIM)

    def update(keys, values, k0, n_keys):
        k_pos = k0 + lax.broadcasted_iota(jnp.int32, (1, n_keys), 1)
        allowed = (k_pos >> CHUNK_SHIFT) <= (q_pos >> CHUNK_SHIFT)
        dist = jnp.abs(q_pos - k_pos).astype(F32)
        for h0 in range(0, HEADS, DECODE_HEAD_GROUP):
            heads = range(h0, h0 + DECODE_HEAD_GROUP)
            scores, probs, alphas = {}, {}, {}
            for h in heads:
                slope = slope_ref[h, :, 0:1] * LOG2E
                q = q_ref[0, :, h * HEAD_DIM:(h + 1) * HEAD_DIM]
                q2 = jnp.concatenate([q, q], axis=0)
                q_stack = jnp.where(map_of_row_has_lane, q2, jnp.zeros_like(q2))
                scores[h] = jnp.where(allowed, _dot_nt(q_stack, keys(h).astype(BF16)) - slope * dist, NEG)
            for h in heads:
                m_old = m_ref[h]
                m_new = jnp.maximum(m_old, jnp.max(scores[h], axis=-1, keepdims=True))
                alphas[h] = jnp.exp2(m_old - m_new)
                p = jnp.exp2(scores[h] - m_new)
                l_ref[h] = alphas[h] * l_ref[h] + jnp.sum(p, axis=-1, keepdims=True)
                probs[h] = p.astype(BF16)
                m_ref[h] = m_new
            pv = {h: _dot(probs[h], values(h).astype(BF16)) for h in heads}
            for h in heads:
                acc_ref[h] = alphas[h] * acc_ref[h] + pv[h]

    def head_rows(h):
        return pl.ds(h, tk, stride=HEADS)

    update(lambda h: k_ref[0, head_rows(h), :], lambda h: v_ref[0, head_rows(h), :], j * tk, tk)

    @pl.when(j == nk - 1)
    def _():
        update(lambda h: kn_ref[0, :, h * HEAD_DIM:(h + 1) * HEAD_DIM],
               lambda h: vn_ref[0, :, h * HEAD_DIM:(h + 1) * HEAD_DIM], pos0, n)
        lam = _lambda(lam_ref, lam_init)
        for h in range(HEADS):
            o = acc_ref[h] / l_ref[h]
            o = o[0:n] - lam * o[n:2 * n]
            o_ref[0, :, h * HEAD_DIM:(h + 1) * HEAD_DIM] = (_rms(o, sg_ref[...]) * (1.0 - lam_init)).astype(o_ref.dtype)


def diff_attention_decode(q, k, v, k_new, v_new, slopes, lam_p, sub_gain, *, tk, lam_init):
    b, n, _ = q.shape
    t_k = k.shape[1]
    assert t_k % tk == 0 and t_k % CHUNK == 0
    k = k.reshape(b, t_k * HEADS, HEAD_DIM)
    v = v.reshape(b, t_k * HEADS, HEAD_DIM)
    row_block = pl.BlockSpec((1, n, D_MODEL), lambda bi, j: (bi, 0, 0))
    kv_spec = pl.BlockSpec((1, tk * HEADS, HEAD_DIM), lambda bi, j: (bi, j, 0))
    return pl.pallas_call(
        functools.partial(_attn_decode_kernel, tk=tk, pos0=t_k, lam_init=lam_init),
        grid=(b, t_k // tk),
        in_specs=[
            pl.BlockSpec((HEADS, 1, LANES), lambda bi, j: (0, 0, 0)),
            pl.BlockSpec((4, MAP_DIM), lambda bi, j: (0, 0)),
            pl.BlockSpec((1, HEAD_DIM), lambda bi, j: (0, 0)),
            row_block, kv_spec, kv_spec, row_block, row_block,
        ],
        out_specs=row_block,
        out_shape=jax.ShapeDtypeStruct((b, n, D_MODEL), BF16),
        scratch_shapes=[
            pltpu.VMEM((HEADS, 2 * n, 1), F32),
            pltpu.VMEM((HEADS, 2 * n, 1), F32),
            pltpu.VMEM((HEADS, 2 * n, HEAD_DIM), F32),
        ],
        compiler_params=_params("arbitrary", "arbitrary"),
        name="diff_attention_decode",
    )(slopes, lam_p, sub_gain.reshape(1, HEAD_DIM), q, k, v, k_new, v_new)


class Tiles(NamedTuple):
    wide_rows: int
    rows: int
    mlp_hidden: int
    gdn_chunk: int
    gdn_group: int
    attn_q: int
    attn_k: int


def _tiles(batch, seq):
    rows = batch * seq
    if seq > CHUNK:
        return Tiles(wide_rows=512, rows=1024, mlp_hidden=2048, gdn_chunk=CHUNK, gdn_group=4, attn_q=1024, attn_k=512)
    return Tiles(wide_rows=rows, rows=rows, mlp_hidden=2048, gdn_chunk=seq, gdn_group=1, attn_q=seq, attn_k=512)


def _trunk(x, conv_state, ssm_state, past_k, past_v, wts):
    b, t, _ = x.shape
    m = b * t
    tiles = _tiles(b, t)
    xf = x.reshape(m, D_MODEL)
    new_conv, new_ssm = [], []
    k_new = v_new = k_bf = v_t = None
    for l in range(DEPTH):
        gains = wts["norm_gains"][l]
        if l < N_A:
            proj = norm_matmul(xf, gains[0], wts["a_w_in"][l], tm=tiles.wide_rows, tn=A_IN_PAD).reshape(b, t, A_IN_PAD)
            w, uu, qg, kd, qkd, egl, conv_l = gdn_prep(proj, conv_state[l], wts["a_conv_w"][l], wts["gate_par"][l],
                                                       C=tiles.gdn_chunk, G=tiles.gdn_group)
            og, s_l = gdn_scan(w, uu, qg, kd, qkd, egl, proj, ssm_state[l], wts["a_o_gain"][l],
                               C=tiles.gdn_chunk, G=tiles.gdn_group)
            new_conv.append(conv_l)
            new_ssm.append(s_l)
            xf = matmul_norm_res(og.reshape(m, D_MODEL), wts["a_w_out"][l], gains[1], xf, tm=tiles.rows)
        else:
            j = l - N_A
            prompt = past_k is None
            if j == 0:
                k_new, k_bf = norm_matmul(xf, wts["kv_gain"], wts["w_k"], tm=tiles.rows, tn=D_MODEL, out_dtypes=(F32, BF16))
                v_new = norm_matmul(xf, wts["kv_gain"], wts["w_v"], tm=tiles.rows, tn=D_MODEL, head_transposed_copy=prompt)
                if prompt:
                    v_new, v_t = v_new
            q = norm_matmul(xf, gains[0], wts["b_w_q"][j], tm=tiles.rows, tn=D_MODEL, out_dtypes=(BF16,), out_scale=Q_SCALE)
            lam_init = 0.8 - 0.6 * math.exp(-0.3 * l)
            tail = (wts["b_lam"][j], wts["b_sub_gain"][j])
            if prompt:
                assert b == 1
                o = diff_attention_prompt(q, k_bf, v_t, wts["slopes"], alibi_key_columns(tiles.attn_k),
                                          *tail, tq=tiles.attn_q, tk=tiles.attn_k, lam_init=lam_init)
            else:
                o = diff_attention_decode(q.reshape(b, t, D_MODEL), past_k, past_v, k_new.reshape(b, t, D_MODEL),
                                          v_new.reshape(b, t, D_MODEL), wts["slopes"], *tail, tk=tiles.attn_k,
                                          lam_init=lam_init)
            xf = matmul_norm_res(o.reshape(m, D_MODEL), wts["b_w_out"][j], gains[1], xf, tm=tiles.rows)
        xf = mlp_block(xf, gains[2], wts["mlp_w1"][l], wts["mlp_w2"][l], gains[3], tm=tiles.wide_rows, tf=tiles.mlp_hidden)
    kv_shape = (b, t, HEADS, HEAD_DIM)
    return (xf.reshape(b, t, D_MODEL), jnp.stack(new_conv), jnp.stack(new_ssm),
            k_new.reshape(kv_shape), v_new.reshape(kv_shape))


def kernel(x_prompt, x_sample, state_conv, state_ssm, cache_k, cache_v, norm_gains, a_w_in, a_conv_w, a_log,
           a_dt_bias, a_o_gain, a_w_out, kv_gain, w_kv, b_w_q, b_lam, b_sub_gain, b_w_out, mlp_w1, mlp_w2):
    a_in = a_w_in.shape[-1]
    gate_par = jnp.zeros((N_A, 2, LANES), F32)
    gate_par = gate_par.at[:, 0, HEADS:2 * HEADS].set(a_log.astype(F32))
    gate_par = gate_par.at[:, 1, HEADS:2 * HEADS].set(a_dt_bias.astype(F32))
    slopes = jnp.asarray(alibi_slopes())
    wts = dict(
        norm_gains=norm_gains.astype(F32),
        a_w_in=jnp.pad(a_w_in, ((0, 0), (0, 0), (0, A_IN_PAD - a_in))).astype(BF16),
        a_conv_w=a_conv_w.astype(F32),
        gate_par=gate_par,
        a_o_gain=a_o_gain.astype(F32),
        a_w_out=a_w_out.astype(BF16),
        kv_gain=kv_gain.astype(F32),
        w_k=w_kv[:, :D_MODEL].astype(BF16),
        w_v=w_kv[:, D_MODEL:].astype(BF16),
        b_w_q=b_w_q.astype(BF16),
        b_lam=b_lam.astype(F32),
        b_sub_gain=b_sub_gain.astype(F32),
        b_w_out=b_w_out.astype(BF16),
        mlp_w1=mlp_w1.astype(BF16),
        mlp_w2=mlp_w2.astype(BF16),
        slopes=jnp.broadcast_to(slopes[:, None, None], (HEADS, 1, LANES)),
    )
    bp = x_prompt.shape[0]
    conv0 = jnp.zeros((N_A, bp, CONV_W - 1, A_QKV), F32)
    ssm0 = jnp.zeros((N_A, bp, HEADS, HEAD_DIM, HEAD_DIM), F32)
    y_p, p_conv, p_ssm, p_k, p_v = _trunk(x_prompt, conv0, ssm0, None, None, wts)
    y_s, s_conv, s_ssm, s_k, s_v = _trunk(x_sample, state_conv, state_ssm, cache_k, cache_v, wts)
    return (y_p, y_s, p_conv, p_ssm, p_k, p_v, s_conv, s_ssm, s_k, s_v)
```

```python
import functools
import math
from typing import NamedTuple

import jax
import jax.numpy as jnp
import numpy as np
from jax import lax
from jax.experimental import pallas as pl
from jax.experimental.pallas import tpu as pltpu

F32 = jnp.float32
BF16 = jnp.bfloat16

D_MODEL = 1024
DEPTH = 4
N_A = DEPTH // 2
CHUNK = 64
CHUNK_SHIFT = CHUNK.bit_length() - 1
assert 1 << CHUNK_SHIFT == CHUNK
HEADS = 8
HEAD_DIM = D_MODEL // HEADS
CONV_W = 4
A_QKV = 3 * D_MODEL
A_GATE_COL = A_QKV + D_MODEL
LANES = 128
A_IN_PAD = A_GATE_COL + LANES
MAP_DIM = HEAD_DIM // 2
EPS = 1e-6
NEG = -1e30
VMEM_LIMIT = 48 * 1024 * 1024
HIGHEST = lax.Precision.HIGHEST
ONES_ROWS = 16
HEAD_ROWS_T = HEAD_DIM + ONES_ROWS
LOG2E = math.log2(math.e)
Q_SCALE = MAP_DIM ** -0.5 * LOG2E


def _dot(a, b, precision=None):
    return jnp.dot(a, b, preferred_element_type=F32, precision=precision)


def _dot_nt(a, b):
    return lax.dot_general(a, b, (((1,), (1,)), ((), ())), preferred_element_type=F32)


def _dot_tn(a, b):
    return lax.dot_general(a, b, (((0,), (0,)), ((), ())), preferred_element_type=F32)


def _split(a):
    hi = a.astype(BF16)
    return hi, (a - hi.astype(F32)).astype(BF16)


def _dot_split(a, b):
    return _dot(a[0], b[0]) + (_dot(a[0], b[1]) + _dot(a[1], b[0]))


def _rms(x, gain):
    return x * lax.rsqrt(jnp.mean(x * x, axis=-1, keepdims=True) + EPS) * gain


def _sigmoid(x):
    return 1.0 / (1.0 + jnp.exp(-x))


def _params(*semantics):
    return pltpu.CompilerParams(dimension_semantics=semantics, vmem_limit_bytes=VMEM_LIMIT)


def _norm_matmul_kernel(x_ref, g_ref, w_ref, *out_and_scratch, out_scale, head_transposed_copy, head_major_first):
    *o_refs, hn_ref = out_and_scratch

    @pl.when(pl.program_id(1) == 0)
    def _():
        hn_ref[...] = _rms(x_ref[...], g_ref[...]).astype(BF16)

    y = _dot(hn_ref[...], w_ref[...])
    if out_scale != 1.0:
        y = y * out_scale
    if head_transposed_copy:
        *o_refs, ot_ref = o_refs
        y_t = y.T.astype(ot_ref.dtype)
        for h in range(y_t.shape[0] // HEAD_DIM):
            r0 = h * HEAD_ROWS_T
            ot_ref[r0:r0 + HEAD_DIM, :] = y_t[h * HEAD_DIM:(h + 1) * HEAD_DIM]
            ot_ref[r0 + HEAD_DIM:r0 + HEAD_ROWS_T, :] = jnp.ones((ONES_ROWS, y_t.shape[1]), ot_ref.dtype)
    if head_major_first:
        o_ref, *o_refs = o_refs
        rows = y.shape[0]
        for h in range(y.shape[1] // HEAD_DIM):
            o_ref[pl.ds(h, rows, stride=y.shape[1] // HEAD_DIM), :] = (
                y[:, h * HEAD_DIM:(h + 1) * HEAD_DIM].astype(o_ref.dtype))
    for o_ref in o_refs:
        o_ref[...] = y.astype(o_ref.dtype)


def norm_matmul(x, gain, w, *, tm, tn, out_dtypes=(F32,), out_scale=1.0, head_transposed_copy=False, columns=None,
                layer=None, head_major_first=False):
    m, k = x.shape
    first, n = columns or (0, w.shape[-1])
    first_tile = first // tn
    if layer is None:
        w_spec = pl.BlockSpec((k, tn), lambda i, j: (0, first_tile + j))
    else:
        w_spec = pl.BlockSpec((None, k, tn), lambda i, j: (layer, 0, first_tile + j))
    out_specs = [pl.BlockSpec((tm, tn), lambda i, j: (i, j)) for _ in out_dtypes]
    out_shape = [jax.ShapeDtypeStruct((m, n), dt) for dt in out_dtypes]
    if head_major_first:
        assert tn == n
        heads = n // HEAD_DIM
        out_specs[0] = pl.BlockSpec((tm * heads, HEAD_DIM), lambda i, j: (i, 0))
        out_shape[0] = jax.ShapeDtypeStruct((m * heads, HEAD_DIM), out_dtypes[0])
    if head_transposed_copy:
        out_specs.append(pl.BlockSpec((tn // HEAD_DIM * HEAD_ROWS_T, tm), lambda i, j: (j, i)))
        out_shape.append(jax.ShapeDtypeStruct((n // HEAD_DIM * HEAD_ROWS_T, m), BF16))
    outs = pl.pallas_call(
        functools.partial(_norm_matmul_kernel, out_scale=out_scale, head_transposed_copy=head_transposed_copy,
                          head_major_first=head_major_first),
        grid=(m // tm, n // tn),
        in_specs=[
            pl.BlockSpec((tm, k), lambda i, j: (i, 0)),
            pl.BlockSpec((1, k), lambda i, j: (0, 0)),
            w_spec,
        ],
        out_specs=out_specs,
        out_shape=out_shape,
        scratch_shapes=[pltpu.VMEM((tm, k), BF16)],
        compiler_params=_params("parallel", "arbitrary"),
        name="norm_matmul",
    )(x, gain.reshape(1, k), w)
    return outs[0] if len(outs) == 1 else outs


def _matmul_norm_res_kernel(a_ref, w_ref, g_ref, x_ref, o_ref):
    y = _dot(a_ref[...], w_ref[...])
    o_ref[...] = x_ref[...] + _rms(y, g_ref[...])


def matmul_norm_res(a, w, gain, x, *, tm):
    m, k = a.shape
    d = w.shape[1]
    return pl.pallas_call(
        _matmul_norm_res_kernel,
        grid=(m // tm,),
        in_specs=[
            pl.BlockSpec((tm, k), lambda i: (i, 0)),
            pl.BlockSpec((k, d), lambda i: (0, 0)),
            pl.BlockSpec((1, d), lambda i: (0, 0)),
            pl.BlockSpec((tm, d), lambda i: (i, 0)),
        ],
        out_specs=pl.BlockSpec((tm, d), lambda i: (i, 0)),
        out_shape=jax.ShapeDtypeStruct((m, d), F32),
        compiler_params=_params("parallel"),
        name="matmul_norm_res",
    )(a, w, gain.reshape(1, d), x)


def _mlp_kernel(x_ref, g_in_ref, w1_ref, w2_ref, g_out_ref, o_ref, hn_ref, acc_ref):
    f = pl.program_id(1)

    @pl.when(f == 0)
    def _():
        hn_ref[...] = _rms(x_ref[...], g_in_ref[...]).astype(BF16)
        acc_ref[...] = jnp.zeros_like(acc_ref)

    h = _dot(hn_ref[...], w1_ref[...])
    h = jnp.square(jnp.maximum(h, 0.0)).astype(BF16)
    acc_ref[...] += _dot(h, w2_ref[...])

    @pl.when(f == pl.num_programs(1) - 1)
    def _():
        o_ref[...] = x_ref[...] + _rms(acc_ref[...], g_out_ref[...])


def mlp_block(x, g_in, w1, w2, g_out, *, layer, tm, tf):
    m, d = x.shape
    ff = w1.shape[2]
    return pl.pallas_call(
        _mlp_kernel,
        grid=(m // tm, ff // tf),
        in_specs=[
            pl.BlockSpec((tm, d), lambda i, f: (i, 0)),
            pl.BlockSpec((1, d), lambda i, f: (0, 0)),
            pl.BlockSpec((None, d, tf), lambda i, f: (layer, 0, f)),
            pl.BlockSpec((None, tf, d), lambda i, f: (layer, f, 0)),
            pl.BlockSpec((1, d), lambda i, f: (0, 0)),
        ],
        out_specs=pl.BlockSpec((tm, d), lambda i, f: (i, 0)),
        out_shape=jax.ShapeDtypeStruct((m, d), F32),
        scratch_shapes=[pltpu.VMEM((tm, d), BF16), pltpu.VMEM((tm, d), F32)],
        compiler_params=_params("parallel", "arbitrary"),
        name="mlp_block",
    )(x, g_in.reshape(1, d), w1, w2, g_out.reshape(1, d))


CONV_ROW0 = 8
UNIT_GROUP = 16


def _gdn_prep_kernel(proj_ref, conv0_ref, cw_ref, gp_ref,
                     w_ref, uu_ref, qg_ref, kd_ref, qkd_ref, egl_ref, conv_ref, xbuf_ref, *, C, G):
    R = C * G
    prev0 = CONV_ROW0 - (CONV_W - 1)

    @pl.when(pl.program_id(1) == 0)
    def _():
        xbuf_ref[prev0:CONV_ROW0, :] = conv0_ref[0]

    xbuf_ref[CONV_ROW0:CONV_ROW0 + R, :] = proj_ref[0, :, 0:A_QKV]
    cw = cw_ref[...]
    y = xbuf_ref[prev0:prev0 + R, :] * cw[0:1, :]
    for j in range(1, CONV_W):
        y = y + xbuf_ref[prev0 + j:prev0 + j + R, :] * cw[j:j + 1, :]
    y = y * _sigmoid(y)
    last = xbuf_ref[prev0 + R:CONV_ROW0 + R, :]
    xbuf_ref[prev0:CONV_ROW0, :] = last
    conv_ref[0] = last

    tail = proj_ref[0, :, A_GATE_COL:A_IN_PAD]
    gp = gp_ref[...]
    beta = _sigmoid(tail)
    t = tail + gp[1:2, :]
    softplus = jnp.maximum(t, 0.0) + jnp.log(1.0 + jnp.exp(-jnp.abs(t)))
    g = -jnp.exp(gp[0:1, :]) * softplus

    rr = lax.broadcasted_iota(jnp.int32, (R, R), 0)
    cc = lax.broadcasted_iota(jnp.int32, (R, R), 1)
    shift = C.bit_length() - 1
    same = (rr >> shift) == (cc >> shift)
    gc = _dot(jnp.where(same & (rr >= cc), 1.0, 0.0), g, HIGHEST)
    gl = _dot(jnp.where(same, 1.0, 0.0), g, HIGHEST)
    pad_rows = -R % LANES
    gc_t = (jnp.concatenate([gc, jnp.zeros((pad_rows, LANES), F32)], axis=0) if pad_rows else gc).T
    eg = jnp.exp(gc)
    ekd = jnp.exp(gl - gc)
    egl = jnp.exp(gl)
    for gi in range(G):
        egl_ref[0, gi] = egl[gi * C:gi * C + 1, :]

    row = lax.broadcasted_iota(jnp.int32, (C, C), 0)
    col = lax.broadcasted_iota(jnp.int32, (C, C), 1)
    incl = row >= col
    strict = row > col
    eye = (row == col).astype(F32)

    units = [(gi, h) for gi in range(G) for h in range(HEADS)]
    qn, kn = [], []
    for h in range(HEADS):
        q = y[:, h * HEAD_DIM:(h + 1) * HEAD_DIM]
        k = y[:, D_MODEL + h * HEAD_DIM:D_MODEL + (h + 1) * HEAD_DIM]
        qn.append(q * lax.rsqrt(jnp.sum(q * q, axis=-1, keepdims=True) + EPS) * (HEAD_DIM ** -0.5))
        kn.append(k * lax.rsqrt(jnp.sum(k * k, axis=-1, keepdims=True) + EPS))

    def rows(gi):
        return slice(gi * C, (gi + 1) * C)

    def process(units):
        decay, qk, a_mat = {}, {}, {}
        for gi, h in units:
            rs, gl_h = rows(gi), HEADS + h
            g_col = gc[rs, gl_h:gl_h + 1]
            g_row = gc_t[gl_h:gl_h + 1, gi * C:(gi + 1) * C]
            decay[gi, h] = jnp.where(incl, jnp.exp(jnp.minimum(g_col - g_row, 0.0)), 0.0)
            k_bf = kn[h][rs].astype(BF16)
            qk_kk = _dot_nt(jnp.concatenate([qn[h][rs].astype(BF16), k_bf], axis=0), k_bf)
            qk[gi, h] = qk_kk[0:C]
            a_mat[gi, h] = jnp.where(strict, beta[rs, h:h + 1] * qk_kk[C:2 * C] * decay[gi, h], 0.0)
        t_inv = {u: eye - a_mat[u] for u in units}
        pw = {}
        for u in units:
            a_s = _split(a_mat[u])
            pw[u] = _dot_split(a_s, a_s)
        n_double = C.bit_length() - 2
        for step in range(n_double):
            for u in units:
                p_s = _split(pw[u])
                t_s = _split(t_inv[u])
                if step + 1 < n_double:
                    tp = _dot_split(tuple(jnp.concatenate([t_part, p_part], axis=0)
                                          for t_part, p_part in zip(t_s, p_s)), p_s)
                    t_inv[u] = t_inv[u] + tp[0:C]
                    pw[u] = tp[C:2 * C]
                else:
                    t_inv[u] = t_inv[u] + _dot_split(t_s, p_s)
        for gi, h in units:
            rs, gl_h = rows(gi), HEADS + h
            hs = slice(h * HEAD_DIM, (h + 1) * HEAD_DIM)
            b_col = beta[rs, h:h + 1]
            eg_col = eg[rs, gl_h:gl_h + 1]
            k = kn[h][rs]
            v = y[rs, 2 * D_MODEL + h * HEAD_DIM:2 * D_MODEL + (h + 1) * HEAD_DIM]
            rhs = jnp.concatenate([b_col * v, (b_col * eg_col) * k], axis=-1).astype(BF16)
            sol = _dot(t_inv[gi, h].astype(BF16), rhs)
            uu_ref[0, rs, hs] = sol[:, 0:HEAD_DIM].astype(uu_ref.dtype)
            w_ref[0, rs, hs] = sol[:, HEAD_DIM:].astype(w_ref.dtype)
            qg_ref[0, rs, hs] = (eg_col * qn[h][rs]).astype(qg_ref.dtype)
            kd_ref[0, rs, hs] = (ekd[rs, gl_h:gl_h + 1] * k).astype(kd_ref.dtype)
            qkd_ref[0, rs, h * C:(h + 1) * C] = (qk[gi, h] * decay[gi, h]).astype(qkd_ref.dtype)

    for g0 in range(0, len(units), UNIT_GROUP):
        process(units[g0:g0 + UNIT_GROUP])


def gdn_prep(proj, conv0, conv_w, gate_par, *, C, G):
    b, t, _ = proj.shape
    r = C * G
    nc = t // C
    row_spec = pl.BlockSpec((1, r, D_MODEL), lambda i, c: (i, c, 0))
    conv_spec = pl.BlockSpec((1, CONV_W - 1, A_QKV), lambda i, c: (i, 0, 0))
    return pl.pallas_call(
        functools.partial(_gdn_prep_kernel, C=C, G=G),
        grid=(b, t // r),
        in_specs=[
            pl.BlockSpec((1, r, A_IN_PAD), lambda i, c: (i, c, 0)),
            conv_spec,
            pl.BlockSpec((CONV_W, A_QKV), lambda i, c: (0, 0)),
            pl.BlockSpec((2, LANES), lambda i, c: (0, 0)),
        ],
        out_specs=[
            row_spec, row_spec, row_spec, row_spec,
            pl.BlockSpec((1, r, HEADS * C), lambda i, c: (i, c, 0)),
            pl.BlockSpec((1, G, 1, LANES), lambda i, c: (i, c, 0, 0)),
            conv_spec,
        ],
        out_shape=[jax.ShapeDtypeStruct((b, t, D_MODEL), BF16)] * 4 + [
            jax.ShapeDtypeStruct((b, t, HEADS * C), BF16),
            jax.ShapeDtypeStruct((b, nc, 1, LANES), F32),
            jax.ShapeDtypeStruct((b, CONV_W - 1, A_QKV), F32),
        ],
        scratch_shapes=[pltpu.VMEM((CONV_ROW0 + r, A_QKV), F32)],
        compiler_params=_params("arbitrary", "arbitrary"),
        name="gdn_prep",
    )(proj, conv0, conv_w, gate_par)


def _gdn_scan_kernel(w_ref, uu_ref, qg_ref, kd_ref, qkd_ref, egl_ref, z_ref, s0_ref, og_ref, o_ref, s_ref, *, C, G):
    @pl.when(pl.program_id(1) == 0)
    def _():
        s_ref[0] = s0_ref[0]

    heads = range(HEADS)
    for gi in range(G):
        rs = slice(gi * C, (gi + 1) * C)
        egl = egl_ref[0, gi]

        def hs(h):
            return slice(h * HEAD_DIM, (h + 1) * HEAD_DIM)

        s_old = [s_ref[0, h] for h in heads]
        s_bf = [s.astype(BF16) for s in s_old]
        wq = [_dot(jnp.concatenate([w_ref[0, rs, hs(h)], qg_ref[0, rs, hs(h)]], axis=0), s_bf[h]) for h in heads]
        u_bf = [(uu_ref[0, rs, hs(h)].astype(F32) - wq[h][0:C]).astype(BF16) for h in heads]
        o = [wq[h][C:2 * C] + _dot(qkd_ref[0, rs, h * C:(h + 1) * C], u_bf[h]) for h in heads]
        for h in heads:
            s_ref[0, h] = egl[:, HEADS + h:HEADS + h + 1] * s_old[h] + _dot_tn(kd_ref[0, rs, hs(h)], u_bf[h])
        for h in heads:
            z = z_ref[0, rs, hs(h)]
            o_ref[0, rs, hs(h)] = (_rms(o[h], og_ref[...]) * (z * _sigmoid(z))).astype(o_ref.dtype)


def gdn_scan(w, uu, qg, kd, qkd, egl, proj, s0, o_gain, *, C, G):
    b, t, _ = w.shape
    r = C * G
    row_spec = pl.BlockSpec((1, r, D_MODEL), lambda i, c: (i, c, 0))
    state_spec = pl.BlockSpec((1, HEADS, HEAD_DIM, HEAD_DIM), lambda i, c: (i, 0, 0, 0))
    return pl.pallas_call(
        functools.partial(_gdn_scan_kernel, C=C, G=G),
        grid=(b, t // r),
        in_specs=[
            row_spec, row_spec, row_spec, row_spec,
            pl.BlockSpec((1, r, HEADS * C), lambda i, c: (i, c, 0)),
            pl.BlockSpec((1, G, 1, LANES), lambda i, c: (i, c, 0, 0)),
            pl.BlockSpec((1, r, D_MODEL), lambda i, c: (i, c, A_QKV // D_MODEL)),
            state_spec,
            pl.BlockSpec((1, HEAD_DIM), lambda i, c: (0, 0)),
        ],
        out_specs=[row_spec, state_spec],
        out_shape=[
            jax.ShapeDtypeStruct((b, t, D_MODEL), BF16),
            jax.ShapeDtypeStruct((b, HEADS, HEAD_DIM, HEAD_DIM), F32),
        ],
        compiler_params=_params("arbitrary", "arbitrary"),
        name="gdn_scan",
    )(w, uu, qg, kd, qkd, egl, proj, s0, o_gain.reshape(1, HEAD_DIM))


def _lambda(lam_ref, lam_init):
    lp = lam_ref[...]
    return (jnp.exp(jnp.sum(lp[0:1] * lp[1:2], axis=-1, keepdims=True))
            - jnp.exp(jnp.sum(lp[2:3] * lp[3:4], axis=-1, keepdims=True)) + lam_init)


DECODE_HEAD_GROUP = HEADS
KEY_COLS = 3


def _attn_prompt_kernel(qi_ref, kj_ref, slope_ref, lam_ref, sg_ref, kx_ref, q_ref, k_ref, vt_ref, o_ref,
                        s_even_ref, s_odd_ref, smax_even_ref, smax_odd_ref, diag_ref, m_ref, acc_ref,
                        *, tq, tk, col_block, n_pairs, lam_init):
    step = pl.program_id(1)
    slope = slope_ref[0, :, 0:1] * LOG2E
    lane = lax.broadcasted_iota(jnp.int32, (1, HEAD_DIM), 1)
    map_lanes = [lane < MAP_DIM, lane >= MAP_DIM]

    def score_stage(s_ref, smax_ref, mp, cols):
        q = q_ref[cols, :]
        k = k_ref[...]
        a0 = MAP_DIM if mp == 0 else 0
        extra = (lane >= a0) & (lane < a0 + KEY_COLS)
        qa = jnp.where(map_lanes[mp], q, jnp.where(extra, 1.0, 0.0).astype(BF16))
        ka = jnp.where(map_lanes[mp], k, kx_ref[0, mp])
        scores = _dot_nt(ka, qa)
        s_ref[mp, :, cols] = scores
        smax_ref[mp, :, cols] = jnp.max(scores, axis=0, keepdims=True)

    def mask_diagonal(s_ref, smax_ref, key_offset):
        facing = slice(key_offset, key_offset + tk)
        for mp in range(2):
            if key_offset:
                s_ref[mp, :, 0:key_offset] = jnp.full((tk, key_offset), NEG, F32)
                smax_ref[mp, :, 0:key_offset] = jnp.full((1, key_offset), NEG, F32)
            fixed = s_ref[mp, :, facing] + diag_ref[...]
            s_ref[mp, :, facing] = fixed
            smax_ref[mp, :, facing] = jnp.max(fixed, axis=0, keepdims=True)

    def build_diagonal_table():
        j = lax.broadcasted_iota(jnp.int32, (tk, 1), 0)
        i = lax.broadcasted_iota(jnp.int32, (1, tk), 1)
        allowed = (j >> CHUNK_SHIFT) <= (i >> CHUNK_SHIFT)
        fix = jnp.where(j > i, (2.0 * slope) * (i - j).astype(F32), 0.0)
        diag_ref[...] = jnp.where(allowed, fix, NEG)

    def softmax_stage(s_ref, smax_ref, shift_const, mp, cols):
        scores = s_ref[mp, :, cols]
        m_old = m_ref[mp, :, cols]
        m_new = jnp.maximum(m_old, smax_ref[mp, :, cols] + shift_const)
        probs = jnp.exp2(scores - (m_new - shift_const)).astype(BF16)
        m_ref[mp, :, cols] = m_new
        acc_ref[mp, :, cols] = jnp.exp2(m_old - m_new) * acc_ref[mp, :, cols] + _dot(vt_ref[...], probs)

    def finalize():
        lam = _lambda(lam_ref, lam_init)
        a0 = acc_ref[0]
        a1 = acc_ref[1]
        o_t = (a0[0:HEAD_DIM] / a0[HEAD_DIM:HEAD_DIM + 1]
               - lam * (a1[0:HEAD_DIM] / a1[HEAD_DIM:HEAD_DIM + 1]))
        o_t = o_t * (lax.rsqrt(jnp.mean(o_t * o_t, axis=0, keepdims=True) + EPS) * (1.0 - lam_init))
        o_ref[...] = (o_t.T * sg_ref[...]).astype(o_ref.dtype)

    units = [(mp, slice(c0, c0 + col_block)) for mp in range(2) for c0 in range(0, tq, col_block)]
    scored = jnp.minimum(step, n_pairs - 1)
    key_offset_s = kj_ref[scored] * tk - qi_ref[scored] * tq
    consumed = jnp.maximum(step - 1, 0)
    kj_c = kj_ref[consumed]
    key_offset_c = kj_c * tk - qi_ref[consumed] * tq

    @pl.when(step == 0)
    def _():
        build_diagonal_table()
        for mp, cols in units:
            score_stage(s_even_ref, smax_even_ref, mp, cols)

    @pl.when((step > 0) & (kj_c == 0))
    def _():
        m_ref[...] = jnp.full_like(m_ref, NEG)
        acc_ref[...] = jnp.zeros_like(acc_ref)

    tile_shift = slope * key_offset_c.astype(F32)
    even, odd = (s_even_ref, smax_even_ref), (s_odd_ref, smax_odd_ref)
    for parity, (written, read) in enumerate([(even, odd), (odd, even)]):
        @pl.when((step > 0) & (step % 2 == parity))
        def _():
            for mp, cols in units:
                score_stage(*written, mp, cols)
                softmax_stage(*read, tile_shift, mp, cols)

    for parity, written in enumerate([even, odd]):
        for key_offset in range(0, tq, tk):
            @pl.when((key_offset_s == key_offset) & (step % 2 == parity))
            def _():
                mask_diagonal(*written, key_offset)

    @pl.when((step > 0) & (key_offset_c == tq - tk))
    def _():
        finalize()


def alibi_slopes():
    return 2.0 ** (-8.0 * np.arange(1, HEADS + 1, dtype=np.float32) / HEADS)


def alibi_key_columns(t):
    term = (np.float32(LOG2E) * alibi_slopes())[:, None] * np.arange(t, dtype=np.float32)[None, :]
    cols = np.zeros((HEADS, 2, t, HEAD_DIM), np.float32)
    for c in range(KEY_COLS):
        piece = term.astype(BF16).astype(np.float32)
        term = term - piece
        for mp in range(2):
            a0 = MAP_DIM if mp == 0 else 0
            cols[:, mp, :, a0 + c] = piece
    return jnp.asarray(cols.astype(BF16))


def diff_attention_prompt(q, k, v_t, slopes, key_cols, lam_p, sub_gain, *, tq, tk, lam_init):
    t_all = q.shape[0]
    assert tq % tk == 0 and tk % CHUNK == 0 and t_all % tq == 0
    per_q = tq // tk
    qi = np.concatenate([np.full((i + 1) * per_q, i, np.int32) for i in range(t_all // tq)])
    kj = np.concatenate([np.arange((i + 1) * per_q, dtype=np.int32) for i in range(t_all // tq)])
    n_pairs = len(qi)

    def scored(p):
        return jnp.minimum(p, n_pairs - 1)

    def consumed(p):
        return jnp.maximum(p - 1, 0)

    grid_spec = pltpu.PrefetchScalarGridSpec(
        num_scalar_prefetch=2,
        grid=(HEADS, n_pairs + 1),
        in_specs=[
            pl.BlockSpec((1, 1, LANES), lambda h, p, qi_r, kj_r: (h, 0, 0)),
            pl.BlockSpec((4, MAP_DIM), lambda h, p, qi_r, kj_r: (0, 0)),
            pl.BlockSpec((1, HEAD_DIM), lambda h, p, qi_r, kj_r: (0, 0)),
            pl.BlockSpec((1, 2, tk, HEAD_DIM), lambda h, p, qi_r, kj_r: (h, 0, 0, 0)),
            pl.BlockSpec((tq, HEAD_DIM), lambda h, p, qi_r, kj_r: (qi_r[scored(p)], h)),
            pl.BlockSpec((tk, HEAD_DIM), lambda h, p, qi_r, kj_r: (kj_r[scored(p)], h)),
            pl.BlockSpec((HEAD_ROWS_T, tk), lambda h, p, qi_r, kj_r: (h, kj_r[consumed(p)])),
        ],
        out_specs=pl.BlockSpec((tq, HEAD_DIM), lambda h, p, qi_r, kj_r: (qi_r[consumed(p)], h)),
        scratch_shapes=[
            pltpu.VMEM((2, tk, tq), F32), pltpu.VMEM((2, tk, tq), F32),
            pltpu.VMEM((2, 1, tq), F32), pltpu.VMEM((2, 1, tq), F32),
            pltpu.VMEM((tk, tk), F32),
            pltpu.VMEM((2, 1, tq), F32), pltpu.VMEM((2, HEAD_ROWS_T, tq), F32),
        ],
    )
    return pl.pallas_call(
        functools.partial(_attn_prompt_kernel, tq=tq, tk=tk, col_block=tq, n_pairs=n_pairs, lam_init=lam_init),
        grid_spec=grid_spec,
        out_shape=jax.ShapeDtypeStruct((t_all, D_MODEL), BF16),
        compiler_params=_params("arbitrary", "arbitrary"),
        name="diff_attention_prompt",
    )(jnp.asarray(qi), jnp.asarray(kj), slopes, lam_p, sub_gain.reshape(1, HEAD_DIM), key_cols, q, k, v_t)


def _attn_decode_kernel(slope_ref, lam_ref, sg_ref, q_ref, k_ref, v_ref, kn_ref, vn_ref, o_ref, m_ref, l_ref, acc_ref,
                        *, tk, pos0, lam_init):
    j = pl.program_id(1)
    nk = pl.num_programs(1)
    n = q_ref.shape[1]

    @pl.when(j == 0)
    def _():
        m_ref[...] = jnp.full_like(m_ref, NEG)
        l_ref[...] = jnp.zeros_like(l_ref)
        acc_ref[...] = jnp.zeros_like(acc_ref)

    row = lax.broadcasted_iota(jnp.int32, (2 * n, 1), 0)
    q_pos = pos0 + jnp.where(row < n, row, row - n)
    lane = lax.broadcasted_iota(jnp.int32, (1, HEAD_DIM), 1)
    map_of_row_has_lane = (row < n) == (lane < MAP_DIM)

    def update(keys, values, k0, n_keys):
        k_pos = k0 + lax.broadcasted_iota(jnp.int32, (1, n_keys), 1)
        allowed = (k_pos >> CHUNK_SHIFT) <= (q_pos >> CHUNK_SHIFT)
        dist = jnp.abs(q_pos - k_pos).astype(F32)
        for h0 in range(0, HEADS, DECODE_HEAD_GROUP):
            heads = range(h0, h0 + DECODE_HEAD_GROUP)
            scores, probs, alphas = {}, {}, {}
            for h in heads:
                slope = slope_ref[h, :, 0:1] * LOG2E
                q = q_ref[0, :, h * HEAD_DIM:(h + 1) * HEAD_DIM]
                q2 = jnp.concatenate([q, q], axis=0)
                q_stack = jnp.where(map_of_row_has_lane, q2, jnp.zeros_like(q2))
                scores[h] = jnp.where(allowed, _dot_nt(q_stack, keys(h).astype(BF16)) - slope * dist, NEG)
            for h in heads:
                m_old = m_ref[h]
                m_new = jnp.maximum(m_old, jnp.max(scores[h], axis=-1, keepdims=True))
                alphas[h] = jnp.exp2(m_old - m_new)
                p = jnp.exp2(scores[h] - m_new)
                l_ref[h] = alphas[h] * l_ref[h] + jnp.sum(p, axis=-1, keepdims=True)
                probs[h] = p.astype(BF16)
                m_ref[h] = m_new
            pv = {h: _dot(probs[h], values(h).astype(BF16)) for h in heads}
            for h in heads:
                acc_ref[h] = alphas[h] * acc_ref[h] + pv[h]

    def head_rows(h):
        return pl.ds(h, tk, stride=HEADS)

    update(lambda h: k_ref[0, head_rows(h), :], lambda h: v_ref[0, head_rows(h), :], j * tk, tk)

    @pl.when(j == nk - 1)
    def _():
        update(lambda h: kn_ref[0, :, h * HEAD_DIM:(h + 1) * HEAD_DIM],
               lambda h: vn_ref[0, :, h * HEAD_DIM:(h + 1) * HEAD_DIM], pos0, n)
        lam = _lambda(lam_ref, lam_init)
        for h in range(HEADS):
            o = acc_ref[h] / l_ref[h]
            o = o[0:n] - lam * o[n:2 * n]
            o_ref[0, :, h * HEAD_DIM:(h + 1) * HEAD_DIM] = (_rms(o, sg_ref[...]) * (1.0 - lam_init)).astype(o_ref.dtype)


def diff_attention_decode(q, k, v, k_new, v_new, slopes, lam_p, sub_gain, *, tk, lam_init):
    b, n, _ = q.shape
    t_k = k.shape[1]
    assert t_k % tk == 0 and t_k % CHUNK == 0
    k = k.reshape(b, t_k * HEADS, HEAD_DIM)
    v = v.reshape(b, t_k * HEADS, HEAD_DIM)
    row_block = pl.BlockSpec((1, n, D_MODEL), lambda bi, j: (bi, 0, 0))
    kv_spec = pl.BlockSpec((1, tk * HEADS, HEAD_DIM), lambda bi, j: (bi, j, 0))
    return pl.pallas_call(
        functools.partial(_attn_decode_kernel, tk=tk, pos0=t_k, lam_init=lam_init),
        grid=(b, t_k // tk),
        in_specs=[
            pl.BlockSpec((HEADS, 1, LANES), lambda bi, j: (0, 0, 0)),
            pl.BlockSpec((4, MAP_DIM), lambda bi, j: (0, 0)),
            pl.BlockSpec((1, HEAD_DIM), lambda bi, j: (0, 0)),
            row_block, kv_spec, kv_spec, row_block, row_block,
        ],
        out_specs=row_block,
        out_shape=jax.ShapeDtypeStruct((b, n, D_MODEL), BF16),
        scratch_shapes=[
            pltpu.VMEM((HEADS, 2 * n, 1), F32),
            pltpu.VMEM((HEADS, 2 * n, 1), F32),
            pltpu.VMEM((HEADS, 2 * n, HEAD_DIM), F32),
        ],
        compiler_params=_params("arbitrary", "arbitrary"),
        name="diff_attention_decode",
    )(slopes, lam_p, sub_gain.reshape(1, HEAD_DIM), q, k, v, k_new, v_new)


class Tiles(NamedTuple):
    wide_rows: int
    rows: int
    mlp_hidden: int
    gdn_chunk: int
    gdn_group: int
    attn_q: int
    attn_k: int


def _tiles(batch, seq):
    rows = batch * seq
    if seq > CHUNK:
        return Tiles(wide_rows=512, rows=1024, mlp_hidden=2048, gdn_chunk=CHUNK, gdn_group=4, attn_q=1024, attn_k=512)
    return Tiles(wide_rows=rows, rows=rows, mlp_hidden=2048, gdn_chunk=seq, gdn_group=1, attn_q=seq, attn_k=512)


def _trunk(x, conv_state, ssm_state, past_k, past_v, wts):
    b, t, _ = x.shape
    m = b * t
    tiles = _tiles(b, t)
    xf = x.reshape(m, D_MODEL)
    new_conv, new_ssm = [], []
    k_new = v_new = k_bf = v_t = None
    for l in range(DEPTH):
        gains = wts["norm_gains"][l]
        if l < N_A:
            proj = norm_matmul(xf, gains[0], wts["a_w_in"], layer=l, tm=tiles.wide_rows,
                               tn=A_IN_PAD).reshape(b, t, A_IN_PAD)
            w, uu, qg, kd, qkd, egl, conv_l = gdn_prep(proj, conv_state[l], wts["a_conv_w"][l], wts["gate_par"][l],
                                                       C=tiles.gdn_chunk, G=tiles.gdn_group)
            og, s_l = gdn_scan(w, uu, qg, kd, qkd, egl, proj, ssm_state[l], wts["a_o_gain"][l],
                               C=tiles.gdn_chunk, G=tiles.gdn_group)
            new_conv.append(conv_l)
            new_ssm.append(s_l)
            xf = matmul_norm_res(og.reshape(m, D_MODEL), wts["a_w_out"][l], gains[1], xf, tm=tiles.rows)
        else:
            j = l - N_A
            prompt = past_k is None
            if j == 0:
                kv = dict(tm=tiles.rows, tn=D_MODEL, head_major_first=prompt)
                k_new, k_bf = norm_matmul(xf, wts["kv_gain"], wts["w_kv"], columns=(0, D_MODEL),
                                          out_dtypes=(F32, BF16), **kv)
                v_new = norm_matmul(xf, wts["kv_gain"], wts["w_kv"], columns=(D_MODEL, D_MODEL),
                                    head_transposed_copy=prompt, **kv)
                if prompt:
                    v_new, v_t = v_new
            q = norm_matmul(xf, gains[0], wts["b_w_q"][j], tm=tiles.rows, tn=D_MODEL, out_dtypes=(BF16,), out_scale=Q_SCALE)
            lam_init = 0.8 - 0.6 * math.exp(-0.3 * l)
            tail = (wts["b_lam"][j], wts["b_sub_gain"][j])
            if prompt:
                assert b == 1
                o = diff_attention_prompt(q, k_bf, v_t, wts["slopes"], alibi_key_columns(tiles.attn_k),
                                          *tail, tq=tiles.attn_q, tk=tiles.attn_k, lam_init=lam_init)
            else:
                o = diff_attention_decode(q.reshape(b, t, D_MODEL), past_k, past_v, k_new.reshape(b, t, D_MODEL),
                                          v_new.reshape(b, t, D_MODEL), wts["slopes"], *tail, tk=tiles.attn_k,
                                          lam_init=lam_init)
            xf = matmul_norm_res(o.reshape(m, D_MODEL), wts["b_w_out"][j], gains[1], xf, tm=tiles.rows)
        xf = mlp_block(xf, gains[2], wts["mlp_w1"], wts["mlp_w2"], gains[3], layer=l,
                       tm=tiles.wide_rows, tf=tiles.mlp_hidden)
    kv_shape = (b, t, HEADS, HEAD_DIM)
    return (xf.reshape(b, t, D_MODEL), jnp.stack(new_conv), jnp.stack(new_ssm),
            k_new.reshape(kv_shape), v_new.reshape(kv_shape))


def kernel(x_prompt, x_sample, state_conv, state_ssm, cache_k, cache_v, norm_gains, a_w_in, a_conv_w, a_log,
           a_dt_bias, a_o_gain, a_w_out, kv_gain, w_kv, b_w_q, b_lam, b_sub_gain, b_w_out, mlp_w1, mlp_w2):
    a_in = a_w_in.shape[-1]
    gate_par = jnp.zeros((N_A, 2, LANES), F32)
    gate_par = gate_par.at[:, 0, HEADS:2 * HEADS].set(a_log.astype(F32))
    gate_par = gate_par.at[:, 1, HEADS:2 * HEADS].set(a_dt_bias.astype(F32))
    slopes = jnp.asarray(alibi_slopes())
    wts = dict(
        norm_gains=norm_gains.astype(F32),
        a_w_in=jnp.pad(a_w_in, ((0, 0), (0, 0), (0, A_IN_PAD - a_in))).astype(BF16),
        a_conv_w=a_conv_w.astype(F32),
        gate_par=gate_par,
        a_o_gain=a_o_gain.astype(F32),
        a_w_out=a_w_out.astype(BF16),
        kv_gain=kv_gain.astype(F32),
        w_kv=w_kv.astype(BF16),
        b_w_q=b_w_q.astype(BF16),
        b_lam=b_lam.astype(F32),
        b_sub_gain=b_sub_gain.astype(F32),
        b_w_out=b_w_out.astype(BF16),
        mlp_w1=mlp_w1.astype(BF16),
        mlp_w2=mlp_w2.astype(BF16),
        slopes=jnp.broadcast_to(slopes[:, None, None], (HEADS, 1, LANES)),
    )
    bp = x_prompt.shape[0]
    conv0 = jnp.zeros((N_A, bp, CONV_W - 1, A_QKV), F32)
    ssm0 = jnp.zeros((N_A, bp, HEADS, HEAD_DIM, HEAD_DIM), F32)
    y_p, p_conv, p_ssm, p_k, p_v = _trunk(x_prompt, conv0, ssm0, None, None, wts)
    y_s, s_conv, s_ssm, s_k, s_v = _trunk(x_sample, state_conv, state_ssm, cache_k, cache_v, wts)
    return (y_p, y_s, p_conv, p_ssm, p_k, p_v, s_conv, s_ssm, s_k, s_v)
```

```python
import functools
import math
from typing import NamedTuple

import jax
import jax.numpy as jnp
import numpy as np
from jax import lax
from jax.experimental import pallas as pl
from jax.experimental.pallas import tpu as pltpu

F32 = jnp.float32
BF16 = jnp.bfloat16

D_MODEL = 1024
DEPTH = 4
N_A = DEPTH // 2
CHUNK = 64
CHUNK_SHIFT = CHUNK.bit_length() - 1
assert 1 << CHUNK_SHIFT == CHUNK
HEADS = 8
HEAD_DIM = D_MODEL // HEADS
CONV_W = 4
A_QKV = 3 * D_MODEL
A_GATE_COL = A_QKV + D_MODEL
LANES = 128
A_IN_PAD = A_GATE_COL + LANES
MAP_DIM = HEAD_DIM // 2
EPS = 1e-6
NEG = -1e30
VMEM_LIMIT = 48 * 1024 * 1024
HIGHEST = lax.Precision.HIGHEST
ONES_ROWS = 16
HEAD_ROWS_T = HEAD_DIM + ONES_ROWS
LOG2E = math.log2(math.e)
Q_SCALE = MAP_DIM ** -0.5 * LOG2E


def _dot(a, b, precision=None):
    return jnp.dot(a, b, preferred_element_type=F32, precision=precision)


def _dot_nt(a, b):
    return lax.dot_general(a, b, (((1,), (1,)), ((), ())), preferred_element_type=F32)


def _dot_tn(a, b):
    return lax.dot_general(a, b, (((0,), (0,)), ((), ())), preferred_element_type=F32)


def _split(a):
    hi = a.astype(BF16)
    return hi, (a - hi.astype(F32)).astype(BF16)


def _dot_split(a, b):
    return _dot(a[0], b[0]) + (_dot(a[0], b[1]) + _dot(a[1], b[0]))


def _rms(x, gain):
    return x * lax.rsqrt(jnp.mean(x * x, axis=-1, keepdims=True) + EPS) * gain


def _sigmoid(x):
    return 1.0 / (1.0 + jnp.exp(-x))


def _params(*semantics):
    return pltpu.CompilerParams(dimension_semantics=semantics, vmem_limit_bytes=VMEM_LIMIT)


def _norm_matmul_kernel(x_ref, g_ref, w_ref, *out_and_scratch, out_scale, head_transposed_copy, head_major_first):
    *o_refs, hn_ref = out_and_scratch

    @pl.when(pl.program_id(1) == 0)
    def _():
        hn_ref[...] = _rms(x_ref[...], g_ref[...]).astype(BF16)

    y = _dot(hn_ref[...], w_ref[...])
    if out_scale != 1.0:
        y = y * out_scale
    if head_transposed_copy:
        *o_refs, ot_ref = o_refs
        y_t = y.T.astype(ot_ref.dtype)
        for h in range(y_t.shape[0] // HEAD_DIM):
            r0 = h * HEAD_ROWS_T
            ot_ref[r0:r0 + HEAD_DIM, :] = y_t[h * HEAD_DIM:(h + 1) * HEAD_DIM]
            ot_ref[r0 + HEAD_DIM:r0 + HEAD_ROWS_T, :] = jnp.ones((ONES_ROWS, y_t.shape[1]), ot_ref.dtype)
    if head_major_first:
        o_ref, *o_refs = o_refs
        rows = y.shape[0]
        for h in range(y.shape[1] // HEAD_DIM):
            o_ref[pl.ds(h, rows, stride=y.shape[1] // HEAD_DIM), :] = (
                y[:, h * HEAD_DIM:(h + 1) * HEAD_DIM].astype(o_ref.dtype))
    for o_ref in o_refs:
        o_ref[...] = y.astype(o_ref.dtype)


def norm_matmul(x, gain, w, *, tm, tn, out_dtypes=(F32,), out_scale=1.0, head_transposed_copy=False, columns=None,
                layer=None, head_major_first=False):
    m, k = x.shape
    first, n = columns or (0, w.shape[-1])
    first_tile = first // tn
    if layer is None:
        w_spec = pl.BlockSpec((k, tn), lambda i, j: (0, first_tile + j))
    else:
        w_spec = pl.BlockSpec((None, k, tn), lambda i, j: (layer, 0, first_tile + j))
    out_specs = [pl.BlockSpec((tm, tn), lambda i, j: (i, j)) for _ in out_dtypes]
    out_shape = [jax.ShapeDtypeStruct((m, n), dt) for dt in out_dtypes]
    if head_major_first:
        assert tn == n
        heads = n // HEAD_DIM
        out_specs[0] = pl.BlockSpec((tm * heads, HEAD_DIM), lambda i, j: (i, 0))
        out_shape[0] = jax.ShapeDtypeStruct((m * heads, HEAD_DIM), out_dtypes[0])
    if head_transposed_copy:
        out_specs.append(pl.BlockSpec((tn // HEAD_DIM * HEAD_ROWS_T, tm), lambda i, j: (j, i)))
        out_shape.append(jax.ShapeDtypeStruct((n // HEAD_DIM * HEAD_ROWS_T, m), BF16))
    outs = pl.pallas_call(
        functools.partial(_norm_matmul_kernel, out_scale=out_scale, head_transposed_copy=head_transposed_copy,
                          head_major_first=head_major_first),
        grid=(m // tm, n // tn),
        in_specs=[
            pl.BlockSpec((tm, k), lambda i, j: (i, 0)),
            pl.BlockSpec((1, k), lambda i, j: (0, 0)),
            w_spec,
        ],
        out_specs=out_specs,
        out_shape=out_shape,
        scratch_shapes=[pltpu.VMEM((tm, k), BF16)],
        compiler_params=_params("parallel", "arbitrary"),
        name="norm_matmul",
    )(x, gain.reshape(1, k), w)
    return outs[0] if len(outs) == 1 else outs


def _matmul_norm_res_kernel(a_ref, w_ref, g_ref, x_ref, o_ref):
    y = _dot(a_ref[...], w_ref[...])
    o_ref[...] = x_ref[...] + _rms(y, g_ref[...])


def matmul_norm_res(a, w, gain, x, *, tm):
    m, k = a.shape
    d = w.shape[1]
    return pl.pallas_call(
        _matmul_norm_res_kernel,
        grid=(m // tm,),
        in_specs=[
            pl.BlockSpec((tm, k), lambda i: (i, 0)),
            pl.BlockSpec((k, d), lambda i: (0, 0)),
            pl.BlockSpec((1, d), lambda i: (0, 0)),
            pl.BlockSpec((tm, d), lambda i: (i, 0)),
        ],
        out_specs=pl.BlockSpec((tm, d), lambda i: (i, 0)),
        out_shape=jax.ShapeDtypeStruct((m, d), F32),
        compiler_params=_params("parallel"),
        name="matmul_norm_res",
    )(a, w, gain.reshape(1, d), x)


def _mlp_kernel(x_ref, g_in_ref, w1_ref, w2_ref, g_out_ref, o_ref, hn_ref, acc_ref):
    f = pl.program_id(1)

    @pl.when(f == 0)
    def _():
        hn_ref[...] = _rms(x_ref[...], g_in_ref[...]).astype(BF16)
        acc_ref[...] = jnp.zeros_like(acc_ref)

    h = _dot(hn_ref[...], w1_ref[...])
    h = jnp.square(jnp.maximum(h, 0.0)).astype(BF16)
    acc_ref[...] += _dot(h, w2_ref[...])

    @pl.when(f == pl.num_programs(1) - 1)
    def _():
        o_ref[...] = x_ref[...] + _rms(acc_ref[...], g_out_ref[...])


def mlp_block(x, g_in, w1, w2, g_out, *, layer, tm, tf):
    m, d = x.shape
    ff = w1.shape[2]
    return pl.pallas_call(
        _mlp_kernel,
        grid=(m // tm, ff // tf),
        in_specs=[
            pl.BlockSpec((tm, d), lambda i, f: (i, 0)),
            pl.BlockSpec((1, d), lambda i, f: (0, 0)),
            pl.BlockSpec((None, d, tf), lambda i, f: (layer, 0, f)),
            pl.BlockSpec((None, tf, d), lambda i, f: (layer, f, 0)),
            pl.BlockSpec((1, d), lambda i, f: (0, 0)),
        ],
        out_specs=pl.BlockSpec((tm, d), lambda i, f: (i, 0)),
        out_shape=jax.ShapeDtypeStruct((m, d), F32),
        scratch_shapes=[pltpu.VMEM((tm, d), BF16), pltpu.VMEM((tm, d), F32)],
        compiler_params=_params("parallel", "arbitrary"),
        name="mlp_block",
    )(x, g_in.reshape(1, d), w1, w2, g_out.reshape(1, d))


CONV_ROW0 = 8
UNIT_GROUP = 16


def _gdn_prep_kernel(proj_ref, conv0_ref, cw_ref, gp_ref,
                     w_ref, uu_ref, qg_ref, kd_ref, qkd_ref, egl_ref, conv_ref, xbuf_ref, *, C, G):
    R = C * G
    prev0 = CONV_ROW0 - (CONV_W - 1)

    @pl.when(pl.program_id(1) == 0)
    def _():
        xbuf_ref[prev0:CONV_ROW0, :] = conv0_ref[0]

    xbuf_ref[CONV_ROW0:CONV_ROW0 + R, :] = proj_ref[0, :, 0:A_QKV]
    cw = cw_ref[...]
    y = xbuf_ref[prev0:prev0 + R, :] * cw[0:1, :]
    for j in range(1, CONV_W):
        y = y + xbuf_ref[prev0 + j:prev0 + j + R, :] * cw[j:j + 1, :]
    y = y * _sigmoid(y)
    last = xbuf_ref[prev0 + R:CONV_ROW0 + R, :]
    xbuf_ref[prev0:CONV_ROW0, :] = last
    conv_ref[0] = last

    tail = proj_ref[0, :, A_GATE_COL:A_IN_PAD]
    gp = gp_ref[...]
    beta = _sigmoid(tail)
    t = tail + gp[1:2, :]
    softplus = jnp.maximum(t, 0.0) + jnp.log(1.0 + jnp.exp(-jnp.abs(t)))
    g = -jnp.exp(gp[0:1, :]) * softplus

    rr = lax.broadcasted_iota(jnp.int32, (R, R), 0)
    cc = lax.broadcasted_iota(jnp.int32, (R, R), 1)
    shift = C.bit_length() - 1
    same = (rr >> shift) == (cc >> shift)
    gc = _dot(jnp.where(same & (rr >= cc), 1.0, 0.0), g, HIGHEST)
    gl = _dot(jnp.where(same, 1.0, 0.0), g, HIGHEST)
    pad_rows = -R % LANES
    gc_t = (jnp.concatenate([gc, jnp.zeros((pad_rows, LANES), F32)], axis=0) if pad_rows else gc).T
    eg = jnp.exp(gc)
    ekd = jnp.exp(gl - gc)
    egl = jnp.exp(gl)
    for gi in range(G):
        egl_ref[0, gi] = egl[gi * C:gi * C + 1, :]

    row = lax.broadcasted_iota(jnp.int32, (C, C), 0)
    col = lax.broadcasted_iota(jnp.int32, (C, C), 1)
    incl = row >= col
    strict = row > col
    eye = (row == col).astype(F32)

    units = [(gi, h) for gi in range(G) for h in range(HEADS)]
    qn, kn = [], []
    for h in range(HEADS):
        q = y[:, h * HEAD_DIM:(h + 1) * HEAD_DIM]
        k = y[:, D_MODEL + h * HEAD_DIM:D_MODEL + (h + 1) * HEAD_DIM]
        qn.append(q * lax.rsqrt(jnp.sum(q * q, axis=-1, keepdims=True) + EPS) * (HEAD_DIM ** -0.5))
        kn.append(k * lax.rsqrt(jnp.sum(k * k, axis=-1, keepdims=True) + EPS))

    def rows(gi):
        return slice(gi * C, (gi + 1) * C)

    def process(units):
        decay, qk, a_mat = {}, {}, {}
        for gi, h in units:
            rs, gl_h = rows(gi), HEADS + h
            g_col = gc[rs, gl_h:gl_h + 1]
            g_row = gc_t[gl_h:gl_h + 1, gi * C:(gi + 1) * C]
            decay[gi, h] = jnp.where(incl, jnp.exp(jnp.minimum(g_col - g_row, 0.0)), 0.0)
            k_bf = kn[h][rs].astype(BF16)
            qk_kk = _dot_nt(jnp.concatenate([qn[h][rs].astype(BF16), k_bf], axis=0), k_bf)
            qk[gi, h] = qk_kk[0:C]
            a_mat[gi, h] = jnp.where(strict, beta[rs, h:h + 1] * qk_kk[C:2 * C] * decay[gi, h], 0.0)
        t_inv = {u: eye - a_mat[u] for u in units}
        pw = {}
        for u in units:
            a_s = _split(a_mat[u])
            pw[u] = _dot_split(a_s, a_s)
        n_double = C.bit_length() - 2
        for step in range(n_double):
            for u in units:
                p_s = _split(pw[u])
                t_s = _split(t_inv[u])
                if step + 1 < n_double:
                    tp = _dot_split(tuple(jnp.concatenate([t_part, p_part], axis=0)
                                          for t_part, p_part in zip(t_s, p_s)), p_s)
                    t_inv[u] = t_inv[u] + tp[0:C]
                    pw[u] = tp[C:2 * C]
                else:
                    t_inv[u] = t_inv[u] + _dot_split(t_s, p_s)
        for gi, h in units:
            rs, gl_h = rows(gi), HEADS + h
            hs = slice(h * HEAD_DIM, (h + 1) * HEAD_DIM)
            b_col = beta[rs, h:h + 1]
            eg_col = eg[rs, gl_h:gl_h + 1]
            k = kn[h][rs]
            v = y[rs, 2 * D_MODEL + h * HEAD_DIM:2 * D_MODEL + (h + 1) * HEAD_DIM]
            rhs = jnp.concatenate([b_col * v, (b_col * eg_col) * k], axis=-1).astype(BF16)
            sol = _dot(t_inv[gi, h].astype(BF16), rhs)
            uu_ref[0, rs, hs] = sol[:, 0:HEAD_DIM].astype(uu_ref.dtype)
            w_ref[0, rs, hs] = sol[:, HEAD_DIM:].astype(w_ref.dtype)
            qg_ref[0, rs, hs] = (eg_col * qn[h][rs]).astype(qg_ref.dtype)
            kd_ref[0, rs, hs] = (ekd[rs, gl_h:gl_h + 1] * k).astype(kd_ref.dtype)
            qkd_ref[0, rs, h * C:(h + 1) * C] = (qk[gi, h] * decay[gi, h]).astype(qkd_ref.dtype)

    for g0 in range(0, len(units), UNIT_GROUP):
        process(units[g0:g0 + UNIT_GROUP])


def gdn_prep(proj, conv0, conv_w, gate_par, *, C, G):
    b, t, _ = proj.shape
    r = C * G
    nc = t // C
    row_spec = pl.BlockSpec((1, r, D_MODEL), lambda i, c: (i, c, 0))
    conv_spec = pl.BlockSpec((1, CONV_W - 1, A_QKV), lambda i, c: (i, 0, 0))
    return pl.pallas_call(
        functools.partial(_gdn_prep_kernel, C=C, G=G),
        grid=(b, t // r),
        in_specs=[
            pl.BlockSpec((1, r, A_IN_PAD), lambda i, c: (i, c, 0)),
            conv_spec,
            pl.BlockSpec((CONV_W, A_QKV), lambda i, c: (0, 0)),
            pl.BlockSpec((2, LANES), lambda i, c: (0, 0)),
        ],
        out_specs=[
            row_spec, row_spec, row_spec, row_spec,
            pl.BlockSpec((1, r, HEADS * C), lambda i, c: (i, c, 0)),
            pl.BlockSpec((1, G, 1, LANES), lambda i, c: (i, c, 0, 0)),
            conv_spec,
        ],
        out_shape=[jax.ShapeDtypeStruct((b, t, D_MODEL), BF16)] * 4 + [
            jax.ShapeDtypeStruct((b, t, HEADS * C), BF16),
            jax.ShapeDtypeStruct((b, nc, 1, LANES), F32),
            jax.ShapeDtypeStruct((b, CONV_W - 1, A_QKV), F32),
        ],
        scratch_shapes=[pltpu.VMEM((CONV_ROW0 + r, A_QKV), F32)],
        compiler_params=_params("arbitrary", "arbitrary"),
        name="gdn_prep",
    )(proj, conv0, conv_w, gate_par)


def _gdn_scan_kernel(w_ref, uu_ref, qg_ref, kd_ref, qkd_ref, egl_ref, z_ref, s0_ref, og_ref, o_ref, s_ref, *, C, G):
    @pl.when(pl.program_id(1) == 0)
    def _():
        s_ref[0] = s0_ref[0]

    heads = range(HEADS)
    for gi in range(G):
        rs = slice(gi * C, (gi + 1) * C)
        egl = egl_ref[0, gi]

        def hs(h):
            return slice(h * HEAD_DIM, (h + 1) * HEAD_DIM)

        s_old = [s_ref[0, h] for h in heads]
        s_bf = [s.astype(BF16) for s in s_old]
        wq = [_dot(jnp.concatenate([w_ref[0, rs, hs(h)], qg_ref[0, rs, hs(h)]], axis=0), s_bf[h]) for h in heads]
        u_bf = [(uu_ref[0, rs, hs(h)].astype(F32) - wq[h][0:C]).astype(BF16) for h in heads]
        o = [wq[h][C:2 * C] + _dot(qkd_ref[0, rs, h * C:(h + 1) * C], u_bf[h]) for h in heads]
        for h in heads:
            s_ref[0, h] = egl[:, HEADS + h:HEADS + h + 1] * s_old[h] + _dot_tn(kd_ref[0, rs, hs(h)], u_bf[h])
        for h in heads:
            z = z_ref[0, rs, hs(h)]
            o_ref[0, rs, hs(h)] = (_rms(o[h], og_ref[...]) * (z * _sigmoid(z))).astype(o_ref.dtype)


def gdn_scan(w, uu, qg, kd, qkd, egl, proj, s0, o_gain, *, C, G):
    b, t, _ = w.shape
    r = C * G
    row_spec = pl.BlockSpec((1, r, D_MODEL), lambda i, c: (i, c, 0))
    state_spec = pl.BlockSpec((1, HEADS, HEAD_DIM, HEAD_DIM), lambda i, c: (i, 0, 0, 0))
    return pl.pallas_call(
        functools.partial(_gdn_scan_kernel, C=C, G=G),
        grid=(b, t // r),
        in_specs=[
            row_spec, row_spec, row_spec, row_spec,
            pl.BlockSpec((1, r, HEADS * C), lambda i, c: (i, c, 0)),
            pl.BlockSpec((1, G, 1, LANES), lambda i, c: (i, c, 0, 0)),
            pl.BlockSpec((1, r, D_MODEL), lambda i, c: (i, c, A_QKV // D_MODEL)),
            state_spec,
            pl.BlockSpec((1, HEAD_DIM), lambda i, c: (0, 0)),
        ],
        out_specs=[row_spec, state_spec],
        out_shape=[
            jax.ShapeDtypeStruct((b, t, D_MODEL), BF16),
            jax.ShapeDtypeStruct((b, HEADS, HEAD_DIM, HEAD_DIM), F32),
        ],
        compiler_params=_params("arbitrary", "arbitrary"),
        name="gdn_scan",
    )(w, uu, qg, kd, qkd, egl, proj, s0, o_gain.reshape(1, HEAD_DIM))


def _lambda(lam_ref, lam_init):
    lp = lam_ref[...]
    return (jnp.exp(jnp.sum(lp[0:1] * lp[1:2], axis=-1, keepdims=True))
            - jnp.exp(jnp.sum(lp[2:3] * lp[3:4], axis=-1, keepdims=True)) + lam_init)


DECODE_HEAD_GROUP = HEADS
KEY_COLS = 3


def _attn_prompt_kernel(qi_ref, kj_ref, slope_ref, lam_ref, sg_ref, kx_ref, q_ref, k_ref, vt_ref, o_ref,
                        s_even_ref, s_odd_ref, smax_even_ref, smax_odd_ref, diag_ref, m_ref, acc_ref,
                        *, tq, tk, col_block, n_pairs, lam_init):
    step = pl.program_id(1)
    slope = slope_ref[0, :, 0:1] * LOG2E
    lane = lax.broadcasted_iota(jnp.int32, (1, HEAD_DIM), 1)
    map_lanes = [lane < MAP_DIM, lane >= MAP_DIM]

    def score_stage(s_ref, smax_ref, mp, cols):
        q = q_ref[cols, :]
        k = k_ref[...]
        a0 = MAP_DIM if mp == 0 else 0
        extra = (lane >= a0) & (lane < a0 + KEY_COLS)
        qa = jnp.where(map_lanes[mp], q, jnp.where(extra, 1.0, 0.0).astype(BF16))
        ka = jnp.where(map_lanes[mp], k, kx_ref[0, mp])
        scores = _dot_nt(ka, qa)
        s_ref[mp, :, cols] = scores
        smax_ref[mp, :, cols] = jnp.max(scores, axis=0, keepdims=True)

    def mask_diagonal(s_ref, smax_ref, key_offset):
        facing = slice(key_offset, key_offset + tk)
        for mp in range(2):
            if key_offset:
                s_ref[mp, :, 0:key_offset] = jnp.full((tk, key_offset), NEG, F32)
                smax_ref[mp, :, 0:key_offset] = jnp.full((1, key_offset), NEG, F32)
            fixed = s_ref[mp, :, facing] + diag_ref[...]
            s_ref[mp, :, facing] = fixed
            smax_ref[mp, :, facing] = jnp.max(fixed, axis=0, keepdims=True)

    def build_diagonal_table():
        j = lax.broadcasted_iota(jnp.int32, (tk, 1), 0)
        i = lax.broadcasted_iota(jnp.int32, (1, tk), 1)
        allowed = (j >> CHUNK_SHIFT) <= (i >> CHUNK_SHIFT)
        fix = jnp.where(j > i, (2.0 * slope) * (i - j).astype(F32), 0.0)
        diag_ref[...] = jnp.where(allowed, fix, NEG)

    def softmax_stage(s_ref, smax_ref, shift_const, mp, cols):
        scores = s_ref[mp, :, cols]
        m_old = m_ref[mp, :, cols]
        m_new = jnp.maximum(m_old, smax_ref[mp, :, cols] + shift_const)
        probs = jnp.exp2(scores - (m_new - shift_const)).astype(BF16)
        m_ref[mp, :, cols] = m_new
        acc_ref[mp, :, cols] = jnp.exp2(m_old - m_new) * acc_ref[mp, :, cols] + _dot(vt_ref[...], probs)

    def finalize():
        lam = _lambda(lam_ref, lam_init)
        a0 = acc_ref[0]
        a1 = acc_ref[1]
        o_t = (a0[0:HEAD_DIM] / a0[HEAD_DIM:HEAD_DIM + 1]
               - lam * (a1[0:HEAD_DIM] / a1[HEAD_DIM:HEAD_DIM + 1]))
        o_t = o_t * (lax.rsqrt(jnp.mean(o_t * o_t, axis=0, keepdims=True) + EPS) * (1.0 - lam_init))
        o_ref[...] = (o_t.T * sg_ref[...]).astype(o_ref.dtype)

    units = [(mp, slice(c0, c0 + col_block)) for mp in range(2) for c0 in range(0, tq, col_block)]
    scored = jnp.minimum(step, n_pairs - 1)
    key_offset_s = kj_ref[scored] * tk - qi_ref[scored] * tq
    consumed = jnp.maximum(step - 1, 0)
    kj_c = kj_ref[consumed]
    key_offset_c = kj_c * tk - qi_ref[consumed] * tq

    @pl.when(step == 0)
    def _():
        build_diagonal_table()
        for mp, cols in units:
            score_stage(s_even_ref, smax_even_ref, mp, cols)

    @pl.when((step > 0) & (kj_c == 0))
    def _():
        m_ref[...] = jnp.full_like(m_ref, NEG)
        acc_ref[...] = jnp.zeros_like(acc_ref)

    tile_shift = slope * key_offset_c.astype(F32)
    even, odd = (s_even_ref, smax_even_ref), (s_odd_ref, smax_odd_ref)
    for parity, (written, read) in enumerate([(even, odd), (odd, even)]):
        @pl.when((step > 0) & (step % 2 == parity))
        def _():
            for mp, cols in units:
                score_stage(*written, mp, cols)
                softmax_stage(*read, tile_shift, mp, cols)

    for parity, written in enumerate([even, odd]):
        for key_offset in range(0, tq, tk):
            @pl.when((key_offset_s == key_offset) & (step % 2 == parity))
            def _():
                mask_diagonal(*written, key_offset)

    @pl.when((step > 0) & (key_offset_c == tq - tk))
    def _():
        finalize()


def alibi_slopes():
    return 2.0 ** (-8.0 * np.arange(1, HEADS + 1, dtype=np.float32) / HEADS)


def alibi_key_columns(t):
    term = (np.float32(LOG2E) * alibi_slopes())[:, None] * np.arange(t, dtype=np.float32)[None, :]
    cols = np.zeros((HEADS, 2, t, HEAD_DIM), np.float32)
    for c in range(KEY_COLS):
        piece = term.astype(BF16).astype(np.float32)
        term = term - piece
        for mp in range(2):
            a0 = MAP_DIM if mp == 0 else 0
            cols[:, mp, :, a0 + c] = piece
    return jnp.asarray(cols.astype(BF16))


def diff_attention_prompt(q, k, v_t, slopes, key_cols, lam_p, sub_gain, *, tq, tk, lam_init):
    t_all = q.shape[0]
    assert tq % tk == 0 and tk % CHUNK == 0 and t_all % tq == 0
    per_q = tq // tk
    qi = np.concatenate([np.full((i + 1) * per_q, i, np.int32) for i in range(t_all // tq)])
    kj = np.concatenate([np.arange((i + 1) * per_q, dtype=np.int32) for i in range(t_all // tq)])
    n_pairs = len(qi)

    def scored(p):
        return jnp.minimum(p, n_pairs - 1)

    def consumed(p):
        return jnp.maximum(p - 1, 0)

    grid_spec = pltpu.PrefetchScalarGridSpec(
        num_scalar_prefetch=2,
        grid=(HEADS, n_pairs + 1),
        in_specs=[
            pl.BlockSpec((1, 1, LANES), lambda h, p, qi_r, kj_r: (h, 0, 0)),
            pl.BlockSpec((4, MAP_DIM), lambda h, p, qi_r, kj_r: (0, 0)),
            pl.BlockSpec((1, HEAD_DIM), lambda h, p, qi_r, kj_r: (0, 0)),
            pl.BlockSpec((1, 2, tk, HEAD_DIM), lambda h, p, qi_r, kj_r: (h, 0, 0, 0)),
            pl.BlockSpec((tq, HEAD_DIM), lambda h, p, qi_r, kj_r: (qi_r[scored(p)], h)),
            pl.BlockSpec((tk, HEAD_DIM), lambda h, p, qi_r, kj_r: (kj_r[scored(p)], h)),
            pl.BlockSpec((HEAD_ROWS_T, tk), lambda h, p, qi_r, kj_r: (h, kj_r[consumed(p)])),
        ],
        out_specs=pl.BlockSpec((tq, HEAD_DIM), lambda h, p, qi_r, kj_r: (qi_r[consumed(p)], h)),
        scratch_shapes=[
            pltpu.VMEM((2, tk, tq), F32), pltpu.VMEM((2, tk, tq), F32),
            pltpu.VMEM((2, 1, tq), F32), pltpu.VMEM((2, 1, tq), F32),
            pltpu.VMEM((tk, tk), F32),
            pltpu.VMEM((2, 1, tq), F32), pltpu.VMEM((2, HEAD_ROWS_T, tq), F32),
        ],
    )
    return pl.pallas_call(
        functools.partial(_attn_prompt_kernel, tq=tq, tk=tk, col_block=tq, n_pairs=n_pairs, lam_init=lam_init),
        grid_spec=grid_spec,
        out_shape=jax.ShapeDtypeStruct((t_all, D_MODEL), BF16),
        compiler_params=_params("arbitrary", "arbitrary"),
        name="diff_attention_prompt",
    )(jnp.asarray(qi), jnp.asarray(kj), slopes, lam_p, sub_gain.reshape(1, HEAD_DIM), key_cols, q, k, v_t)


def _attn_decode_kernel(slope_ref, lam_ref, sg_ref, q_ref, k_ref, v_ref, kn_ref, vn_ref, o_ref, m_ref, l_ref, acc_ref,
                        *, tk, pos0, lam_init):
    j = pl.program_id(1)
    nk = pl.num_programs(1)
    n = q_ref.shape[1]

    @pl.when(j == 0)
    def _():
        m_ref[...] = jnp.full_like(m_ref, NEG)
        l_ref[...] = jnp.zeros_like(l_ref)
        acc_ref[...] = jnp.zeros_like(acc_ref)

    row = lax.broadcasted_iota(jnp.int32, (2 * n, 1), 0)
    q_pos = pos0 + jnp.where(row < n, row, row - n)
    lane = lax.broadcasted_iota(jnp.int32, (1, HEAD_DIM), 1)
    map_of_row_has_lane = (row < n) == (lane < MAP_DIM)

    def update(keys, values, k0, n_keys):
        k_pos = k0 + lax.broadcasted_iota(jnp.int32, (1, n_keys), 1)
        allowed = (k_pos >> CHUNK_SHIFT) <= (q_pos >> CHUNK_SHIFT)
        dist = jnp.abs(q_pos - k_pos).astype(F32)
        for h0 in range(0, HEADS, DECODE_HEAD_GROUP):
            heads = range(h0, h0 + DECODE_HEAD_GROUP)
            scores, probs, alphas = {}, {}, {}
            for h in heads:
                slope = slope_ref[h, :, 0:1] * LOG2E
                q = q_ref[0, :, h * HEAD_DIM:(h + 1) * HEAD_DIM]
                q2 = jnp.concatenate([q, q], axis=0)
                q_stack = jnp.where(map_of_row_has_lane, q2, jnp.zeros_like(q2))
                scores[h] = jnp.where(allowed, _dot_nt(q_stack, keys(h).astype(BF16)) - slope * dist, NEG)
            for h in heads:
                m_old = m_ref[h]
                m_new = jnp.maximum(m_old, jnp.max(scores[h], axis=-1, keepdims=True))
                alphas[h] = jnp.exp2(m_old - m_new)
                p = jnp.exp2(scores[h] - m_new)
                l_ref[h] = alphas[h] * l_ref[h] + jnp.sum(p, axis=-1, keepdims=True)
                probs[h] = p.astype(BF16)
                m_ref[h] = m_new
            pv = {h: _dot(probs[h], values(h).astype(BF16)) for h in heads}
            for h in heads:
                acc_ref[h] = alphas[h] * acc_ref[h] + pv[h]

    def head_rows(h):
        return pl.ds(h, tk, stride=HEADS)

    update(lambda h: k_ref[0, head_rows(h), :], lambda h: v_ref[0, head_rows(h), :], j * tk, tk)

    @pl.when(j == nk - 1)
    def _():
        update(lambda h: kn_ref[0, :, h * HEAD_DIM:(h + 1) * HEAD_DIM],
               lambda h: vn_ref[0, :, h * HEAD_DIM:(h + 1) * HEAD_DIM], pos0, n)
        lam = _lambda(lam_ref, lam_init)
        for h in range(HEADS):
            o = acc_ref[h] / l_ref[h]
            o = o[0:n] - lam * o[n:2 * n]
            o_ref[0, :, h * HEAD_DIM:(h + 1) * HEAD_DIM] = (_rms(o, sg_ref[...]) * (1.0 - lam_init)).astype(o_ref.dtype)


def diff_attention_decode(q, k, v, k_new, v_new, slopes, lam_p, sub_gain, *, tk, lam_init):
    b, n, _ = q.shape
    t_k = k.shape[1]
    assert t_k % tk == 0 and t_k % CHUNK == 0
    k = k.reshape(b, t_k * HEADS, HEAD_DIM)
    v = v.reshape(b, t_k * HEADS, HEAD_DIM)
    row_block = pl.BlockSpec((1, n, D_MODEL), lambda bi, j: (bi, 0, 0))
    kv_spec = pl.BlockSpec((1, tk * HEADS, HEAD_DIM), lambda bi, j: (bi, j, 0))
    return pl.pallas_call(
        functools.partial(_attn_decode_kernel, tk=tk, pos0=t_k, lam_init=lam_init),
        grid=(b, t_k // tk),
        in_specs=[
            pl.BlockSpec((HEADS, 1, LANES), lambda bi, j: (0, 0, 0)),
            pl.BlockSpec((4, MAP_DIM), lambda bi, j: (0, 0)),
            pl.BlockSpec((1, HEAD_DIM), lambda bi, j: (0, 0)),
            row_block, kv_spec, kv_spec, row_block, row_block,
        ],
        out_specs=row_block,
        out_shape=jax.ShapeDtypeStruct((b, n, D_MODEL), BF16),
        scratch_shapes=[
            pltpu.VMEM((HEADS, 2 * n, 1), F32),
            pltpu.VMEM((HEADS, 2 * n, 1), F32),
            pltpu.VMEM((HEADS, 2 * n, HEAD_DIM), F32),
        ],
        compiler_params=_params("arbitrary", "arbitrary"),
        name="diff_attention_decode",
    )(slopes, lam_p, sub_gain.reshape(1, HEAD_DIM), q, k, v, k_new, v_new)


class Tiles(NamedTuple):
    wide_rows: int
    rows: int
    mlp_hidden: int
    gdn_chunk: int
    gdn_group: int
    attn_q: int
    attn_k: int


def _tiles(batch, seq):
    rows = batch * seq
    if seq > CHUNK:
        return Tiles(wide_rows=512, rows=1024, mlp_hidden=2048, gdn_chunk=CHUNK, gdn_group=4, attn_q=1024, attn_k=512)
    return Tiles(wide_rows=rows, rows=rows, mlp_hidden=2048, gdn_chunk=seq, gdn_group=1, attn_q=seq, attn_k=1024)


def _trunk(x, conv_state, ssm_state, past_k, past_v, wts):
    b, t, _ = x.shape
    m = b * t
    tiles = _tiles(b, t)
    xf = x.reshape(m, D_MODEL)
    new_conv, new_ssm = [], []
    k_new = v_new = k_bf = v_t = None
    for l in range(DEPTH):
        gains = wts["norm_gains"][l]
        if l < N_A:
            proj = norm_matmul(xf, gains[0], wts["a_w_in"], layer=l, tm=tiles.wide_rows,
                               tn=A_IN_PAD).reshape(b, t, A_IN_PAD)
            w, uu, qg, kd, qkd, egl, conv_l = gdn_prep(proj, conv_state[l], wts["a_conv_w"][l], wts["gate_par"][l],
                                                       C=tiles.gdn_chunk, G=tiles.gdn_group)
            og, s_l = gdn_scan(w, uu, qg, kd, qkd, egl, proj, ssm_state[l], wts["a_o_gain"][l],
                               C=tiles.gdn_chunk, G=tiles.gdn_group)
            new_conv.append(conv_l)
            new_ssm.append(s_l)
            xf = matmul_norm_res(og.reshape(m, D_MODEL), wts["a_w_out"][l], gains[1], xf, tm=tiles.rows)
        else:
            j = l - N_A
            prompt = past_k is None
            if j == 0:
                kv = dict(tm=tiles.rows, tn=D_MODEL, head_major_first=prompt)
                k_new, k_bf = norm_matmul(xf, wts["kv_gain"], wts["w_kv"], columns=(0, D_MODEL),
                                          out_dtypes=(F32, BF16), **kv)
                v_new = norm_matmul(xf, wts["kv_gain"], wts["w_kv"], columns=(D_MODEL, D_MODEL),
                                    head_transposed_copy=prompt, **kv)
                if prompt:
                    v_new, v_t = v_new
            q = norm_matmul(xf, gains[0], wts["b_w_q"][j], tm=tiles.rows, tn=D_MODEL, out_dtypes=(BF16,), out_scale=Q_SCALE)
            lam_init = 0.8 - 0.6 * math.exp(-0.3 * l)
            tail = (wts["b_lam"][j], wts["b_sub_gain"][j])
            if prompt:
                assert b == 1
                o = diff_attention_prompt(q, k_bf, v_t, wts["slopes"], alibi_key_columns(tiles.attn_k),
                                          *tail, tq=tiles.attn_q, tk=tiles.attn_k, lam_init=lam_init)
            else:
                o = diff_attention_decode(q.reshape(b, t, D_MODEL), past_k, past_v, k_new.reshape(b, t, D_MODEL),
                                          v_new.reshape(b, t, D_MODEL), wts["slopes"], *tail, tk=tiles.attn_k,
                                          lam_init=lam_init)
            xf = matmul_norm_res(o.reshape(m, D_MODEL), wts["b_w_out"][j], gains[1], xf, tm=tiles.rows)
        xf = mlp_block(xf, gains[2], wts["mlp_w1"], wts["mlp_w2"], gains[3], layer=l,
                       tm=tiles.wide_rows, tf=tiles.mlp_hidden)
    kv_shape = (b, t, HEADS, HEAD_DIM)
    return (xf.reshape(b, t, D_MODEL), jnp.stack(new_conv), jnp.stack(new_ssm),
            k_new.reshape(kv_shape), v_new.reshape(kv_shape))


def kernel(x_prompt, x_sample, state_conv, state_ssm, cache_k, cache_v, norm_gains, a_w_in, a_conv_w, a_log,
           a_dt_bias, a_o_gain, a_w_out, kv_gain, w_kv, b_w_q, b_lam, b_sub_gain, b_w_out, mlp_w1, mlp_w2):
    a_in = a_w_in.shape[-1]
    gate_par = jnp.zeros((N_A, 2, LANES), F32)
    gate_par = gate_par.at[:, 0, HEADS:2 * HEADS].set(a_log.astype(F32))
    gate_par = gate_par.at[:, 1, HEADS:2 * HEADS].set(a_dt_bias.astype(F32))
    slopes = jnp.asarray(alibi_slopes())
    wts = dict(
        norm_gains=norm_gains.astype(F32),
        a_w_in=jnp.pad(a_w_in, ((0, 0), (0, 0), (0, A_IN_PAD - a_in))).astype(BF16),
        a_conv_w=a_conv_w.astype(F32),
        gate_par=gate_par,
        a_o_gain=a_o_gain.astype(F32),
        a_w_out=a_w_out.astype(BF16),
        kv_gain=kv_gain.astype(F32),
        w_kv=w_kv.astype(BF16),
        b_w_q=b_w_q.astype(BF16),
        b_lam=b_lam.astype(F32),
        b_sub_gain=b_sub_gain.astype(F32),
        b_w_out=b_w_out.astype(BF16),
        mlp_w1=mlp_w1.astype(BF16),
        mlp_w2=mlp_w2.astype(BF16),
        slopes=jnp.broadcast_to(slopes[:, None, None], (HEADS, 1, LANES)),
    )
    bp = x_prompt.shape[0]
    conv0 = jnp.zeros((N_A, bp, CONV_W - 1, A_QKV), F32)
    ssm0 = jnp.zeros((N_A, bp, HEADS, HEAD_DIM, HEAD_DIM), F32)
    y_p, p_conv, p_ssm, p_k, p_v = _trunk(x_prompt, conv0, ssm0, None, None, wts)
    y_s, s_conv, s_ssm, s_k, s_v = _trunk(x_sample, state_conv, state_ssm, cache_k, cache_v, wts)
    return (y_p, y_s, p_conv, p_ssm, p_k, p_v, s_conv, s_ssm, s_k, s_v)
```

```python
import functools
import math
from typing import NamedTuple

import jax
import jax.numpy as jnp
import numpy as np
from jax import lax
from jax.experimental import pallas as pl
from jax.experimental.pallas import tpu as pltpu

F32 = jnp.float32
BF16 = jnp.bfloat16

D_MODEL = 1024
DEPTH = 4
N_A = DEPTH // 2
CHUNK = 64
CHUNK_SHIFT = CHUNK.bit_length() - 1
assert 1 << CHUNK_SHIFT == CHUNK
HEADS = 8
HEAD_DIM = D_MODEL // HEADS
CONV_W = 4
A_QKV = 3 * D_MODEL
A_GATE_COL = A_QKV + D_MODEL
LANES = 128
A_IN_PAD = A_GATE_COL + LANES
MAP_DIM = HEAD_DIM // 2
EPS = 1e-6
NEG = -1e30
VMEM_LIMIT = 48 * 1024 * 1024
HIGHEST = lax.Precision.HIGHEST
ONES_ROWS = 16
HEAD_ROWS_T = HEAD_DIM + ONES_ROWS
LOG2E = math.log2(math.e)
Q_SCALE = MAP_DIM ** -0.5 * LOG2E


def _dot(a, b, precision=None):
    return jnp.dot(a, b, preferred_element_type=F32, precision=precision)


def _dot_nt(a, b):
    return lax.dot_general(a, b, (((1,), (1,)), ((), ())), preferred_element_type=F32)


def _dot_tn(a, b):
    return lax.dot_general(a, b, (((0,), (0,)), ((), ())), preferred_element_type=F32)


def _split(a):
    hi = a.astype(BF16)
    return hi, (a - hi.astype(F32)).astype(BF16)


def _dot_split(a, b):
    return _dot(a[0], b[0]) + (_dot(a[0], b[1]) + _dot(a[1], b[0]))


def _rms(x, gain):
    return x * lax.rsqrt(jnp.mean(x * x, axis=-1, keepdims=True) + EPS) * gain


def _sigmoid(x):
    return 1.0 / (1.0 + jnp.exp(-x))


def _params(*semantics):
    return pltpu.CompilerParams(dimension_semantics=semantics, vmem_limit_bytes=VMEM_LIMIT)


def _norm_matmul_kernel(x_ref, g_ref, w_ref, *out_and_scratch, out_scale, head_transposed_copy, head_major_first):
    *o_refs, hn_ref = out_and_scratch

    @pl.when(pl.program_id(1) == 0)
    def _():
        hn_ref[...] = _rms(x_ref[...], g_ref[...]).astype(BF16)

    y = _dot(hn_ref[...], w_ref[...])
    if out_scale != 1.0:
        y = y * out_scale
    if head_transposed_copy:
        *o_refs, ot_ref = o_refs
        y_t = y.T.astype(ot_ref.dtype)
        for h in range(y_t.shape[0] // HEAD_DIM):
            r0 = h * HEAD_ROWS_T
            ot_ref[r0:r0 + HEAD_DIM, :] = y_t[h * HEAD_DIM:(h + 1) * HEAD_DIM]
            ot_ref[r0 + HEAD_DIM:r0 + HEAD_ROWS_T, :] = jnp.ones((ONES_ROWS, y_t.shape[1]), ot_ref.dtype)
    if head_major_first:
        o_ref, *o_refs = o_refs
        rows = y.shape[0]
        for h in range(y.shape[1] // HEAD_DIM):
            o_ref[pl.ds(h, rows, stride=y.shape[1] // HEAD_DIM), :] = (
                y[:, h * HEAD_DIM:(h + 1) * HEAD_DIM].astype(o_ref.dtype))
    for o_ref in o_refs:
        o_ref[...] = y.astype(o_ref.dtype)


def norm_matmul(x, gain, w, *, tm, tn, out_dtypes=(F32,), out_scale=1.0, head_transposed_copy=False, columns=None,
                layer=None, head_major_first=False):
    m, k = x.shape
    first, n = columns or (0, w.shape[-1])
    first_tile = first // tn
    if layer is None:
        w_spec = pl.BlockSpec((k, tn), lambda i, j: (0, first_tile + j))
    else:
        w_spec = pl.BlockSpec((None, k, tn), lambda i, j: (layer, 0, first_tile + j))
    out_specs = [pl.BlockSpec((tm, tn), lambda i, j: (i, j)) for _ in out_dtypes]
    out_shape = [jax.ShapeDtypeStruct((m, n), dt) for dt in out_dtypes]
    if head_major_first:
        assert tn == n
        heads = n // HEAD_DIM
        out_specs[0] = pl.BlockSpec((tm * heads, HEAD_DIM), lambda i, j: (i, 0))
        out_shape[0] = jax.ShapeDtypeStruct((m * heads, HEAD_DIM), out_dtypes[0])
    if head_transposed_copy:
        out_specs.append(pl.BlockSpec((tn // HEAD_DIM * HEAD_ROWS_T, tm), lambda i, j: (j, i)))
        out_shape.append(jax.ShapeDtypeStruct((n // HEAD_DIM * HEAD_ROWS_T, m), BF16))
    outs = pl.pallas_call(
        functools.partial(_norm_matmul_kernel, out_scale=out_scale, head_transposed_copy=head_transposed_copy,
                          head_major_first=head_major_first),
        grid=(m // tm, n // tn),
        in_specs=[
            pl.BlockSpec((tm, k), lambda i, j: (i, 0)),
            pl.BlockSpec((1, k), lambda i, j: (0, 0)),
            w_spec,
        ],
        out_specs=out_specs,
        out_shape=out_shape,
        scratch_shapes=[pltpu.VMEM((tm, k), BF16)],
        compiler_params=_params("parallel", "arbitrary"),
        name="norm_matmul",
    )(x, gain.reshape(1, k), w)
    return outs[0] if len(outs) == 1 else outs


def _matmul_norm_res_kernel(a_ref, w_ref, g_ref, x_ref, o_ref):
    y = _dot(a_ref[...], w_ref[...])
    o_ref[...] = x_ref[...] + _rms(y, g_ref[...])


def matmul_norm_res(a, w, gain, x, *, tm):
    m, k = a.shape
    d = w.shape[1]
    return pl.pallas_call(
        _matmul_norm_res_kernel,
        grid=(m // tm,),
        in_specs=[
            pl.BlockSpec((tm, k), lambda i: (i, 0)),
            pl.BlockSpec((k, d), lambda i: (0, 0)),
            pl.BlockSpec((1, d), lambda i: (0, 0)),
            pl.BlockSpec((tm, d), lambda i: (i, 0)),
        ],
        out_specs=pl.BlockSpec((tm, d), lambda i: (i, 0)),
        out_shape=jax.ShapeDtypeStruct((m, d), F32),
        compiler_params=_params("parallel"),
        name="matmul_norm_res",
    )(a, w, gain.reshape(1, d), x)


def _mlp_kernel(x_ref, g_in_ref, w1_ref, w2_ref, g_out_ref, o_ref, hn_ref, acc_ref):
    f = pl.program_id(1)

    @pl.when(f == 0)
    def _():
        hn_ref[...] = _rms(x_ref[...], g_in_ref[...]).astype(BF16)
        acc_ref[...] = jnp.zeros_like(acc_ref)

    h = _dot(hn_ref[...], w1_ref[...])
    h = jnp.square(jnp.maximum(h, 0.0)).astype(BF16)
    acc_ref[...] += _dot(h, w2_ref[...])

    @pl.when(f == pl.num_programs(1) - 1)
    def _():
        o_ref[...] = x_ref[...] + _rms(acc_ref[...], g_out_ref[...])


def mlp_block(x, g_in, w1, w2, g_out, *, layer, tm, tf):
    m, d = x.shape
    ff = w1.shape[2]
    return pl.pallas_call(
        _mlp_kernel,
        grid=(m // tm, ff // tf),
        in_specs=[
            pl.BlockSpec((tm, d), lambda i, f: (i, 0)),
            pl.BlockSpec((1, d), lambda i, f: (0, 0)),
            pl.BlockSpec((None, d, tf), lambda i, f: (layer, 0, f)),
            pl.BlockSpec((None, tf, d), lambda i, f: (layer, f, 0)),
            pl.BlockSpec((1, d), lambda i, f: (0, 0)),
        ],
        out_specs=pl.BlockSpec((tm, d), lambda i, f: (i, 0)),
        out_shape=jax.ShapeDtypeStruct((m, d), F32),
        scratch_shapes=[pltpu.VMEM((tm, d), BF16), pltpu.VMEM((tm, d), F32)],
        compiler_params=_params("parallel", "arbitrary"),
        name="mlp_block",
    )(x, g_in.reshape(1, d), w1, w2, g_out.reshape(1, d))


CONV_ROW0 = 8
UNIT_GROUP = 16


def _gdn_prep_kernel(proj_ref, conv0_ref, cw_ref, gp_ref,
                     w_ref, uu_ref, qg_ref, kd_ref, qkd_ref, egl_ref, conv_ref, xbuf_ref, *, C, G):
    R = C * G
    prev0 = CONV_ROW0 - (CONV_W - 1)

    @pl.when(pl.program_id(1) == 0)
    def _():
        xbuf_ref[prev0:CONV_ROW0, :] = conv0_ref[0]

    xbuf_ref[CONV_ROW0:CONV_ROW0 + R, :] = proj_ref[0, :, 0:A_QKV]
    cw = cw_ref[...]
    y = xbuf_ref[prev0:prev0 + R, :] * cw[0:1, :]
    for j in range(1, CONV_W):
        y = y + xbuf_ref[prev0 + j:prev0 + j + R, :] * cw[j:j + 1, :]
    y = y * _sigmoid(y)
    last = xbuf_ref[prev0 + R:CONV_ROW0 + R, :]
    xbuf_ref[prev0:CONV_ROW0, :] = last
    conv_ref[0] = last

    tail = proj_ref[0, :, A_GATE_COL:A_IN_PAD]
    gp = gp_ref[...]
    beta = _sigmoid(tail)
    t = tail + gp[1:2, :]
    softplus = jnp.maximum(t, 0.0) + jnp.log(1.0 + jnp.exp(-jnp.abs(t)))
    g = -jnp.exp(gp[0:1, :]) * softplus

    rr = lax.broadcasted_iota(jnp.int32, (R, R), 0)
    cc = lax.broadcasted_iota(jnp.int32, (R, R), 1)
    shift = C.bit_length() - 1
    same = (rr >> shift) == (cc >> shift)
    gc = _dot(jnp.where(same & (rr >= cc), 1.0, 0.0), g, HIGHEST)
    gl = _dot(jnp.where(same, 1.0, 0.0), g, HIGHEST)
    pad_rows = -R % LANES
    gc_t = (jnp.concatenate([gc, jnp.zeros((pad_rows, LANES), F32)], axis=0) if pad_rows else gc).T
    eg = jnp.exp(gc)
    ekd = jnp.exp(gl - gc)
    egl = jnp.exp(gl)
    for gi in range(G):
        egl_ref[0, gi] = egl[gi * C:gi * C + 1, :]

    row = lax.broadcasted_iota(jnp.int32, (C, C), 0)
    col = lax.broadcasted_iota(jnp.int32, (C, C), 1)
    incl = row >= col
    strict = row > col
    eye = (row == col).astype(F32)

    units = [(gi, h) for gi in range(G) for h in range(HEADS)]
    qn, kn = [], []
    for h in range(HEADS):
        q = y[:, h * HEAD_DIM:(h + 1) * HEAD_DIM]
        k = y[:, D_MODEL + h * HEAD_DIM:D_MODEL + (h + 1) * HEAD_DIM]
        qn.append(q * lax.rsqrt(jnp.sum(q * q, axis=-1, keepdims=True) + EPS) * (HEAD_DIM ** -0.5))
        kn.append(k * lax.rsqrt(jnp.sum(k * k, axis=-1, keepdims=True) + EPS))

    def rows(gi):
        return slice(gi * C, (gi + 1) * C)

    def process(units):
        decay, qk, a_mat = {}, {}, {}
        for gi, h in units:
            rs, gl_h = rows(gi), HEADS + h
            g_col = gc[rs, gl_h:gl_h + 1]
            g_row = gc_t[gl_h:gl_h + 1, gi * C:(gi + 1) * C]
            decay[gi, h] = jnp.where(incl, jnp.exp(jnp.minimum(g_col - g_row, 0.0)), 0.0)
            k_bf = kn[h][rs].astype(BF16)
            qk_kk = _dot_nt(jnp.concatenate([qn[h][rs].astype(BF16), k_bf], axis=0), k_bf)
            qk[gi, h] = qk_kk[0:C]
            a_mat[gi, h] = jnp.where(strict, beta[rs, h:h + 1] * qk_kk[C:2 * C] * decay[gi, h], 0.0)
        t_inv = {u: eye - a_mat[u] for u in units}
        pw = {}
        for u in units:
            a_s = _split(a_mat[u])
            pw[u] = _dot_split(a_s, a_s)
        n_double = C.bit_length() - 2
        for step in range(n_double):
            for u in units:
                p_s = _split(pw[u])
                t_s = _split(t_inv[u])
                if step + 1 < n_double:
                    tp = _dot_split(tuple(jnp.concatenate([t_part, p_part], axis=0)
                                          for t_part, p_part in zip(t_s, p_s)), p_s)
                    t_inv[u] = t_inv[u] + tp[0:C]
                    pw[u] = tp[C:2 * C]
                else:
                    t_inv[u] = t_inv[u] + _dot_split(t_s, p_s)
        for gi, h in units:
            rs, gl_h = rows(gi), HEADS + h
            hs = slice(h * HEAD_DIM, (h + 1) * HEAD_DIM)
            b_col = beta[rs, h:h + 1]
            eg_col = eg[rs, gl_h:gl_h + 1]
            k = kn[h][rs]
            v = y[rs, 2 * D_MODEL + h * HEAD_DIM:2 * D_MODEL + (h + 1) * HEAD_DIM]
            rhs = jnp.concatenate([b_col * v, (b_col * eg_col) * k], axis=-1).astype(BF16)
            sol = _dot(t_inv[gi, h].astype(BF16), rhs)
            uu_ref[0, rs, hs] = sol[:, 0:HEAD_DIM].astype(uu_ref.dtype)
            w_ref[0, rs, hs] = sol[:, HEAD_DIM:].astype(w_ref.dtype)
            qg_ref[0, rs, hs] = (eg_col * qn[h][rs]).astype(qg_ref.dtype)
            kd_ref[0, rs, hs] = (ekd[rs, gl_h:gl_h + 1] * k).astype(kd_ref.dtype)
            qkd_ref[0, rs, h * C:(h + 1) * C] = (qk[gi, h] * decay[gi, h]).astype(qkd_ref.dtype)

    for g0 in range(0, len(units), UNIT_GROUP):
        process(units[g0:g0 + UNIT_GROUP])


def gdn_prep(proj, conv0, conv_w, gate_par, *, C, G):
    b, t, _ = proj.shape
    r = C * G
    nc = t // C
    row_spec = pl.BlockSpec((1, r, D_MODEL), lambda i, c: (i, c, 0))
    conv_spec = pl.BlockSpec((1, CONV_W - 1, A_QKV), lambda i, c: (i, 0, 0))
    return pl.pallas_call(
        functools.partial(_gdn_prep_kernel, C=C, G=G),
        grid=(b, t // r),
        in_specs=[
            pl.BlockSpec((1, r, A_IN_PAD), lambda i, c: (i, c, 0)),
            conv_spec,
            pl.BlockSpec((CONV_W, A_QKV), lambda i, c: (0, 0)),
            pl.BlockSpec((2, LANES), lambda i, c: (0, 0)),
        ],
        out_specs=[
            row_spec, row_spec, row_spec, row_spec,
            pl.BlockSpec((1, r, HEADS * C), lambda i, c: (i, c, 0)),
            pl.BlockSpec((1, G, 1, LANES), lambda i, c: (i, c, 0, 0)),
            conv_spec,
        ],
        out_shape=[jax.ShapeDtypeStruct((b, t, D_MODEL), BF16)] * 4 + [
            jax.ShapeDtypeStruct((b, t, HEADS * C), BF16),
            jax.ShapeDtypeStruct((b, nc, 1, LANES), F32),
            jax.ShapeDtypeStruct((b, CONV_W - 1, A_QKV), F32),
        ],
        scratch_shapes=[pltpu.VMEM((CONV_ROW0 + r, A_QKV), F32)],
        compiler_params=_params("arbitrary", "arbitrary"),
        name="gdn_prep",
    )(proj, conv0, conv_w, gate_par)


def _gdn_scan_kernel(w_ref, uu_ref, qg_ref, kd_ref, qkd_ref, egl_ref, z_ref, s0_ref, og_ref, o_ref, s_ref, *, C, G):
    @pl.when(pl.program_id(1) == 0)
    def _():
        s_ref[0] = s0_ref[0]

    heads = range(HEADS)
    for gi in range(G):
        rs = slice(gi * C, (gi + 1) * C)
        egl = egl_ref[0, gi]

        def hs(h):
            return slice(h * HEAD_DIM, (h + 1) * HEAD_DIM)

        s_old = [s_ref[0, h] for h in heads]
        s_bf = [s.astype(BF16) for s in s_old]
        wq = [_dot(jnp.concatenate([w_ref[0, rs, hs(h)], qg_ref[0, rs, hs(h)]], axis=0), s_bf[h]) for h in heads]
        u_bf = [(uu_ref[0, rs, hs(h)].astype(F32) - wq[h][0:C]).astype(BF16) for h in heads]
        o = [wq[h][C:2 * C] + _dot(qkd_ref[0, rs, h * C:(h + 1) * C], u_bf[h]) for h in heads]
        for h in heads:
            s_ref[0, h] = egl[:, HEADS + h:HEADS + h + 1] * s_old[h] + _dot_tn(kd_ref[0, rs, hs(h)], u_bf[h])
        for h in heads:
            z = z_ref[0, rs, hs(h)]
            o_ref[0, rs, hs(h)] = (_rms(o[h], og_ref[...]) * (z * _sigmoid(z))).astype(o_ref.dtype)


def gdn_scan(w, uu, qg, kd, qkd, egl, proj, s0, o_gain, *, C, G):
    b, t, _ = w.shape
    r = C * G
    row_spec = pl.BlockSpec((1, r, D_MODEL), lambda i, c: (i, c, 0))
    state_spec = pl.BlockSpec((1, HEADS, HEAD_DIM, HEAD_DIM), lambda i, c: (i, 0, 0, 0))
    return pl.pallas_call(
        functools.partial(_gdn_scan_kernel, C=C, G=G),
        grid=(b, t // r),
        in_specs=[
            row_spec, row_spec, row_spec, row_spec,
            pl.BlockSpec((1, r, HEADS * C), lambda i, c: (i, c, 0)),
            pl.BlockSpec((1, G, 1, LANES), lambda i, c: (i, c, 0, 0)),
            pl.BlockSpec((1, r, D_MODEL), lambda i, c: (i, c, A_QKV // D_MODEL)),
            state_spec,
            pl.BlockSpec((1, HEAD_DIM), lambda i, c: (0, 0)),
        ],
        out_specs=[row_spec, state_spec],
        out_shape=[
            jax.ShapeDtypeStruct((b, t, D_MODEL), BF16),
            jax.ShapeDtypeStruct((b, HEADS, HEAD_DIM, HEAD_DIM), F32),
        ],
        compiler_params=_params("arbitrary", "arbitrary"),
        name="gdn_scan",
    )(w, uu, qg, kd, qkd, egl, proj, s0, o_gain.reshape(1, HEAD_DIM))


def _lambda(lam_ref, lam_init):
    lp = lam_ref[...]
    return (jnp.exp(jnp.sum(lp[0:1] * lp[1:2], axis=-1, keepdims=True))
            - jnp.exp(jnp.sum(lp[2:3] * lp[3:4], axis=-1, keepdims=True)) + lam_init)


DECODE_HEAD_GROUP = HEADS
KEY_COLS = 3


def _attn_prompt_kernel(qi_ref, kj_ref, slope_ref, lam_ref, sg_ref, kx_ref, q_ref, k_ref, vt_ref, o_ref,
                        s_even_ref, s_odd_ref, smax_even_ref, smax_odd_ref, diag_ref, m_ref, acc_ref,
                        *, tq, tk, col_block, n_pairs, lam_init):
    step = pl.program_id(1)
    slope = slope_ref[0, :, 0:1] * LOG2E
    lane = lax.broadcasted_iota(jnp.int32, (1, HEAD_DIM), 1)
    map_lanes = [lane < MAP_DIM, lane >= MAP_DIM]

    def score_stage(s_ref, smax_ref, mp, cols):
        q = q_ref[cols, :]
        k = k_ref[...]
        a0 = MAP_DIM if mp == 0 else 0
        extra = (lane >= a0) & (lane < a0 + KEY_COLS)
        qa = jnp.where(map_lanes[mp], q, jnp.where(extra, 1.0, 0.0).astype(BF16))
        ka = jnp.where(map_lanes[mp], k, kx_ref[0, mp])
        scores = _dot_nt(ka, qa)
        s_ref[mp, :, cols] = scores
        smax_ref[mp, :, cols] = jnp.max(scores, axis=0, keepdims=True)

    def mask_diagonal(s_ref, smax_ref, key_offset):
        facing = slice(key_offset, key_offset + tk)
        for mp in range(2):
            if key_offset:
                s_ref[mp, :, 0:key_offset] = jnp.full((tk, key_offset), NEG, F32)
                smax_ref[mp, :, 0:key_offset] = jnp.full((1, key_offset), NEG, F32)
            fixed = s_ref[mp, :, facing] + diag_ref[...]
            s_ref[mp, :, facing] = fixed
            smax_ref[mp, :, facing] = jnp.max(fixed, axis=0, keepdims=True)

    def build_diagonal_table():
        j = lax.broadcasted_iota(jnp.int32, (tk, 1), 0)
        i = lax.broadcasted_iota(jnp.int32, (1, tk), 1)
        allowed = (j >> CHUNK_SHIFT) <= (i >> CHUNK_SHIFT)
        fix = jnp.where(j > i, (2.0 * slope) * (i - j).astype(F32), 0.0)
        diag_ref[...] = jnp.where(allowed, fix, NEG)

    def softmax_stage(s_ref, smax_ref, shift_const, mp, cols):
        scores = s_ref[mp, :, cols]
        m_old = m_ref[mp, :, cols]
        m_new = jnp.maximum(m_old, smax_ref[mp, :, cols] + shift_const)
        probs = jnp.exp2(scores - (m_new - shift_const)).astype(BF16)
        m_ref[mp, :, cols] = m_new
        acc_ref[mp, :, cols] = jnp.exp2(m_old - m_new) * acc_ref[mp, :, cols] + _dot(vt_ref[...], probs)

    def finalize():
        lam = _lambda(lam_ref, lam_init)
        a0 = acc_ref[0]
        a1 = acc_ref[1]
        o_t = (a0[0:HEAD_DIM] / a0[HEAD_DIM:HEAD_DIM + 1]
               - lam * (a1[0:HEAD_DIM] / a1[HEAD_DIM:HEAD_DIM + 1]))
        o_t = o_t * (lax.rsqrt(jnp.mean(o_t * o_t, axis=0, keepdims=True) + EPS) * (1.0 - lam_init))
        o_ref[...] = (o_t.T * sg_ref[...]).astype(o_ref.dtype)

    units = [(mp, slice(c0, c0 + col_block)) for mp in range(2) for c0 in range(0, tq, col_block)]
    scored = jnp.minimum(step, n_pairs - 1)
    key_offset_s = kj_ref[scored] * tk - qi_ref[scored] * tq
    consumed = jnp.maximum(step - 1, 0)
    kj_c = kj_ref[consumed]
    key_offset_c = kj_c * tk - qi_ref[consumed] * tq

    @pl.when(step == 0)
    def _():
        build_diagonal_table()
        for mp, cols in units:
            score_stage(s_even_ref, smax_even_ref, mp, cols)

    @pl.when((step > 0) & (kj_c == 0))
    def _():
        m_ref[...] = jnp.full_like(m_ref, NEG)
        acc_ref[...] = jnp.zeros_like(acc_ref)

    tile_shift = slope * key_offset_c.astype(F32)
    even, odd = (s_even_ref, smax_even_ref), (s_odd_ref, smax_odd_ref)
    for parity, (written, read) in enumerate([(even, odd), (odd, even)]):
        @pl.when((step > 0) & (step % 2 == parity))
        def _():
            for mp, cols in units:
                score_stage(*written, mp, cols)
                softmax_stage(*read, tile_shift, mp, cols)

    for parity, written in enumerate([even, odd]):
        for key_offset in range(0, tq, tk):
            @pl.when((key_offset_s == key_offset) & (step % 2 == parity))
            def _():
                mask_diagonal(*written, key_offset)

    @pl.when((step > 0) & (key_offset_c == tq - tk))
    def _():
        finalize()


def alibi_slopes():
    return 2.0 ** (-8.0 * np.arange(1, HEADS + 1, dtype=np.float32) / HEADS)


def alibi_key_columns(t):
    term = (np.float32(LOG2E) * alibi_slopes())[:, None] * np.arange(t, dtype=np.float32)[None, :]
    cols = np.zeros((HEADS, 2, t, HEAD_DIM), np.float32)
    for c in range(KEY_COLS):
        piece = term.astype(BF16).astype(np.float32)
        term = term - piece
        for mp in range(2):
            a0 = MAP_DIM if mp == 0 else 0
            cols[:, mp, :, a0 + c] = piece
    return jnp.asarray(cols.astype(BF16))


def diff_attention_prompt(q, k, v_t, slopes, key_cols, lam_p, sub_gain, *, tq, tk, lam_init):
    t_all = q.shape[0]
    assert tq % tk == 0 and tk % CHUNK == 0 and t_all % tq == 0
    per_q = tq // tk
    qi = np.concatenate([np.full((i + 1) * per_q, i, np.int32) for i in range(t_all // tq)])
    kj = np.concatenate([np.arange((i + 1) * per_q, dtype=np.int32) for i in range(t_all // tq)])
    n_pairs = len(qi)

    def scored(p):
        return jnp.minimum(p, n_pairs - 1)

    def consumed(p):
        return jnp.maximum(p - 1, 0)

    grid_spec = pltpu.PrefetchScalarGridSpec(
        num_scalar_prefetch=2,
        grid=(HEADS, n_pairs + 1),
        in_specs=[
            pl.BlockSpec((1, 1, LANES), lambda h, p, qi_r, kj_r: (h, 0, 0)),
            pl.BlockSpec((4, MAP_DIM), lambda h, p, qi_r, kj_r: (0, 0)),
            pl.BlockSpec((1, HEAD_DIM), lambda h, p, qi_r, kj_r: (0, 0)),
            pl.BlockSpec((1, 2, tk, HEAD_DIM), lambda h, p, qi_r, kj_r: (h, 0, 0, 0)),
            pl.BlockSpec((tq, HEAD_DIM), lambda h, p, qi_r, kj_r: (qi_r[scored(p)], h)),
            pl.BlockSpec((tk, HEAD_DIM), lambda h, p, qi_r, kj_r: (kj_r[scored(p)], h)),
            pl.BlockSpec((HEAD_ROWS_T, tk), lambda h, p, qi_r, kj_r: (h, kj_r[consumed(p)])),
        ],
        out_specs=pl.BlockSpec((tq, HEAD_DIM), lambda h, p, qi_r, kj_r: (qi_r[consumed(p)], h)),
        scratch_shapes=[
            pltpu.VMEM((2, tk, tq), F32), pltpu.VMEM((2, tk, tq), F32),
            pltpu.VMEM((2, 1, tq), F32), pltpu.VMEM((2, 1, tq), F32),
            pltpu.VMEM((tk, tk), F32),
            pltpu.VMEM((2, 1, tq), F32), pltpu.VMEM((2, HEAD_ROWS_T, tq), F32),
        ],
    )
    return pl.pallas_call(
        functools.partial(_attn_prompt_kernel, tq=tq, tk=tk, col_block=tq, n_pairs=n_pairs, lam_init=lam_init),
        grid_spec=grid_spec,
        out_shape=jax.ShapeDtypeStruct((t_all, D_MODEL), BF16),
        compiler_params=_params("arbitrary", "arbitrary"),
        name="diff_attention_prompt",
    )(jnp.asarray(qi), jnp.asarray(kj), slopes, lam_p, sub_gain.reshape(1, HEAD_DIM), key_cols, q, k, v_t)


def _attn_decode_kernel(slope_ref, lam_ref, sg_ref, q_ref, k_ref, v_ref, kn_ref, vn_ref, o_ref, m_ref, l_ref, acc_ref,
                        *, tk, pos0, lam_init):
    j = pl.program_id(1)
    nk = pl.num_programs(1)
    n = q_ref.shape[1]

    @pl.when(j == 0)
    def _():
        m_ref[...] = jnp.full_like(m_ref, NEG)
        l_ref[...] = jnp.zeros_like(l_ref)
        acc_ref[...] = jnp.zeros_like(acc_ref)

    row = lax.broadcasted_iota(jnp.int32, (2 * n, 1), 0)
    q_pos = pos0 + jnp.where(row < n, row, row - n)
    lane = lax.broadcasted_iota(jnp.int32, (1, HEAD_DIM), 1)
    map_of_row_has_lane = (row < n) == (lane < MAP_DIM)

    def update(keys, values, k0, n_keys):
        k_pos = k0 + lax.broadcasted_iota(jnp.int32, (1, n_keys), 1)
        allowed = (k_pos >> CHUNK_SHIFT) <= (q_pos >> CHUNK_SHIFT)
        dist = jnp.abs(q_pos - k_pos).astype(F32)
        for h0 in range(0, HEADS, DECODE_HEAD_GROUP):
            heads = range(h0, h0 + DECODE_HEAD_GROUP)
            scores, probs, alphas = {}, {}, {}
            for h in heads:
                slope = slope_ref[h, :, 0:1] * LOG2E
                q = q_ref[0, :, h * HEAD_DIM:(h + 1) * HEAD_DIM]
                q2 = jnp.concatenate([q, q], axis=0)
                q_stack = jnp.where(map_of_row_has_lane, q2, jnp.zeros_like(q2))
                scores[h] = jnp.where(allowed, _dot_nt(q_stack, keys(h).astype(BF16)) - slope * dist, NEG)
            for h in heads:
                m_old = m_ref[h]
                m_new = jnp.maximum(m_old, jnp.max(scores[h], axis=-1, keepdims=True))
                alphas[h] = jnp.exp2(m_old - m_new)
                p = jnp.exp2(scores[h] - m_new)
                l_ref[h] = alphas[h] * l_ref[h] + jnp.sum(p, axis=-1, keepdims=True)
                probs[h] = p.astype(BF16)
                m_ref[h] = m_new
            pv = {h: _dot(probs[h], values(h).astype(BF16)) for h in heads}
            for h in heads:
                acc_ref[h] = alphas[h] * acc_ref[h] + pv[h]

    def head_rows(h):
        return pl.ds(h, tk, stride=HEADS)

    update(lambda h: k_ref[0, head_rows(h), :], lambda h: v_ref[0, head_rows(h), :], j * tk, tk)

    @pl.when(j == nk - 1)
    def _():
        update(lambda h: kn_ref[0, :, h * HEAD_DIM:(h + 1) * HEAD_DIM],
               lambda h: vn_ref[0, :, h * HEAD_DIM:(h + 1) * HEAD_DIM], pos0, n)
        lam = _lambda(lam_ref, lam_init)
        for h in range(HEADS):
            o = acc_ref[h] / l_ref[h]
            o = o[0:n] - lam * o[n:2 * n]
            o_ref[0, :, h * HEAD_DIM:(h + 1) * HEAD_DIM] = (_rms(o, sg_ref[...]) * (1.0 - lam_init)).astype(o_ref.dtype)


def diff_attention_decode(q, k, v, k_new, v_new, slopes, lam_p, sub_gain, *, tk, lam_init):
    b, n, _ = q.shape
    t_k = k.shape[1]
    assert t_k % tk == 0 and t_k % CHUNK == 0
    k = k.reshape(b, t_k * HEADS, HEAD_DIM)
    v = v.reshape(b, t_k * HEADS, HEAD_DIM)
    row_block = pl.BlockSpec((1, n, D_MODEL), lambda bi, j: (bi, 0, 0))
    kv_spec = pl.BlockSpec((1, tk * HEADS, HEAD_DIM), lambda bi, j: (bi, j, 0))
    return pl.pallas_call(
        functools.partial(_attn_decode_kernel, tk=tk, pos0=t_k, lam_init=lam_init),
        grid=(b, t_k // tk),
        in_specs=[
            pl.BlockSpec((HEADS, 1, LANES), lambda bi, j: (0, 0, 0)),
            pl.BlockSpec((4, MAP_DIM), lambda bi, j: (0, 0)),
            pl.BlockSpec((1, HEAD_DIM), lambda bi, j: (0, 0)),
            row_block, kv_spec, kv_spec, row_block, row_block,
        ],
        out_specs=row_block,
        out_shape=jax.ShapeDtypeStruct((b, n, D_MODEL), BF16),
        scratch_shapes=[
            pltpu.VMEM((HEADS, 2 * n, 1), F32),
            pltpu.VMEM((HEADS, 2 * n, 1), F32),
            pltpu.VMEM((HEADS, 2 * n, HEAD_DIM), F32),
        ],
        compiler_params=_params("arbitrary", "arbitrary"),
        name="diff_attention_decode",
    )(slopes, lam_p, sub_gain.reshape(1, HEAD_DIM), q, k, v, k_new, v_new)


class Tiles(NamedTuple):
    wide_rows: int
    rows: int
    mlp_hidden: int
    gdn_chunk: int
    gdn_group: int
    attn_q: int
    attn_k: int


def _tiles(batch, seq):
    rows = batch * seq
    if seq > CHUNK:
        return Tiles(wide_rows=512, rows=1024, mlp_hidden=2048, gdn_chunk=CHUNK, gdn_group=4, attn_q=1024, attn_k=512)
    return Tiles(wide_rows=rows, rows=rows, mlp_hidden=2048, gdn_chunk=seq, gdn_group=1, attn_q=seq, attn_k=2048)


def _trunk(x, conv_state, ssm_state, past_k, past_v, wts):
    b, t, _ = x.shape
    m = b * t
    tiles = _tiles(b, t)
    xf = x.reshape(m, D_MODEL)
    new_conv, new_ssm = [], []
    k_new = v_new = k_bf = v_t = None
    for l in range(DEPTH):
        gains = wts["norm_gains"][l]
        if l < N_A:
            proj = norm_matmul(xf, gains[0], wts["a_w_in"], layer=l, tm=tiles.wide_rows,
                               tn=A_IN_PAD).reshape(b, t, A_IN_PAD)
            w, uu, qg, kd, qkd, egl, conv_l = gdn_prep(proj, conv_state[l], wts["a_conv_w"][l], wts["gate_par"][l],
                                                       C=tiles.gdn_chunk, G=tiles.gdn_group)
            og, s_l = gdn_scan(w, uu, qg, kd, qkd, egl, proj, ssm_state[l], wts["a_o_gain"][l],
                               C=tiles.gdn_chunk, G=tiles.gdn_group)
            new_conv.append(conv_l)
            new_ssm.append(s_l)
            xf = matmul_norm_res(og.reshape(m, D_MODEL), wts["a_w_out"][l], gains[1], xf, tm=tiles.rows)
        else:
            j = l - N_A
            prompt = past_k is None
            if j == 0:
                kv = dict(tm=tiles.rows, tn=D_MODEL, head_major_first=prompt)
                k_new, k_bf = norm_matmul(xf, wts["kv_gain"], wts["w_kv"], columns=(0, D_MODEL),
                                          out_dtypes=(F32, BF16), **kv)
                v_new = norm_matmul(xf, wts["kv_gain"], wts["w_kv"], columns=(D_MODEL, D_MODEL),
                                    head_transposed_copy=prompt, **kv)
                if prompt:
                    v_new, v_t = v_new
            q = norm_matmul(xf, gains[0], wts["b_w_q"][j], tm=tiles.rows, tn=D_MODEL, out_dtypes=(BF16,), out_scale=Q_SCALE)
            lam_init = 0.8 - 0.6 * math.exp(-0.3 * l)
            tail = (wts["b_lam"][j], wts["b_sub_gain"][j])
            if prompt:
                assert b == 1
                o = diff_attention_prompt(q, k_bf, v_t, wts["slopes"], alibi_key_columns(tiles.attn_k),
                                          *tail, tq=tiles.attn_q, tk=tiles.attn_k, lam_init=lam_init)
            else:
                o = diff_attention_decode(q.reshape(b, t, D_MODEL), past_k, past_v, k_new.reshape(b, t, D_MODEL),
                                          v_new.reshape(b, t, D_MODEL), wts["slopes"], *tail, tk=tiles.attn_k,
                                          lam_init=lam_init)
            xf = matmul_norm_res(o.reshape(m, D_MODEL), wts["b_w_out"][j], gains[1], xf, tm=tiles.rows)
        xf = mlp_block(xf, gains[2], wts["mlp_w1"], wts["mlp_w2"], gains[3], layer=l,
                       tm=tiles.wide_rows, tf=tiles.mlp_hidden)
    kv_shape = (b, t, HEADS, HEAD_DIM)
    return (xf.reshape(b, t, D_MODEL), jnp.stack(new_conv), jnp.stack(new_ssm),
            k_new.reshape(kv_shape), v_new.reshape(kv_shape))


def kernel(x_prompt, x_sample, state_conv, state_ssm, cache_k, cache_v, norm_gains, a_w_in, a_conv_w, a_log,
           a_dt_bias, a_o_gain, a_w_out, kv_gain, w_kv, b_w_q, b_lam, b_sub_gain, b_w_out, mlp_w1, mlp_w2):
    a_in = a_w_in.shape[-1]
    gate_par = jnp.zeros((N_A, 2, LANES), F32)
    gate_par = gate_par.at[:, 0, HEADS:2 * HEADS].set(a_log.astype(F32))
    gate_par = gate_par.at[:, 1, HEADS:2 * HEADS].set(a_dt_bias.astype(F32))
    slopes = jnp.asarray(alibi_slopes())
    wts = dict(
        norm_gains=norm_gains.astype(F32),
        a_w_in=jnp.pad(a_w_in, ((0, 0), (0, 0), (0, A_IN_PAD - a_in))).astype(BF16),
        a_conv_w=a_conv_w.astype(F32),
        gate_par=gate_par,
        a_o_gain=a_o_gain.astype(F32),
        a_w_out=a_w_out.astype(BF16),
        kv_gain=kv_gain.astype(F32),
        w_kv=w_kv.astype(BF16),
        b_w_q=b_w_q.astype(BF16),
        b_lam=b_lam.astype(F32),
        b_sub_gain=b_sub_gain.astype(F32),
        b_w_out=b_w_out.astype(BF16),
        mlp_w1=mlp_w1.astype(BF16),
        mlp_w2=mlp_w2.astype(BF16),
        slopes=jnp.broadcast_to(slopes[:, None, None], (HEADS, 1, LANES)),
    )
    bp = x_prompt.shape[0]
    conv0 = jnp.zeros((N_A, bp, CONV_W - 1, A_QKV), F32)
    ssm0 = jnp.zeros((N_A, bp, HEADS, HEAD_DIM, HEAD_DIM), F32)
    y_p, p_conv, p_ssm, p_k, p_v = _trunk(x_prompt, conv0, ssm0, None, None, wts)
    y_s, s_conv, s_ssm, s_k, s_v = _trunk(x_sample, state_conv, state_ssm, cache_k, cache_v, wts)
    return (y_p, y_s, p_conv, p_ssm, p_k, p_v, s_conv, s_ssm, s_k, s_v)
```

```python
import functools
import math
from typing import NamedTuple

import jax
import jax.numpy as jnp
import numpy as np
from jax import lax
from jax.experimental import pallas as pl
from jax.experimental.pallas import tpu as pltpu

F32 = jnp.float32
BF16 = jnp.bfloat16

D_MODEL = 1024
DEPTH = 4
N_A = DEPTH // 2
CHUNK = 64
CHUNK_SHIFT = CHUNK.bit_length() - 1
assert 1 << CHUNK_SHIFT == CHUNK
HEADS = 8
HEAD_DIM = D_MODEL // HEADS
CONV_W = 4
A_QKV = 3 * D_MODEL
A_GATE_COL = A_QKV + D_MODEL
LANES = 128
A_IN_PAD = A_GATE_COL + LANES
MAP_DIM = HEAD_DIM // 2
EPS = 1e-6
NEG = -1e30
VMEM_LIMIT = 48 * 1024 * 1024
HIGHEST = lax.Precision.HIGHEST
ONES_ROWS = 16
HEAD_ROWS_T = HEAD_DIM + ONES_ROWS
LOG2E = math.log2(math.e)
Q_SCALE = MAP_DIM ** -0.5 * LOG2E


def _dot(a, b, precision=None):
    return jnp.dot(a, b, preferred_element_type=F32, precision=precision)


def _dot_nt(a, b):
    return lax.dot_general(a, b, (((1,), (1,)), ((), ())), preferred_element_type=F32)


def _dot_tn(a, b):
    return lax.dot_general(a, b, (((0,), (0,)), ((), ())), preferred_element_type=F32)


def _split(a):
    hi = a.astype(BF16)
    return hi, (a - hi.astype(F32)).astype(BF16)


def _dot_split(a, b):
    return _dot(a[0], b[0]) + (_dot(a[0], b[1]) + _dot(a[1], b[0]))


def _rms(x, gain):
    return x * lax.rsqrt(jnp.mean(x * x, axis=-1, keepdims=True) + EPS) * gain


def _sigmoid(x):
    return 1.0 / (1.0 + jnp.exp(-x))


def _params(*semantics):
    return pltpu.CompilerParams(dimension_semantics=semantics, vmem_limit_bytes=VMEM_LIMIT)


def _norm_matmul_kernel(x_ref, g_ref, w_ref, *out_and_scratch, out_scale, head_transposed_copy, head_major_first):
    *o_refs, hn_ref = out_and_scratch

    @pl.when(pl.program_id(1) == 0)
    def _():
        hn_ref[...] = _rms(x_ref[...], g_ref[...]).astype(BF16)

    y = _dot(hn_ref[...], w_ref[...])
    if out_scale != 1.0:
        y = y * out_scale
    if head_transposed_copy:
        *o_refs, ot_ref = o_refs
        y_t = y.T.astype(ot_ref.dtype)
        for h in range(y_t.shape[0] // HEAD_DIM):
            r0 = h * HEAD_ROWS_T
            ot_ref[r0:r0 + HEAD_DIM, :] = y_t[h * HEAD_DIM:(h + 1) * HEAD_DIM]
            ot_ref[r0 + HEAD_DIM:r0 + HEAD_ROWS_T, :] = jnp.ones((ONES_ROWS, y_t.shape[1]), ot_ref.dtype)
    if head_major_first:
        o_ref, *o_refs = o_refs
        rows = y.shape[0]
        for h in range(y.shape[1] // HEAD_DIM):
            o_ref[pl.ds(h, rows, stride=y.shape[1] // HEAD_DIM), :] = (
                y[:, h * HEAD_DIM:(h + 1) * HEAD_DIM].astype(o_ref.dtype))
    for o_ref in o_refs:
        o_ref[...] = y.astype(o_ref.dtype)


def norm_matmul(x, gain, w, *, tm, tn, out_dtypes=(F32,), out_scale=1.0, head_transposed_copy=False, columns=None,
                layer=None, head_major_first=False):
    m, k = x.shape
    first, n = columns or (0, w.shape[-1])
    first_tile = first // tn
    if layer is None:
        w_spec = pl.BlockSpec((k, tn), lambda i, j: (0, first_tile + j))
    else:
        w_spec = pl.BlockSpec((None, k, tn), lambda i, j: (layer, 0, first_tile + j))
    out_specs = [pl.BlockSpec((tm, tn), lambda i, j: (i, j)) for _ in out_dtypes]
    out_shape = [jax.ShapeDtypeStruct((m, n), dt) for dt in out_dtypes]
    if head_major_first:
        assert tn == n
        heads = n // HEAD_DIM
        out_specs[0] = pl.BlockSpec((tm * heads, HEAD_DIM), lambda i, j: (i, 0))
        out_shape[0] = jax.ShapeDtypeStruct((m * heads, HEAD_DIM), out_dtypes[0])
    if head_transposed_copy:
        out_specs.append(pl.BlockSpec((tn // HEAD_DIM * HEAD_ROWS_T, tm), lambda i, j: (j, i)))
        out_shape.append(jax.ShapeDtypeStruct((n // HEAD_DIM * HEAD_ROWS_T, m), BF16))
    outs = pl.pallas_call(
        functools.partial(_norm_matmul_kernel, out_scale=out_scale, head_transposed_copy=head_transposed_copy,
                          head_major_first=head_major_first),
        grid=(m // tm, n // tn),
        in_specs=[
            pl.BlockSpec((tm, k), lambda i, j: (i, 0)),
            pl.BlockSpec((1, k), lambda i, j: (0, 0)),
            w_spec,
        ],
        out_specs=out_specs,
        out_shape=out_shape,
        scratch_shapes=[pltpu.VMEM((tm, k), BF16)],
        compiler_params=_params("parallel", "arbitrary"),
        name="norm_matmul",
    )(x, gain.reshape(1, k), w)
    return outs[0] if len(outs) == 1 else outs


def _matmul_norm_res_kernel(a_ref, w_ref, g_ref, x_ref, o_ref):
    y = _dot(a_ref[...], w_ref[...])
    o_ref[...] = x_ref[...] + _rms(y, g_ref[...])


def matmul_norm_res(a, w, gain, x, *, tm):
    m, k = a.shape
    d = w.shape[1]
    return pl.pallas_call(
        _matmul_norm_res_kernel,
        grid=(m // tm,),
        in_specs=[
            pl.BlockSpec((tm, k), lambda i: (i, 0)),
            pl.BlockSpec((k, d), lambda i: (0, 0)),
            pl.BlockSpec((1, d), lambda i: (0, 0)),
            pl.BlockSpec((tm, d), lambda i: (i, 0)),
        ],
        out_specs=pl.BlockSpec((tm, d), lambda i: (i, 0)),
        out_shape=jax.ShapeDtypeStruct((m, d), F32),
        compiler_params=_params("parallel"),
        name="matmul_norm_res",
    )(a, w, gain.reshape(1, d), x)


def _mlp_kernel(x_ref, g_in_ref, w1_ref, w2_ref, g_out_ref, o_ref, hn_ref, acc_ref):
    f = pl.program_id(1)

    @pl.when(f == 0)
    def _():
        hn_ref[...] = _rms(x_ref[...], g_in_ref[...]).astype(BF16)
        acc_ref[...] = jnp.zeros_like(acc_ref)

    h = _dot(hn_ref[...], w1_ref[...])
    h = jnp.square(jnp.maximum(h, 0.0)).astype(BF16)
    acc_ref[...] += _dot(h, w2_ref[...])

    @pl.when(f == pl.num_programs(1) - 1)
    def _():
        o_ref[...] = x_ref[...] + _rms(acc_ref[...], g_out_ref[...])


def mlp_block(x, g_in, w1, w2, g_out, *, layer, tm, tf):
    m, d = x.shape
    ff = w1.shape[2]
    return pl.pallas_call(
        _mlp_kernel,
        grid=(m // tm, ff // tf),
        in_specs=[
            pl.BlockSpec((tm, d), lambda i, f: (i, 0)),
            pl.BlockSpec((1, d), lambda i, f: (0, 0)),
            pl.BlockSpec((None, d, tf), lambda i, f: (layer, 0, f)),
            pl.BlockSpec((None, tf, d), lambda i, f: (layer, f, 0)),
            pl.BlockSpec((1, d), lambda i, f: (0, 0)),
        ],
        out_specs=pl.BlockSpec((tm, d), lambda i, f: (i, 0)),
        out_shape=jax.ShapeDtypeStruct((m, d), F32),
        scratch_shapes=[pltpu.VMEM((tm, d), BF16), pltpu.VMEM((tm, d), F32)],
        compiler_params=_params("parallel", "arbitrary"),
        name="mlp_block",
    )(x, g_in.reshape(1, d), w1, w2, g_out.reshape(1, d))


CONV_ROW0 = 8
UNIT_GROUP = 16


def _gdn_prep_kernel(proj_ref, conv0_ref, cw_ref, gp_ref,
                     w_ref, uu_ref, qg_ref, kd_ref, qkd_ref, egl_ref, conv_ref, xbuf_ref, *, C, G):
    R = C * G
    prev0 = CONV_ROW0 - (CONV_W - 1)

    @pl.when(pl.program_id(1) == 0)
    def _():
        xbuf_ref[prev0:CONV_ROW0, :] = conv0_ref[0]

    xbuf_ref[CONV_ROW0:CONV_ROW0 + R, :] = proj_ref[0, :, 0:A_QKV]
    conv_ref[0] = xbuf_ref[prev0 + R:CONV_ROW0 + R, :]

    def conv_block(c0):
        cs = slice(c0, c0 + HEAD_DIM)
        acc = xbuf_ref[prev0:prev0 + R, cs] * cw_ref[0:1, cs]
        for j in range(1, CONV_W):
            acc = acc + xbuf_ref[prev0 + j:prev0 + j + R, cs] * cw_ref[j:j + 1, cs]
        return acc * _sigmoid(acc)

    tail = proj_ref[0, :, A_GATE_COL:A_IN_PAD]
    gp = gp_ref[...]
    beta = _sigmoid(tail)
    t = tail + gp[1:2, :]
    softplus = jnp.maximum(t, 0.0) + jnp.log(1.0 + jnp.exp(-jnp.abs(t)))
    g = -jnp.exp(gp[0:1, :]) * softplus

    rr = lax.broadcasted_iota(jnp.int32, (R, R), 0)
    cc = lax.broadcasted_iota(jnp.int32, (R, R), 1)
    shift = C.bit_length() - 1
    same = (rr >> shift) == (cc >> shift)
    gc = _dot(jnp.where(same & (rr >= cc), 1.0, 0.0), g, HIGHEST)
    gl = _dot(jnp.where(same, 1.0, 0.0), g, HIGHEST)
    pad_rows = -R % LANES
    gc_t = (jnp.concatenate([gc, jnp.zeros((pad_rows, LANES), F32)], axis=0) if pad_rows else gc).T
    eg = jnp.exp(gc)
    ekd = jnp.exp(gl - gc)
    egl = jnp.exp(gl)
    for gi in range(G):
        egl_ref[0, gi] = egl[gi * C:gi * C + 1, :]

    row = lax.broadcasted_iota(jnp.int32, (C, C), 0)
    col = lax.broadcasted_iota(jnp.int32, (C, C), 1)
    incl = row >= col
    strict = row > col
    eye = (row == col).astype(F32)

    units = [(gi, h) for gi in range(G) for h in range(HEADS)]
    qn, kn, vn = [], [], []
    for h in range(HEADS):
        q = conv_block(h * HEAD_DIM)
        k = conv_block(D_MODEL + h * HEAD_DIM)
        vn.append(conv_block(2 * D_MODEL + h * HEAD_DIM))
        qn.append(q * lax.rsqrt(jnp.sum(q * q, axis=-1, keepdims=True) + EPS) * (HEAD_DIM ** -0.5))
        kn.append(k * lax.rsqrt(jnp.sum(k * k, axis=-1, keepdims=True) + EPS))

    def rows(gi):
        return slice(gi * C, (gi + 1) * C)

    def process(units):
        decay, qk, a_mat = {}, {}, {}
        for gi, h in units:
            rs, gl_h = rows(gi), HEADS + h
            g_col = gc[rs, gl_h:gl_h + 1]
            g_row = gc_t[gl_h:gl_h + 1, gi * C:(gi + 1) * C]
            decay[gi, h] = jnp.where(incl, jnp.exp(jnp.minimum(g_col - g_row, 0.0)), 0.0)
            k_bf = kn[h][rs].astype(BF16)
            qk_kk = _dot_nt(jnp.concatenate([qn[h][rs].astype(BF16), k_bf], axis=0), k_bf)
            qk[gi, h] = qk_kk[0:C]
            a_mat[gi, h] = jnp.where(strict, beta[rs, h:h + 1] * qk_kk[C:2 * C] * decay[gi, h], 0.0)
        t_inv = {u: eye - a_mat[u] for u in units}
        pw = {}
        for u in units:
            a_s = _split(a_mat[u])
            pw[u] = _dot_split(a_s, a_s)
        n_double = C.bit_length() - 2
        for step in range(n_double):
            for u in units:
                p_s = _split(pw[u])
                t_s = _split(t_inv[u])
                if step + 1 < n_double:
                    tp = _dot_split(tuple(jnp.concatenate([t_part, p_part], axis=0)
                                          for t_part, p_part in zip(t_s, p_s)), p_s)
                    t_inv[u] = t_inv[u] + tp[0:C]
                    pw[u] = tp[C:2 * C]
                else:
                    t_inv[u] = t_inv[u] + _dot_split(t_s, p_s)
        for gi, h in units:
            rs, gl_h = rows(gi), HEADS + h
            hs = slice(h * HEAD_DIM, (h + 1) * HEAD_DIM)
            b_col = beta[rs, h:h + 1]
            eg_col = eg[rs, gl_h:gl_h + 1]
            k = kn[h][rs]
            v = vn[h][rs]
            rhs = jnp.concatenate([b_col * v, (b_col * eg_col) * k], axis=-1).astype(BF16)
            sol = _dot(t_inv[gi, h].astype(BF16), rhs)
            uu_ref[0, rs, hs] = sol[:, 0:HEAD_DIM].astype(uu_ref.dtype)
            w_ref[0, rs, hs] = sol[:, HEAD_DIM:].astype(w_ref.dtype)
            qg_ref[0, rs, hs] = (eg_col * qn[h][rs]).astype(qg_ref.dtype)
            kd_ref[0, rs, hs] = (ekd[rs, gl_h:gl_h + 1] * k).astype(kd_ref.dtype)
            qkd_ref[0, rs, h * C:(h + 1) * C] = (qk[gi, h] * decay[gi, h]).astype(qkd_ref.dtype)

    for g0 in range(0, len(units), UNIT_GROUP):
        process(units[g0:g0 + UNIT_GROUP])

    xbuf_ref[prev0:CONV_ROW0, :] = xbuf_ref[prev0 + R:CONV_ROW0 + R, :]


def gdn_prep(proj, conv0, conv_w, gate_par, *, C, G):
    b, t, _ = proj.shape
    r = C * G
    nc = t // C
    row_spec = pl.BlockSpec((1, r, D_MODEL), lambda i, c: (i, c, 0))
    conv_spec = pl.BlockSpec((1, CONV_W - 1, A_QKV), lambda i, c: (i, 0, 0))
    return pl.pallas_call(
        functools.partial(_gdn_prep_kernel, C=C, G=G),
        grid=(b, t // r),
        in_specs=[
            pl.BlockSpec((1, r, A_IN_PAD), lambda i, c: (i, c, 0)),
            conv_spec,
            pl.BlockSpec((CONV_W, A_QKV), lambda i, c: (0, 0)),
            pl.BlockSpec((2, LANES), lambda i, c: (0, 0)),
        ],
        out_specs=[
            row_spec, row_spec, row_spec, row_spec,
            pl.BlockSpec((1, r, HEADS * C), lambda i, c: (i, c, 0)),
            pl.BlockSpec((1, G, 1, LANES), lambda i, c: (i, c, 0, 0)),
            conv_spec,
        ],
        out_shape=[jax.ShapeDtypeStruct((b, t, D_MODEL), BF16)] * 4 + [
            jax.ShapeDtypeStruct((b, t, HEADS * C), BF16),
            jax.ShapeDtypeStruct((b, nc, 1, LANES), F32),
            jax.ShapeDtypeStruct((b, CONV_W - 1, A_QKV), F32),
        ],
        scratch_shapes=[pltpu.VMEM((CONV_ROW0 + r, A_QKV), F32)],
        compiler_params=_params("arbitrary", "arbitrary"),
        name="gdn_prep",
    )(proj, conv0, conv_w, gate_par)


def _gdn_scan_kernel(w_ref, uu_ref, qg_ref, kd_ref, qkd_ref, egl_ref, z_ref, s0_ref, og_ref, o_ref, s_ref, *, C, G):
    @pl.when(pl.program_id(1) == 0)
    def _():
        s_ref[0] = s0_ref[0]

    heads = range(HEADS)
    for gi in range(G):
        rs = slice(gi * C, (gi + 1) * C)
        egl = egl_ref[0, gi]

        def hs(h):
            return slice(h * HEAD_DIM, (h + 1) * HEAD_DIM)

        s_old = [s_ref[0, h] for h in heads]
        s_bf = [s.astype(BF16) for s in s_old]
        wq = [_dot(jnp.concatenate([w_ref[0, rs, hs(h)], qg_ref[0, rs, hs(h)]], axis=0), s_bf[h]) for h in heads]
        u_bf = [(uu_ref[0, rs, hs(h)].astype(F32) - wq[h][0:C]).astype(BF16) for h in heads]
        o = [wq[h][C:2 * C] + _dot(qkd_ref[0, rs, h * C:(h + 1) * C], u_bf[h]) for h in heads]
        for h in heads:
            s_ref[0, h] = egl[:, HEADS + h:HEADS + h + 1] * s_old[h] + _dot_tn(kd_ref[0, rs, hs(h)], u_bf[h])
        for h in heads:
            z = z_ref[0, rs, hs(h)]
            o_ref[0, rs, hs(h)] = (_rms(o[h], og_ref[...]) * (z * _sigmoid(z))).astype(o_ref.dtype)


def gdn_scan(w, uu, qg, kd, qkd, egl, proj, s0, o_gain, *, C, G):
    b, t, _ = w.shape
    r = C * G
    row_spec = pl.BlockSpec((1, r, D_MODEL), lambda i, c: (i, c, 0))
    state_spec = pl.BlockSpec((1, HEADS, HEAD_DIM, HEAD_DIM), lambda i, c: (i, 0, 0, 0))
    return pl.pallas_call(
        functools.partial(_gdn_scan_kernel, C=C, G=G),
        grid=(b, t // r),
        in_specs=[
            row_spec, row_spec, row_spec, row_spec,
            pl.BlockSpec((1, r, HEADS * C), lambda i, c: (i, c, 0)),
            pl.BlockSpec((1, G, 1, LANES), lambda i, c: (i, c, 0, 0)),
            pl.BlockSpec((1, r, D_MODEL), lambda i, c: (i, c, A_QKV // D_MODEL)),
            state_spec,
            pl.BlockSpec((1, HEAD_DIM), lambda i, c: (0, 0)),
        ],
        out_specs=[row_spec, state_spec],
        out_shape=[
            jax.ShapeDtypeStruct((b, t, D_MODEL), BF16),
            jax.ShapeDtypeStruct((b, HEADS, HEAD_DIM, HEAD_DIM), F32),
        ],
        compiler_params=_params("arbitrary", "arbitrary"),
        name="gdn_scan",
    )(w, uu, qg, kd, qkd, egl, proj, s0, o_gain.reshape(1, HEAD_DIM))


def _lambda(lam_ref, lam_init):
    lp = lam_ref[...]
    return (jnp.exp(jnp.sum(lp[0:1] * lp[1:2], axis=-1, keepdims=True))
            - jnp.exp(jnp.sum(lp[2:3] * lp[3:4], axis=-1, keepdims=True)) + lam_init)


DECODE_HEAD_GROUP = HEADS
KEY_COLS = 3


def _attn_prompt_kernel(qi_ref, kj_ref, slope_ref, lam_ref, sg_ref, kx_ref, q_ref, k_ref, vt_ref, o_ref,
                        s_even_ref, s_odd_ref, smax_even_ref, smax_odd_ref, diag_ref, m_ref, acc_ref,
                        *, tq, tk, col_block, n_pairs, lam_init):
    step = pl.program_id(1)
    slope = slope_ref[0, :, 0:1] * LOG2E
    lane = lax.broadcasted_iota(jnp.int32, (1, HEAD_DIM), 1)
    map_lanes = [lane < MAP_DIM, lane >= MAP_DIM]

    def score_stage(s_ref, smax_ref, mp, cols):
        q = q_ref[cols, :]
        k = k_ref[...]
        a0 = MAP_DIM if mp == 0 else 0
        extra = (lane >= a0) & (lane < a0 + KEY_COLS)
        qa = jnp.where(map_lanes[mp], q, jnp.where(extra, 1.0, 0.0).astype(BF16))
        ka = jnp.where(map_lanes[mp], k, kx_ref[0, mp])
        scores = _dot_nt(ka, qa)
        s_ref[mp, :, cols] = scores
        smax_ref[mp, :, cols] = jnp.max(scores, axis=0, keepdims=True)

    def mask_diagonal(s_ref, smax_ref, key_offset):
        facing = slice(key_offset, key_offset + tk)
        for mp in range(2):
            if key_offset:
                s_ref[mp, :, 0:key_offset] = jnp.full((tk, key_offset), NEG, F32)
                smax_ref[mp, :, 0:key_offset] = jnp.full((1, key_offset), NEG, F32)
            fixed = s_ref[mp, :, facing] + diag_ref[...]
            s_ref[mp, :, facing] = fixed
            smax_ref[mp, :, facing] = jnp.max(fixed, axis=0, keepdims=True)

    def build_diagonal_table():
        j = lax.broadcasted_iota(jnp.int32, (tk, 1), 0)
        i = lax.broadcasted_iota(jnp.int32, (1, tk), 1)
        allowed = (j >> CHUNK_SHIFT) <= (i >> CHUNK_SHIFT)
        fix = jnp.where(j > i, (2.0 * slope) * (i - j).astype(F32), 0.0)
        diag_ref[...] = jnp.where(allowed, fix, NEG)

    def softmax_stage(s_ref, smax_ref, shift_const, mp, cols):
        scores = s_ref[mp, :, cols]
        m_old = m_ref[mp, :, cols]
        m_new = jnp.maximum(m_old, smax_ref[mp, :, cols] + shift_const)
        probs = jnp.exp2(scores - (m_new - shift_const)).astype(BF16)
        m_ref[mp, :, cols] = m_new
        acc_ref[mp, :, cols] = jnp.exp2(m_old - m_new) * acc_ref[mp, :, cols] + _dot(vt_ref[...], probs)

    def finalize():
        lam = _lambda(lam_ref, lam_init)
        a0 = acc_ref[0]
        a1 = acc_ref[1]
        o_t = (a0[0:HEAD_DIM] / a0[HEAD_DIM:HEAD_DIM + 1]
               - lam * (a1[0:HEAD_DIM] / a1[HEAD_DIM:HEAD_DIM + 1]))
        o_t = o_t * (lax.rsqrt(jnp.mean(o_t * o_t, axis=0, keepdims=True) + EPS) * (1.0 - lam_init))
        o_ref[...] = (o_t.T * sg_ref[...]).astype(o_ref.dtype)

    units = [(mp, slice(c0, c0 + col_block)) for mp in range(2) for c0 in range(0, tq, col_block)]
    scored = jnp.minimum(step, n_pairs - 1)
    key_offset_s = kj_ref[scored] * tk - qi_ref[scored] * tq
    consumed = jnp.maximum(step - 1, 0)
    kj_c = kj_ref[consumed]
    key_offset_c = kj_c * tk - qi_ref[consumed] * tq

    @pl.when(step == 0)
    def _():
        build_diagonal_table()
        for mp, cols in units:
            score_stage(s_even_ref, smax_even_ref, mp, cols)

    @pl.when((step > 0) & (kj_c == 0))
    def _():
        m_ref[...] = jnp.full_like(m_ref, NEG)
        acc_ref[...] = jnp.zeros_like(acc_ref)

    tile_shift = slope * key_offset_c.astype(F32)
    even, odd = (s_even_ref, smax_even_ref), (s_odd_ref, smax_odd_ref)
    for parity, (written, read) in enumerate([(even, odd), (odd, even)]):
        @pl.when((step > 0) & (step % 2 == parity))
        def _():
            for mp, cols in units:
                score_stage(*written, mp, cols)
                softmax_stage(*read, tile_shift, mp, cols)

    for parity, written in enumerate([even, odd]):
        for key_offset in range(0, tq, tk):
            @pl.when((key_offset_s == key_offset) & (step % 2 == parity))
            def _():
                mask_diagonal(*written, key_offset)

    @pl.when((step > 0) & (key_offset_c == tq - tk))
    def _():
        finalize()


def alibi_slopes():
    return 2.0 ** (-8.0 * np.arange(1, HEADS + 1, dtype=np.float32) / HEADS)


def alibi_key_columns(t):
    term = (np.float32(LOG2E) * alibi_slopes())[:, None] * np.arange(t, dtype=np.float32)[None, :]
    cols = np.zeros((HEADS, 2, t, HEAD_DIM), np.float32)
    for c in range(KEY_COLS):
        piece = term.astype(BF16).astype(np.float32)
        term = term - piece
        for mp in range(2):
            a0 = MAP_DIM if mp == 0 else 0
            cols[:, mp, :, a0 + c] = piece
    return jnp.asarray(cols.astype(BF16))


def diff_attention_prompt(q, k, v_t, slopes, key_cols, lam_p, sub_gain, *, tq, tk, lam_init):
    t_all = q.shape[0]
    assert tq % tk == 0 and tk % CHUNK == 0 and t_all % tq == 0
    per_q = tq // tk
    qi = np.concatenate([np.full((i + 1) * per_q, i, np.int32) for i in range(t_all // tq)])
    kj = np.concatenate([np.arange((i + 1) * per_q, dtype=np.int32) for i in range(t_all // tq)])
    n_pairs = len(qi)

    def scored(p):
        return jnp.minimum(p, n_pairs - 1)

    def consumed(p):
        return jnp.maximum(p - 1, 0)

    grid_spec = pltpu.PrefetchScalarGridSpec(
        num_scalar_prefetch=2,
        grid=(HEADS, n_pairs + 1),
        in_specs=[
            pl.BlockSpec((1, 1, LANES), lambda h, p, qi_r, kj_r: (h, 0, 0)),
            pl.BlockSpec((4, MAP_DIM), lambda h, p, qi_r, kj_r: (0, 0)),
            pl.BlockSpec((1, HEAD_DIM), lambda h, p, qi_r, kj_r: (0, 0)),
            pl.BlockSpec((1, 2, tk, HEAD_DIM), lambda h, p, qi_r, kj_r: (h, 0, 0, 0)),
            pl.BlockSpec((tq, HEAD_DIM), lambda h, p, qi_r, kj_r: (qi_r[scored(p)], h)),
            pl.BlockSpec((tk, HEAD_DIM), lambda h, p, qi_r, kj_r: (kj_r[scored(p)], h)),
            pl.BlockSpec((HEAD_ROWS_T, tk), lambda h, p, qi_r, kj_r: (h, kj_r[consumed(p)])),
        ],
        out_specs=pl.BlockSpec((tq, HEAD_DIM), lambda h, p, qi_r, kj_r: (qi_r[consumed(p)], h)),
        scratch_shapes=[
            pltpu.VMEM((2, tk, tq), F32), pltpu.VMEM((2, tk, tq), F32),
            pltpu.VMEM((2, 1, tq), F32), pltpu.VMEM((2, 1, tq), F32),
            pltpu.VMEM((tk, tk), F32),
            pltpu.VMEM((2, 1, tq), F32), pltpu.VMEM((2, HEAD_ROWS_T, tq), F32),
        ],
    )
    return pl.pallas_call(
        functools.partial(_attn_prompt_kernel, tq=tq, tk=tk, col_block=tq, n_pairs=n_pairs, lam_init=lam_init),
        grid_spec=grid_spec,
        out_shape=jax.ShapeDtypeStruct((t_all, D_MODEL), BF16),
        compiler_params=_params("arbitrary", "arbitrary"),
        name="diff_attention_prompt",
    )(jnp.asarray(qi), jnp.asarray(kj), slopes, lam_p, sub_gain.reshape(1, HEAD_DIM), key_cols, q, k, v_t)


def _attn_decode_kernel(slope_ref, lam_ref, sg_ref, q_ref, k_ref, v_ref, kn_ref, vn_ref, o_ref, m_ref, l_ref, acc_ref,
                        *, tk, pos0, lam_init):
    j = pl.program_id(1)
    nk = pl.num_programs(1)
    n = q_ref.shape[1]

    @pl.when(j == 0)
    def _():
        m_ref[...] = jnp.full_like(m_ref, NEG)
        l_ref[...] = jnp.zeros_like(l_ref)
        acc_ref[...] = jnp.zeros_like(acc_ref)

    row = lax.broadcasted_iota(jnp.int32, (2 * n, 1), 0)
    q_pos = pos0 + jnp.where(row < n, row, row - n)
    lane = lax.broadcasted_iota(jnp.int32, (1, HEAD_DIM), 1)
    map_of_row_has_lane = (row < n) == (lane < MAP_DIM)

    def update(keys, values, k0, n_keys):
        k_pos = k0 + lax.broadcasted_iota(jnp.int32, (1, n_keys), 1)
        allowed = (k_pos >> CHUNK_SHIFT) <= (q_pos >> CHUNK_SHIFT)
        dist = jnp.abs(q_pos - k_pos).astype(F32)
        for h0 in range(0, HEADS, DECODE_HEAD_GROUP):
            heads = range(h0, h0 + DECODE_HEAD_GROUP)
            scores, probs, alphas = {}, {}, {}
            for h in heads:
                slope = slope_ref[h, :, 0:1] * LOG2E
                q = q_ref[0, :, h * HEAD_DIM:(h + 1) * HEAD_DIM]
                q2 = jnp.concatenate([q, q], axis=0)
                q_stack = jnp.where(map_of_row_has_lane, q2, jnp.zeros_like(q2))
                scores[h] = jnp.where(allowed, _dot_nt(q_stack, keys(h).astype(BF16)) - slope * dist, NEG)
            for h in heads:
                m_old = m_ref[h]
                m_new = jnp.maximum(m_old, jnp.max(scores[h], axis=-1, keepdims=True))
                alphas[h] = jnp.exp2(m_old - m_new)
                p = jnp.exp2(scores[h] - m_new)
                l_ref[h] = alphas[h] * l_ref[h] + jnp.sum(p, axis=-1, keepdims=True)
                probs[h] = p.astype(BF16)
                m_ref[h] = m_new
            pv = {h: _dot(probs[h], values(h).astype(BF16)) for h in heads}
            for h in heads:
                acc_ref[h] = alphas[h] * acc_ref[h] + pv[h]

    def head_rows(h):
        return pl.ds(h, tk, stride=HEADS)

    update(lambda h: k_ref[0, head_rows(h), :], lambda h: v_ref[0, head_rows(h), :], j * tk, tk)

    @pl.when(j == nk - 1)
    def _():
        update(lambda h: kn_ref[0, :, h * HEAD_DIM:(h + 1) * HEAD_DIM],
               lambda h: vn_ref[0, :, h * HEAD_DIM:(h + 1) * HEAD_DIM], pos0, n)
        lam = _lambda(lam_ref, lam_init)
        for h in range(HEADS):
            o = acc_ref[h] / l_ref[h]
            o = o[0:n] - lam * o[n:2 * n]
            o_ref[0, :, h * HEAD_DIM:(h + 1) * HEAD_DIM] = (_rms(o, sg_ref[...]) * (1.0 - lam_init)).astype(o_ref.dtype)


def diff_attention_decode(q, k, v, k_new, v_new, slopes, lam_p, sub_gain, *, tk, lam_init):
    b, n, _ = q.shape
    t_k = k.shape[1]
    assert t_k % tk == 0 and t_k % CHUNK == 0
    k = k.reshape(b, t_k * HEADS, HEAD_DIM)
    v = v.reshape(b, t_k * HEADS, HEAD_DIM)
    row_block = pl.BlockSpec((1, n, D_MODEL), lambda bi, j: (bi, 0, 0))
    kv_spec = pl.BlockSpec((1, tk * HEADS, HEAD_DIM), lambda bi, j: (bi, j, 0))
    return pl.pallas_call(
        functools.partial(_attn_decode_kernel, tk=tk, pos0=t_k, lam_init=lam_init),
        grid=(b, t_k // tk),
        in_specs=[
            pl.BlockSpec((HEADS, 1, LANES), lambda bi, j: (0, 0, 0)),
            pl.BlockSpec((4, MAP_DIM), lambda bi, j: (0, 0)),
            pl.BlockSpec((1, HEAD_DIM), lambda bi, j: (0, 0)),
            row_block, kv_spec, kv_spec, row_block, row_block,
        ],
        out_specs=row_block,
        out_shape=jax.ShapeDtypeStruct((b, n, D_MODEL), BF16),
        scratch_shapes=[
            pltpu.VMEM((HEADS, 2 * n, 1), F32),
            pltpu.VMEM((HEADS, 2 * n, 1), F32),
            pltpu.VMEM((HEADS, 2 * n, HEAD_DIM), F32),
        ],
        compiler_params=_params("arbitrary", "arbitrary"),
        name="diff_attention_decode",
    )(slopes, lam_p, sub_gain.reshape(1, HEAD_DIM), q, k, v, k_new, v_new)


class Tiles(NamedTuple):
    wide_rows: int
    rows: int
    mlp_hidden: int
    gdn_chunk: int
    gdn_group: int
    attn_q: int
    attn_k: int


def _tiles(batch, seq):
    rows = batch * seq
    if seq > CHUNK:
        return Tiles(wide_rows=512, rows=1024, mlp_hidden=2048, gdn_chunk=CHUNK, gdn_group=4, attn_q=1024, attn_k=512)
    return Tiles(wide_rows=rows, rows=rows, mlp_hidden=2048, gdn_chunk=seq, gdn_group=1, attn_q=seq, attn_k=2048)


def _trunk(x, conv_state, ssm_state, past_k, past_v, wts):
    b, t, _ = x.shape
    m = b * t
    tiles = _tiles(b, t)
    xf = x.reshape(m, D_MODEL)
    new_conv, new_ssm = [], []
    k_new = v_new = k_bf = v_t = None
    for l in range(DEPTH):
        gains = wts["norm_gains"][l]
        if l < N_A:
            proj = norm_matmul(xf, gains[0], wts["a_w_in"], layer=l, tm=tiles.wide_rows,
                               tn=A_IN_PAD).reshape(b, t, A_IN_PAD)
            w, uu, qg, kd, qkd, egl, conv_l = gdn_prep(proj, conv_state[l], wts["a_conv_w"][l], wts["gate_par"][l],
                                                       C=tiles.gdn_chunk, G=tiles.gdn_group)
            og, s_l = gdn_scan(w, uu, qg, kd, qkd, egl, proj, ssm_state[l], wts["a_o_gain"][l],
                               C=tiles.gdn_chunk, G=tiles.gdn_group)
            new_conv.append(conv_l)
            new_ssm.append(s_l)
            xf = matmul_norm_res(og.reshape(m, D_MODEL), wts["a_w_out"][l], gains[1], xf, tm=tiles.rows)
        else:
            j = l - N_A
            prompt = past_k is None
            if j == 0:
                kv = dict(tm=tiles.rows, tn=D_MODEL, head_major_first=prompt)
                k_new, k_bf = norm_matmul(xf, wts["kv_gain"], wts["w_kv"], columns=(0, D_MODEL),
                                          out_dtypes=(F32, BF16), **kv)
                v_new = norm_matmul(xf, wts["kv_gain"], wts["w_kv"], columns=(D_MODEL, D_MODEL),
                                    head_transposed_copy=prompt, **kv)
                if prompt:
                    v_new, v_t = v_new
            q = norm_matmul(xf, gains[0], wts["b_w_q"][j], tm=tiles.rows, tn=D_MODEL, out_dtypes=(BF16,), out_scale=Q_SCALE)
            lam_init = 0.8 - 0.6 * math.exp(-0.3 * l)
            tail = (wts["b_lam"][j], wts["b_sub_gain"][j])
            if prompt:
                assert b == 1
                o = diff_attention_prompt(q, k_bf, v_t, wts["slopes"], alibi_key_columns(tiles.attn_k),
                                          *tail, tq=tiles.attn_q, tk=tiles.attn_k, lam_init=lam_init)
            else:
                o = diff_attention_decode(q.reshape(b, t, D_MODEL), past_k, past_v, k_new.reshape(b, t, D_MODEL),
                                          v_new.reshape(b, t, D_MODEL), wts["slopes"], *tail, tk=tiles.attn_k,
                                          lam_init=lam_init)
            xf = matmul_norm_res(o.reshape(m, D_MODEL), wts["b_w_out"][j], gains[1], xf, tm=tiles.rows)
        xf = mlp_block(xf, gains[2], wts["mlp_w1"], wts["mlp_w2"], gains[3], layer=l,
                       tm=tiles.wide_rows, tf=tiles.mlp_hidden)
    kv_shape = (b, t, HEADS, HEAD_DIM)
    return (xf.reshape(b, t, D_MODEL), jnp.stack(new_conv), jnp.stack(new_ssm),
            k_new.reshape(kv_shape), v_new.reshape(kv_shape))


def kernel(x_prompt, x_sample, state_conv, state_ssm, cache_k, cache_v, norm_gains, a_w_in, a_conv_w, a_log,
           a_dt_bias, a_o_gain, a_w_out, kv_gain, w_kv, b_w_q, b_lam, b_sub_gain, b_w_out, mlp_w1, mlp_w2):
    a_in = a_w_in.shape[-1]
    gate_par = jnp.zeros((N_A, 2, LANES), F32)
    gate_par = gate_par.at[:, 0, HEADS:2 * HEADS].set(a_log.astype(F32))
    gate_par = gate_par.at[:, 1, HEADS:2 * HEADS].set(a_dt_bias.astype(F32))
    slopes = jnp.asarray(alibi_slopes())
    wts = dict(
        norm_gains=norm_gains.astype(F32),
        a_w_in=jnp.pad(a_w_in, ((0, 0), (0, 0), (0, A_IN_PAD - a_in))).astype(BF16),
        a_conv_w=a_conv_w.astype(F32),
        gate_par=gate_par,
        a_o_gain=a_o_gain.astype(F32),
        a_w_out=a_w_out.astype(BF16),
        kv_gain=kv_gain.astype(F32),
        w_kv=w_kv.astype(BF16),
        b_w_q=b_w_q.astype(BF16),
        b_lam=b_lam.astype(F32),
        b_sub_gain=b_sub_gain.astype(F32),
        b_w_out=b_w_out.astype(BF16),
        mlp_w1=mlp_w1.astype(BF16),
        mlp_w2=mlp_w2.astype(BF16),
        slopes=jnp.broadcast_to(slopes[:, None, None], (HEADS, 1, LANES)),
    )
    bp = x_prompt.shape[0]
    conv0 = jnp.zeros((N_A, bp, CONV_W - 1, A_QKV), F32)
    ssm0 = jnp.zeros((N_A, bp, HEADS, HEAD_DIM, HEAD_DIM), F32)
    y_p, p_conv, p_ssm, p_k, p_v = _trunk(x_prompt, conv0, ssm0, None, None, wts)
    y_s, s_conv, s_ssm, s_k, s_v = _trunk(x_sample, state_conv, state_ssm, cache_k, cache_v, wts)
    return (y_p, y_s, p_conv, p_ssm, p_k, p_v, s_conv, s_ssm, s_k, s_v)
```
